```python
import math
import jax, jax.numpy as jnp
from jax import lax
import numpy as np

D_MODEL = 1024
BATCH = 2
SEQ = 16384
DEPTH = 4

GRID_W = 64
CTX_LEN = 256
HEAD_DIM = 128
ATTN_HEADS = 4
ATTN_KV_HEADS = 2
DN_HEADS = 4
ATTN_Q_W = ATTN_HEADS * HEAD_DIM
ATTN_KV_W = ATTN_KV_HEADS * HEAD_DIM
DN_W = DN_HEADS * HEAD_DIM
MIX_WIDTH = ATTN_Q_W + DN_W
IN_SPLITS = (ATTN_Q_W, ATTN_KV_W, ATTN_KV_W, 3 * DN_W, DN_W, 2 * DN_HEADS, 2 * DN_HEADS)
IN_WIDTH = sum(IN_SPLITS)
ROPE_THETA = 10000.0
Q_BLOCK = 128
DN_CHUNK = 64
CONV_K = 5
PEER_HEADS = 8
PEER_QDIM = 256
PEER_HALF = PEER_QDIM // 2
N_KEYS = 128
N_EXPERTS = N_KEYS * N_KEYS
PEER_TOPK = 16
PEER_BLOCK = 128
EPS = 1e-6

kernel_name = 'hybrid_gqa_deltanet_peer_dit'


def rmsnorm(x, w):
    xf = x.astype(jnp.float32)
    y = xf * lax.rsqrt(jnp.mean(xf * xf, axis=-1, keepdims=True) + EPS)
    return (y * w.astype(jnp.float32)).astype(x.dtype)


def l2norm(x):
    return x * lax.rsqrt(jnp.sum(x * x, axis=-1, keepdims=True) + EPS)


def axial_rope_tables(n_tokens):
    rows = n_tokens // GRID_W
    row = jnp.repeat(jnp.arange(rows, dtype=jnp.float32), GRID_W)
    col = jnp.tile(jnp.arange(GRID_W, dtype=jnp.float32), rows)
    axis_dim = HEAD_DIM // 2
    inv_freq = ROPE_THETA ** (-jnp.arange(0, axis_dim, 2, dtype=jnp.float32) / axis_dim)
    ang_r = row[:, None] * inv_freq[None, :]
    ang_c = col[:, None] * inv_freq[None, :]
    ang = jnp.concatenate([ang_r, ang_r, ang_c, ang_c], axis=-1)
    return jnp.cos(ang), jnp.sin(ang)


def apply_axial_rope(x, cos, sin):
    B, T, H, Dh = x.shape
    xr = x.reshape(B, T, H, 2, 2, Dh // 4)
    rot = jnp.stack([-xr[..., 1, :], xr[..., 0, :]], axis=-2).reshape(B, T, H, Dh)
    return (x * cos[None, :, None, :] + rot * sin[None, :, None, :]).astype(x.dtype)


def split_in(p):
    offs = np.cumsum(IN_SPLITS)[:-1].tolist()
    return jnp.split(p, offs, axis=-1)


def attn_group(qa, ka, va, qn_w, kn_w, rope):
    B, T, _ = qa.shape
    q = rmsnorm(qa.reshape(B, T, ATTN_HEADS, HEAD_DIM), qn_w)
    k = rmsnorm(ka.reshape(B, T, ATTN_KV_HEADS, HEAD_DIM), kn_w)
    v = va.reshape(B, T, ATTN_KV_HEADS, HEAD_DIM)
    if rope is not None:
        q = apply_axial_rope(q, rope[0], rope[1])
        k = apply_axial_rope(k, rope[0], rope[1])
    return q, k, v


def latent_attention(q, k_lat, v_lat, k_ctx, v_ctx):
    B, T = q.shape[:2]
    G = ATTN_HEADS // ATTN_KV_HEADS
    scale = HEAD_DIM ** -0.5
    k_all = jnp.concatenate([k_lat, k_ctx], axis=1).transpose(0, 2, 1, 3)
    v_all = jnp.concatenate([v_lat, v_ctx], axis=1).transpose(0, 2, 1, 3)
    nb = T // Q_BLOCK
    qb = q.reshape(B, nb, Q_BLOCK, ATTN_KV_HEADS, G, HEAD_DIM).transpose(1, 0, 3, 4, 2, 5)

    def block(qi):
        s = jnp.einsum('bhgqd,bhkd->bhgqk', qi, k_all).astype(jnp.float32) * scale
        p = jax.nn.softmax(s, axis=-1).astype(v_all.dtype)
        return jnp.einsum('bhgqk,bhkd->bhgqd', p, v_all)

    o = lax.map(block, qb)
    return o.transpose(1, 0, 4, 2, 3, 5).reshape(B, T, ATTN_Q_W)


def context_attention(q, k, v):
    B, T = q.shape[:2]
    G = ATTN_HEADS // ATTN_KV_HEADS
    qg = q.reshape(B, T, ATTN_KV_HEADS, G, HEAD_DIM)
    s = jnp.einsum('bqhgd,bkhd->bhgqk', qg, k).astype(jnp.float32) * (HEAD_DIM ** -0.5)
    p = jax.nn.softmax(s, axis=-1).astype(v.dtype)
    o = jnp.einsum('bhgqk,bkhd->bqhgd', p, v)
    return o.reshape(B, T, ATTN_Q_W)


def centred_short_conv(x, w):
    pad = CONV_K // 2
    T = x.shape[1]
    xp = jnp.pad(x, ((0, 0), (pad, pad), (0, 0)))
    y = xp[:, 0:T] * w[0]
    for j in range(1, CONV_K):
        y = y + xp[:, j:j + T] * w[j]
    return jax.nn.silu(y)


def gated_delta_chunked(q, k, v, g, beta, S0):
    B, T, H, Dk = q.shape
    C = DN_CHUNK
    N = T // C
    qc, kc, vc = [t.reshape(B, N, C, H, -1).transpose(1, 0, 3, 2, 4) for t in (q, k, v)]
    gcum = jnp.cumsum(g.reshape(B, N, C, H).transpose(1, 0, 3, 2), axis=-1)
    bc = beta.reshape(B, N, C, H).transpose(1, 0, 3, 2)
    idx = jnp.arange(C)
    incl = idx[:, None] >= idx[None, :]
    strict = idx[:, None] > idx[None, :]
    decay = jnp.exp(jnp.where(incl, gcum[..., :, None] - gcum[..., None, :], -jnp.inf))
    kb = kc * bc[..., None]
    A = jnp.where(strict, jnp.einsum('nbhid,nbhjd->nbhij', kb, kc) * decay, 0.0)
    Tm = jnp.eye(C, dtype=jnp.float32) + A
    u = lax.linalg.triangular_solve(Tm, vc * bc[..., None], left_side=True, lower=True, unit_diagonal=True)
    w = lax.linalg.triangular_solve(Tm, kb * jnp.exp(gcum)[..., None], left_side=True, lower=True, unit_diagonal=True)
    qk = jnp.einsum('nbhid,nbhjd->nbhij', qc, kc) * decay

    def step(S, xs):
        q_i, k_i, u_i, w_i, g_i, qk_i = xs
        v_new = u_i - jnp.einsum('bhcd,bhde->bhce', w_i, S)
        o_i = (jnp.einsum('bhcd,bhde->bhce', q_i * jnp.exp(g_i)[..., None], S)
               + jnp.einsum('bhij,bhje->bhie', qk_i, v_new))
        g_last = g_i[..., -1:]
        S = (S * jnp.exp(g_last)[..., None]
             + jnp.einsum('bhcd,bhce->bhde', k_i * jnp.exp(g_last - g_i)[..., None], v_new))
        return S, o_i

    S_final, o = lax.scan(step, S0, (qc, kc, u, w, gcum, qk))
    o = o.transpose(1, 0, 3, 2, 4).reshape(B, T, H, v.shape[-1])
    return o, S_final


def dn_prepare(qkv, b_fb, a_fb, conv_w, A_log, dt_bias):
    B, T, _ = qkv.shape
    y = centred_short_conv(qkv, conv_w).astype(jnp.float32)
    q, k, v = jnp.split(y, 3, axis=-1)
    q = l2norm(q.reshape(B, T, DN_HEADS, HEAD_DIM)) * (HEAD_DIM ** -0.5)
    k = l2norm(k.reshape(B, T, DN_HEADS, HEAD_DIM))
    v = v.reshape(B, T, DN_HEADS, HEAD_DIM)
    a = a_fb.astype(jnp.float32).reshape(B, T, 2, DN_HEADS)
    g = -jnp.exp(A_log.astype(jnp.float32)) * jax.nn.softplus(a + dt_bias.astype(jnp.float32))
    beta = jax.nn.sigmoid(b_fb.astype(jnp.float32).reshape(B, T, 2, DN_HEADS))
    return q, k, v, g, beta


def dn_bidirectional(lat, ctx):
    q_l, k_l, v_l, g_l, b_l = lat
    q_c, k_c, v_c, g_c, b_c = ctx
    B = q_l.shape[0]
    S0 = jnp.zeros((B, DN_HEADS, HEAD_DIM, HEAD_DIM), jnp.float32)

    def flip(t):
        return jnp.flip(t, axis=1)

    o_cf, S_f = gated_delta_chunked(q_c, k_c, v_c, g_c[:, :, 0], b_c[:, :, 0], S0)
    o_lf, _ = gated_delta_chunked(q_l, k_l, v_l, g_l[:, :, 0], b_l[:, :, 0], S_f)
    o_cb, S_b = gated_delta_chunked(flip(q_c), flip(k_c), flip(v_c), flip(g_c[:, :, 1]), flip(b_c[:, :, 1]), S0)
    o_lb, _ = gated_delta_chunked(flip(q_l), flip(k_l), flip(v_l), flip(g_l[:, :, 1]), flip(b_l[:, :, 1]), S_b)
    return o_lf + flip(o_lb), o_cf + flip(o_cb)


def dn_output(o, gate, norm_w, dtype):
    B, T = o.shape[:2]
    y = rmsnorm(o, norm_w) * jax.nn.silu(gate.astype(jnp.float32).reshape(B, T, DN_HEADS, HEAD_DIM))
    return y.reshape(B, T, DN_W).astype(dtype)


def hybrid_mixer(h_lat, h_ctx, w_in, qn_w, kn_w, conv_w, A_log, dt_bias, dn_norm_w, w_out, rope, with_ctx):
    qa_l, ka_l, va_l, qkv_l, gate_l, b_l, a_l = split_in(h_lat @ w_in)
    qa_c, ka_c, va_c, qkv_c, gate_c, b_c, a_c = split_in(h_ctx @ w_in)
    q_l, k_l, v_l = attn_group(qa_l, ka_l, va_l, qn_w, kn_w, rope)
    q_c, k_c, v_c = attn_group(qa_c, ka_c, va_c, qn_w, kn_w, None)
    attn_lat = latent_attention(q_l, k_l, v_l, k_c, v_c)
    dn_lat, dn_ctx = dn_bidirectional(dn_prepare(qkv_l, b_l, a_l, conv_w, A_log, dt_bias),
                                      dn_prepare(qkv_c, b_c, a_c, conv_w, A_log, dt_bias))
    o_lat = jnp.concatenate([attn_lat, dn_output(dn_lat, gate_l, dn_norm_w, h_lat.dtype)], axis=-1) @ w_out
    if not with_ctx:
        return o_lat, None
    attn_ctx = context_attention(q_c, k_c, v_c)
    o_ctx = jnp.concatenate([attn_ctx, dn_output(dn_ctx, gate_c, dn_norm_w, h_ctx.dtype)], axis=-1) @ w_out
    return o_lat, o_ctx


def peer_ffn(h, wq, subkeys, u_tab, v_tab):
    B, T, D = h.shape
    tok = h.reshape(-1, PEER_BLOCK, D)

    def block(hb):
        q = (hb @ wq).reshape(PEER_BLOCK, PEER_HEADS, 2, PEER_HALF)
        s = jnp.einsum('thpd,hpkd->thpk', q, subkeys).astype(jnp.float32)
        sv, si = lax.top_k(s, PEER_TOPK)
        cand_s = (sv[..., 0, :, None] + sv[..., 1, None, :]).reshape(PEER_BLOCK, PEER_HEADS, PEER_TOPK * PEER_TOPK)
        cand_i = (si[..., 0, :, None] * N_KEYS + si[..., 1, None, :]).reshape(PEER_BLOCK, PEER_HEADS, PEER_TOPK * PEER_TOPK)
        top_s, pos = lax.top_k(cand_s, PEER_TOPK)
        expert = jnp.take_along_axis(cand_i, pos, axis=-1)
        gate = jax.nn.softmax(top_s, axis=-1)
        u = jnp.take(u_tab, expert, axis=0)
        act = jax.nn.gelu(jnp.einsum('thkd,td->thk', u, hb).astype(jnp.float32), approximate=False)
        vv = jnp.take(v_tab, expert, axis=0)
        return jnp.einsum('thk,thkd->td', (gate * act).astype(hb.dtype), vv)

    return lax.map(block, tok).reshape(B, T, D)


def setup_inputs(seed: int = 0) -> dict:
    key = jax.random.key(seed)
    ks = jax.random.split(key, 20)
    f32 = jnp.float32

    def nrm(k, shape, scale):
        return jax.random.normal(k, shape, f32) * scale

    dt = jnp.exp(jax.random.uniform(ks[13], (DEPTH, 2, DN_HEADS), f32, math.log(1e-3), math.log(1e-1)))
    return {
        'x': nrm(ks[0], (BATCH, SEQ, D_MODEL), 1.0),
        'c': nrm(ks[1], (BATCH, D_MODEL), 1.0),
        'ctx': nrm(ks[2], (BATCH, CTX_LEN, D_MODEL), 1.0),
        'c_ctx': nrm(ks[3], (D_MODEL,), 1.0),
        'ada_w': nrm(ks[4], (DEPTH, D_MODEL, 6 * D_MODEL), 0.5 * D_MODEL ** -0.5),
        'ada_b': nrm(ks[5], (DEPTH, 6 * D_MODEL), 0.02),
        'norm1_w': 1.0 + nrm(ks[6], (DEPTH, D_MODEL), 0.02),
        'norm2_w': 1.0 + nrm(ks[7], (DEPTH, D_MODEL), 0.02),
        'w_in': nrm(ks[8], (DEPTH, D_MODEL, IN_WIDTH), D_MODEL ** -0.5),
        'attn_qnorm_w': 1.0 + nrm(ks[9], (DEPTH, HEAD_DIM), 0.02),
        'attn_knorm_w': 1.0 + nrm(ks[10], (DEPTH, HEAD_DIM), 0.02),
        'dn_conv_w': nrm(ks[11], (DEPTH, CONV_K, 3 * DN_W), CONV_K ** -0.5),
        'dn_A_log': jnp.log(jax.random.uniform(ks[12], (DEPTH, 2, DN_HEADS), f32, 1.0, 16.0)),
        'dn_dt_bias': dt + jnp.log(-jnp.expm1(-dt)),
        'dn_norm_w': 1.0 + nrm(ks[14], (DEPTH, HEAD_DIM), 0.02),
        'w_out': nrm(ks[15], (DEPTH, MIX_WIDTH, D_MODEL), MIX_WIDTH ** -0.5),
        'peer_wq': nrm(ks[16], (DEPTH, D_MODEL, PEER_HEADS * PEER_QDIM), D_MODEL ** -0.5),
        'peer_subkeys': nrm(ks[17], (DEPTH, PEER_HEADS, 2, N_KEYS, PEER_HALF), PEER_HALF ** -0.5),
        'peer_u': nrm(ks[18], (DEPTH, N_EXPERTS, D_MODEL), D_MODEL ** -0.5),
        'peer_v': nrm(ks[19], (DEPTH, N_EXPERTS, D_MODEL), PEER_HEADS ** -0.5),
    }


def reference(x, c, ctx, c_ctx, ada_w, ada_b, norm1_w, norm2_w, w_in, attn_qnorm_w, attn_knorm_w,
              dn_conv_w, dn_A_log, dn_dt_bias, dn_norm_w, w_out, peer_wq, peer_subkeys, peer_u, peer_v):
    T = x.shape[1]
    rope = axial_rope_tables(T)
    for l in range(DEPTH):
        with_ctx = l < DEPTH - 1
        mod_lat = (jax.nn.silu(c) @ ada_w[l] + ada_b[l])[:, None, :]
        mod_ctx = (jax.nn.silu(c_ctx) @ ada_w[l] + ada_b[l])[None, None, :]
        sh1, sc1, g1, sh2, sc2, g2 = jnp.split(mod_lat, 6, axis=-1)
        csh1, csc1, cg1, csh2, csc2, cg2 = jnp.split(mod_ctx, 6, axis=-1)
        h_lat = rmsnorm(x, norm1_w[l]) * (1.0 + sc1) + sh1
        h_ctx = rmsnorm(ctx, norm1_w[l]) * (1.0 + csc1) + csh1
        o_lat, o_ctx = hybrid_mixer(h_lat, h_ctx, w_in[l], attn_qnorm_w[l], attn_knorm_w[l], dn_conv_w[l],
                                    dn_A_log[l], dn_dt_bias[l], dn_norm_w[l], w_out[l], rope, with_ctx)
        x = x + g1 * o_lat
        h2 = rmsnorm(x, norm2_w[l]) * (1.0 + sc2) + sh2
        x = x + g2 * peer_ffn(h2, peer_wq[l], peer_subkeys[l], peer_u[l], peer_v[l])
        if with_ctx:
            ctx = ctx + cg1 * o_ctx
            hc2 = rmsnorm(ctx, norm2_w[l]) * (1.0 + csc2) + csh2
            ctx = ctx + cg2 * peer_ffn(hc2, peer_wq[l], peer_subkeys[l], peer_u[l], peer_v[l])
    return x
```

```python
import functools
import math

import jax
import jax.numpy as jnp
from jax import lax
from jax.experimental import pallas as pl
from jax.experimental.pallas import tpu as pltpu

F32 = jnp.float32
BF16 = jnp.bfloat16
HIGHEST = lax.Precision.HIGHEST

HEAD_DIM = 128
ATTN_HEADS = 4
ATTN_KV_HEADS = 2
ATTN_GROUP = ATTN_HEADS // ATTN_KV_HEADS
DN_HEADS = 4
ATTN_Q_W = ATTN_HEADS * HEAD_DIM
ATTN_KV_W = ATTN_KV_HEADS * HEAD_DIM
DN_W = DN_HEADS * HEAD_DIM
IN_MAIN_W = ATTN_Q_W + 2 * ATTN_KV_W + 3 * DN_W + DN_W
N_GATE_COLS = 4 * DN_HEADS
ROPE_THETA = 10000.0
GRID_W = 64
CONV_K = 5
CONV_PAD = CONV_K // 2
PEER_HEADS = 8
PEER_HALF = 128
N_KEYS = 128
PEER_TOPK = 16
EPS = 1e-6
NEG_BIG = -1e30

LANES = 128
SUBLANES = 8
TOKEN_BLOCK = 256
DN_CHUNK = 128
ATTN_KEY_BLOCK = 1280
PEER_TOKEN_BLOCK = 512
PEER_I_PER_STEP = 8
VMEM_LIMIT = 56 * 1024 * 1024


def _params(sem):
    return pltpu.CompilerParams(dimension_semantics=sem, vmem_limit_bytes=VMEM_LIMIT)


def _bf16_dot(a, b):
    return jnp.dot(a.astype(BF16), b.astype(BF16), preferred_element_type=F32)


def _split_dot(a, b):
    a_hi = a.astype(BF16)
    b_hi = b.astype(BF16)
    a_lo = (a - a_hi.astype(F32)).astype(BF16)
    b_lo = (b - b_hi.astype(F32)).astype(BF16)
    dot = functools.partial(jnp.dot, preferred_element_type=F32)
    return dot(a_hi, b_hi) + (dot(a_hi, b_lo) + dot(a_lo, b_hi))


def _bf16_dot_nt(a, b):
    return lax.dot_general(a.astype(BF16), b.astype(BF16), (((1,), (1,)), ((), ())),
                           preferred_element_type=F32)


def _ada_kernel(c_ref, w_ref, b_ref, o_ref):
    c = c_ref[...]
    a = c * jax.nn.sigmoid(c)
    o_ref[0] = jnp.dot(a, w_ref[0], preferred_element_type=F32, precision=HIGHEST) + b_ref[0]


def ada_modulation(cc, ada_w, ada_b):
    depth, d, n = ada_w.shape
    tn = 1536
    return pl.pallas_call(
        _ada_kernel,
        out_shape=jax.ShapeDtypeStruct((depth, SUBLANES, n), F32),
        grid=(depth, n // tn),
        in_specs=[pl.BlockSpec((SUBLANES, d), lambda l, j: (0, 0)),
                  pl.BlockSpec((1, d, tn), lambda l, j: (l, 0, j)),
                  pl.BlockSpec((1, 1, tn), lambda l, j: (l, 0, j))],
        out_specs=pl.BlockSpec((1, SUBLANES, tn), lambda l, j: (l, 0, j)),
        compiler_params=_params(("parallel", "parallel")),
        name="ada_modulation",
    )(cc, ada_w, ada_b.reshape(depth, 1, n))


def _mod_index(b, j):
    return (b, jnp.minimum(j, 1), 0, 0)


def _modulated_norm(x, nw, shift, scale):
    ms = jnp.mean(x * x, axis=-1, keepdims=True)
    y = x * lax.rsqrt(ms + EPS) * nw
    return y * (1.0 + scale) + shift


def _head_rmsnorm(x, w):
    return x * lax.rsqrt(jnp.mean(x * x, axis=-1, keepdims=True) + EPS) * w


def _inproj_kernel(x_ref, mod_ref, nw_ref, wm_ref, wg_ref, cos_ref, sina_ref, sinb_ref, qnw_ref, knw_ref,
                   q_out, k_out, v_out, dn_out, gate_out, ba_out):
    d = x_ref.shape[-1]
    x = x_ref[0]
    h = _modulated_norm(x, nw_ref[...], mod_ref[0, 0, :, 0:d], mod_ref[0, 0, :, d:2 * d]).astype(BF16)
    p = jnp.dot(h, wm_ref[...], preferred_element_type=F32)
    ba_out[0] = jnp.dot(h, wg_ref[...], preferred_element_type=F32)
    cos = cos_ref[...]
    sina = sina_ref[...]
    sinb = sinb_ref[...]

    def rope(t):
        return (t * cos + pltpu.roll(t, HEAD_DIM - HEAD_DIM // 4, 1) * sina
                + pltpu.roll(t, HEAD_DIM // 4, 1) * sinb)

    scale = HEAD_DIM ** -0.5
    for hd in range(ATTN_HEADS):
        qh = _head_rmsnorm(p[:, hd * HEAD_DIM:(hd + 1) * HEAD_DIM], qnw_ref[...])
        q_out[0, hd] = (rope(qh) * scale).astype(BF16)
    off = ATTN_Q_W
    for hd in range(ATTN_KV_HEADS):
        kh = _head_rmsnorm(p[:, off + hd * HEAD_DIM:off + (hd + 1) * HEAD_DIM], knw_ref[...])
        k_out[0, hd] = rope(kh).astype(BF16)
    off += ATTN_KV_W
    for hd in range(ATTN_KV_HEADS):
        v_out[0, hd] = p[:, off + hd * HEAD_DIM:off + (hd + 1) * HEAD_DIM].astype(BF16)
    off += ATTN_KV_W
    dn_out[0] = p[:, off:off + 3 * DN_W]
    off += 3 * DN_W
    gate_out[0] = p[:, off:off + DN_W]


def in_projection(xs, mod, norm_w, w_main, w_gate, cos, sina, sinb, qn_w, kn_w):
    bsz, s, d = xs.shape
    tm = TOKEN_BLOCK
    tok = lambda b, j: (b, j, 0)
    head_tok = lambda b, j: (b, 0, j, 0)
    full2 = lambda b, j: (0, 0)
    rope_spec = pl.BlockSpec((tm, HEAD_DIM), lambda b, j: (j, 0))
    return pl.pallas_call(
        _inproj_kernel,
        out_shape=(jax.ShapeDtypeStruct((bsz, ATTN_HEADS, s, HEAD_DIM), BF16),
                   jax.ShapeDtypeStruct((bsz, ATTN_KV_HEADS, s, HEAD_DIM), BF16),
                   jax.ShapeDtypeStruct((bsz, ATTN_KV_HEADS, s, HEAD_DIM), BF16),
                   jax.ShapeDtypeStruct((bsz, s, 3 * DN_W), F32),
                   jax.ShapeDtypeStruct((bsz, s, DN_W), F32),
                   jax.ShapeDtypeStruct((bsz, s, LANES), F32)),
        grid=(bsz, s // tm),
        in_specs=[pl.BlockSpec((1, tm, d), tok),
                  pl.BlockSpec((1, 1, 1, mod.shape[-1]), _mod_index),
                  pl.BlockSpec((1, d), full2),
                  pl.BlockSpec((d, IN_MAIN_W), full2),
                  pl.BlockSpec((d, LANES), full2),
                  rope_spec, rope_spec, rope_spec,
                  pl.BlockSpec((1, HEAD_DIM), full2),
                  pl.BlockSpec((1, HEAD_DIM), full2)],
        out_specs=(pl.BlockSpec((1, ATTN_HEADS, tm, HEAD_DIM), head_tok),
                   pl.BlockSpec((1, ATTN_KV_HEADS, tm, HEAD_DIM), head_tok),
                   pl.BlockSpec((1, ATTN_KV_HEADS, tm, HEAD_DIM), head_tok),
                   pl.BlockSpec((1, tm, 3 * DN_W), tok),
                   pl.BlockSpec((1, tm, DN_W), tok),
                   pl.BlockSpec((1, tm, LANES), tok)),
        compiler_params=_params(("parallel", "parallel")),
        name="in_projection",
    )(xs, mod, norm_w.reshape(1, d), w_main, w_gate, cos, sina, sinb,
      qn_w.reshape(1, HEAD_DIM), kn_w.reshape(1, HEAD_DIM))


def _attn_kernel(q_ref, k_ref, v_ref, o_ref, m_sc, l_sc, acc_sc, *, ctx_len):
    qi = pl.program_id(2)
    ki = pl.program_id(3)
    nk = pl.num_programs(3)
    tq = q_ref.shape[2]

    @pl.when(ki == 0)
    def _():
        m_sc[...] = jnp.full(m_sc.shape, NEG_BIG, F32)
        l_sc[...] = jnp.zeros(l_sc.shape, F32)
        acc_sc[...] = jnp.zeros(acc_sc.shape, F32)

    def step(ctx_only):
        q = q_ref[0].reshape(ATTN_GROUP * tq, HEAD_DIM)
        s = lax.dot_general(q, k_ref[0, 0], (((1,), (1,)), ((), ())), preferred_element_type=F32)
        if ctx_only:
            col = lax.broadcasted_iota(jnp.int32, s.shape, 1)
            s = jnp.where(col < ctx_len, s, NEG_BIG)
        m_prev = m_sc[...]
        m_new = jnp.maximum(m_prev, jnp.max(s, axis=-1, keepdims=True))
        p = jnp.exp(s - m_new)
        alpha = jnp.exp(m_prev - m_new)
        l_sc[...] = alpha * l_sc[...] + jnp.sum(p, axis=-1, keepdims=True)
        acc_sc[...] = alpha * acc_sc[...] + jnp.dot(p.astype(BF16), v_ref[0, 0], preferred_element_type=F32)
        m_sc[...] = m_new

    pl.when(qi > 0)(lambda: step(False))
    pl.when((qi == 0) & (ki == 0))(lambda: step(True))

    @pl.when(ki == nk - 1)
    def _():
        o = acc_sc[...] / l_sc[...]
        for g in range(ATTN_GROUP):
            o_ref[0, :, g * HEAD_DIM:(g + 1) * HEAD_DIM] = o[g * tq:(g + 1) * tq].astype(o_ref.dtype)


def attention(q, k, v, ctx_len):
    bsz, _, s, _ = q.shape
    tq = TOKEN_BLOCK
    tk = ATTN_KEY_BLOCK if s % ATTN_KEY_BLOCK == 0 else TOKEN_BLOCK
    assert ctx_len == tq and ctx_len <= tk
    return pl.pallas_call(
        functools.partial(_attn_kernel, ctx_len=ctx_len),
        out_shape=jax.ShapeDtypeStruct((bsz, s, ATTN_Q_W), BF16),
        grid=(bsz, ATTN_KV_HEADS, s // tq, s // tk),
        in_specs=[pl.BlockSpec((1, ATTN_GROUP, tq, HEAD_DIM), lambda b, h, i, j: (b, h, i, 0)),
                  pl.BlockSpec((1, 1, tk, HEAD_DIM), lambda b, h, i, j: (b, h, j, 0)),
                  pl.BlockSpec((1, 1, tk, HEAD_DIM), lambda b, h, i, j: (b, h, j, 0))],
        out_specs=pl.BlockSpec((1, tq, ATTN_GROUP * HEAD_DIM), lambda b, h, i, j: (b, i, h)),
        scratch_shapes=[pltpu.VMEM((ATTN_GROUP * tq, 1), F32),
                        pltpu.VMEM((ATTN_GROUP * tq, 1), F32),
                        pltpu.VMEM((ATTN_GROUP * tq, HEAD_DIM), F32)],
        compiler_params=_params(("parallel", "parallel", "parallel", "arbitrary")),
        name="attention",
    )(q, k, v)


def _dn_prep_kernel(main_ref, prev_ref, next_ref, ba_ref, cw_ref, gp_ref,
                    w_out, u_out, qg_out, kdt_out, qk_out, dl_out, ext_sc, *, ctx_chunks):
    j = pl.program_id(1)
    nj = pl.num_programs(1)
    c = DN_CHUNK
    has_prev = (j != 0) & (j != ctx_chunks)
    has_next = (j != ctx_chunks - 1) & (j != nj - 1)
    ext_sc[0:SUBLANES] = jnp.where(has_prev, prev_ref[0], 0.0)
    ext_sc[SUBLANES:SUBLANES + c] = main_ref[0]
    ext_sc[SUBLANES + c:2 * SUBLANES + c] = jnp.where(has_next, next_ref[0], 0.0)
    y = ext_sc[SUBLANES - CONV_PAD:SUBLANES - CONV_PAD + c] * cw_ref[0:1]
    for t in range(1, CONV_K):
        y = y + ext_sc[SUBLANES - CONV_PAD + t:SUBLANES - CONV_PAD + t + c] * cw_ref[t:t + 1]
    y = y * jax.nn.sigmoid(y)

    ba = ba_ref[0]
    beta_all = jax.nn.sigmoid(ba)
    g_all = -jnp.exp(gp_ref[0:1]) * jax.nn.softplus(ba + gp_ref[1:2])
    row = lax.broadcasted_iota(jnp.int32, (c, c), 0)
    col = lax.broadcasted_iota(jnp.int32, (c, c), 1)
    lower = (row >= col).astype(F32)
    upper = (row <= col).astype(F32)
    gc = (jnp.dot(lower, g_all, preferred_element_type=F32, precision=HIGHEST),
          jnp.dot(upper, g_all, preferred_element_type=F32, precision=HIGHEST))
    gct = (gc[0].T, gc[1].T)
    total = gc[0][c - 1:c]
    row2 = lax.broadcasted_iota(jnp.int32, (2 * c, 2 * c), 0)
    col2 = lax.broadcasted_iota(jnp.int32, (2 * c, 2 * c), 1)
    eye2 = (row2 == col2).astype(F32)
    zero = jnp.zeros((c, c), F32)

    for hd in range(DN_HEADS):
        q = y[:, hd * HEAD_DIM:(hd + 1) * HEAD_DIM]
        k = y[:, DN_W + hd * HEAD_DIM:DN_W + (hd + 1) * HEAD_DIM]
        v = y[:, 2 * DN_W + hd * HEAD_DIM:2 * DN_W + (hd + 1) * HEAD_DIM]
        q = q * lax.rsqrt(jnp.sum(q * q, axis=-1, keepdims=True) + EPS) * (HEAD_DIM ** -0.5)
        k = k * lax.rsqrt(jnp.sum(k * k, axis=-1, keepdims=True) + EPS)
        kk = _bf16_dot_nt(k, k)
        qk = _bf16_dot_nt(q, k)
        a_dir, rhs_dir = [], []
        for dr in range(2):
            cb = dr * DN_HEADS + hd
            cg = 2 * DN_HEADS + cb
            beta = beta_all[:, cb:cb + 1]
            gcol = gc[dr][:, cg:cg + 1]
            grow = gct[dr][cg:cg + 1, :]
            incl = (row >= col) if dr == 0 else (row <= col)
            strict = (row > col) if dr == 0 else (row < col)
            decay = jnp.exp(jnp.where(incl, gcol - grow, NEG_BIG))
            a_dir.append(jnp.where(strict, kk * beta * decay, 0.0))
            eg = jnp.exp(gcol)
            rhs_dir.append(jnp.concatenate([v * beta, k * (beta * eg)], axis=1))
            qg_out[0, dr, hd] = (q * eg).astype(BF16)
            tot = total[:, cg:cg + 1]
            kdt_out[0, dr, hd] = (k * jnp.exp(tot - gcol)).T.astype(BF16)
            qk_out[0, dr, hd] = (qk * decay).astype(BF16)
            dl_out[0, dr, hd, 0] = jnp.broadcast_to(jnp.exp(tot), (1, LANES))
        a2 = jnp.concatenate([jnp.concatenate([a_dir[0], zero], axis=1),
                              jnp.concatenate([zero, a_dir[1]], axis=1)], axis=0)
        base = SUBLANES
        a_base = jnp.where((row2 // base) == (col2 // base), a2, 0.0)
        x = eye2 - a_base
        pw = _split_dot(a_base, a_base)
        for it in range(int(math.log2(base)) - 1):
            x = x + _split_dot(x, pw)
            if it < int(math.log2(base)) - 2:
                pw = _split_dot(pw, pw)
        blk = base
        while blk < c:
            sibling = ((row2 // (2 * blk)) == (col2 // (2 * blk))) & ((row2 // blk) != (col2 // blk))
            x = x - _bf16_dot(_bf16_dot(x, jnp.where(sibling, a2, 0.0)), x)
            blk *= 2
        uw = _bf16_dot(x, jnp.concatenate(rhs_dir, axis=0))
        for dr in range(2):
            u_out[0, dr, hd] = uw[dr * c:(dr + 1) * c, 0:HEAD_DIM]
            w_out[0, dr, hd] = uw[dr * c:(dr + 1) * c, HEAD_DIM:2 * HEAD_DIM].astype(BF16)


def dn_prepare(dnqkv, ba, conv_w, gate_par, ctx_len):
    bsz, s, wdt = dnqkv.shape
    c = DN_CHUNK
    nc = s // c
    rows8 = s // SUBLANES
    per = c // SUBLANES
    chain = lambda b, j: (b, 0, 0, j, 0)
    return pl.pallas_call(
        functools.partial(_dn_prep_kernel, ctx_chunks=ctx_len // c),
        out_shape=(jax.ShapeDtypeStruct((bsz, 2, DN_HEADS, s, HEAD_DIM), BF16),
                   jax.ShapeDtypeStruct((bsz, 2, DN_HEADS, s, HEAD_DIM), F32),
                   jax.ShapeDtypeStruct((bsz, 2, DN_HEADS, s, HEAD_DIM), BF16),
                   jax.ShapeDtypeStruct((bsz, 2, DN_HEADS, HEAD_DIM, s), BF16),
                   jax.ShapeDtypeStruct((bsz, 2, DN_HEADS, s, c), BF16),
                   jax.ShapeDtypeStruct((bsz, 2, DN_HEADS, nc, 1, LANES), F32)),
        grid=(bsz, nc),
        in_specs=[pl.BlockSpec((1, c, wdt), lambda b, j: (b, j, 0)),
                  pl.BlockSpec((1, SUBLANES, wdt), lambda b, j: (b, jnp.maximum(j * per - 1, 0), 0)),
                  pl.BlockSpec((1, SUBLANES, wdt), lambda b, j: (b, jnp.minimum((j + 1) * per, rows8 - 1), 0)),
                  pl.BlockSpec((1, c, LANES), lambda b, j: (b, j, 0)),
                  pl.BlockSpec((SUBLANES, wdt), lambda b, j: (0, 0)),
                  pl.BlockSpec((SUBLANES, LANES), lambda b, j: (0, 0))],
        out_specs=(pl.BlockSpec((1, 2, DN_HEADS, c, HEAD_DIM), chain),
                   pl.BlockSpec((1, 2, DN_HEADS, c, HEAD_DIM), chain),
                   pl.BlockSpec((1, 2, DN_HEADS, c, HEAD_DIM), chain),
                   pl.BlockSpec((1, 2, DN_HEADS, HEAD_DIM, c), lambda b, j: (b, 0, 0, 0, j)),
                   pl.BlockSpec((1, 2, DN_HEADS, c, c), chain),
                   pl.BlockSpec((1, 2, DN_HEADS, 1, 1, LANES), lambda b, j: (b, 0, 0, j, 0, 0))),
        scratch_shapes=[pltpu.VMEM((c + 2 * SUBLANES, wdt), F32)],
        compiler_params=_params(("parallel", "parallel")),
        name="dn_prepare",
    )(dnqkv, dnqkv, dnqkv, ba, conv_w, gate_par)


def _dn_scan_kernel(*refs, bsz):
    ins = refs[:12]
    of_ref, ob_ref, s_sc = refs[12:]
    n = pl.program_id(0)

    @pl.when(n == 0)
    def _():
        s_sc[...] = jnp.zeros(s_sc.shape, F32)

    for dr in range(2):
        w_ref, u_ref, qg_ref, kdt_ref, qk_ref, dl_ref = ins[dr * 6:(dr + 1) * 6]
        o_ref = of_ref if dr == 0 else ob_ref
        for b in range(bsz):
            for hd in range(DN_HEADS):
                ci = (dr * bsz + b) * DN_HEADS + hd
                st = s_sc[ci]
                lhs = jnp.concatenate([w_ref[b, 0, hd], qg_ref[b, 0, hd]], axis=0)
                r = jnp.dot(lhs, st.astype(BF16), preferred_element_type=F32)
                v_new = (u_ref[b, 0, hd] - r[0:DN_CHUNK]).astype(BF16)
                o = r[DN_CHUNK:] + jnp.dot(qk_ref[b, 0, hd], v_new, preferred_element_type=F32)
                o_ref[b, :, hd * HEAD_DIM:(hd + 1) * HEAD_DIM] = o
                s_sc[ci] = st * dl_ref[b, 0, hd, 0] + jnp.dot(kdt_ref[b, 0, hd], v_new,
                                                              preferred_element_type=F32)


def dn_scan(w, u, qg, kdt, qk, dl, ctx_len):
    bsz, _, _, s, _ = w.shape
    c = DN_CHUNK
    nc = s // c
    cc = ctx_len // c

    def bwd_chunk(n):
        return jnp.where(n < cc, cc - 1 - n, nc - 1 - (n - cc))

    in_specs, args = [], []
    for dr in range(2):
        pos = (lambda n: n) if dr == 0 else bwd_chunk
        tokm = lambda n, dr=dr, pos=pos: (0, dr, 0, pos(n), 0)
        for arr in (w, u, qg):
            in_specs.append(pl.BlockSpec((bsz, 1, DN_HEADS, c, HEAD_DIM), tokm))
            args.append(arr)
        in_specs.append(pl.BlockSpec((bsz, 1, DN_HEADS, HEAD_DIM, c), lambda n, dr=dr, pos=pos: (0, dr, 0, 0, pos(n))))
        args.append(kdt)
        in_specs.append(pl.BlockSpec((bsz, 1, DN_HEADS, c, c), tokm))
        args.append(qk)
        in_specs.append(pl.BlockSpec((bsz, 1, DN_HEADS, 1, 1, LANES), lambda n, dr=dr, pos=pos: (0, dr, 0, pos(n), 0, 0)))
        args.append(dl)
    return pl.pallas_call(
        functools.partial(_dn_scan_kernel, bsz=bsz),
        out_shape=(jax.ShapeDtypeStruct((bsz, s, DN_W), F32), jax.ShapeDtypeStruct((bsz, s, DN_W), F32)),
        grid=(nc,),
        in_specs=in_specs,
        out_specs=(pl.BlockSpec((bsz, c, DN_W), lambda n: (0, n, 0)),
                   pl.BlockSpec((bsz, c, DN_W), lambda n: (0, bwd_chunk(n), 0))),
        scratch_shapes=[pltpu.VMEM((2 * bsz * DN_HEADS, HEAD_DIM, HEAD_DIM), F32)],
        compiler_params=_params(("arbitrary",)),
        name="dn_scan",
    )(*args)


def _outproj_kernel(x_ref, mod_ref, attn_ref, of_ref, ob_ref, gate_ref, dnw_ref, wo_ref, n2w_ref,
                    x_out, h2_out):
    d = x_ref.shape[-1]
    o = of_ref[0] + ob_ref[0]
    gate = gate_ref[0]
    parts = [attn_ref[0]]
    for hd in range(DN_HEADS):
        sl = slice(hd * HEAD_DIM, (hd + 1) * HEAD_DIM)
        g = gate[:, sl]
        parts.append((_head_rmsnorm(o[:, sl], dnw_ref[...]) * (g * jax.nn.sigmoid(g))).astype(BF16))
    mix = jnp.concatenate(parts, axis=1)
    y = jnp.dot(mix, wo_ref[...], preferred_element_type=F32)
    x = x_ref[0] + mod_ref[0, 0, :, 2 * d:3 * d] * y
    x_out[0] = x
    h2_out[0] = _modulated_norm(x, n2w_ref[...], mod_ref[0, 0, :, 3 * d:4 * d],
                                mod_ref[0, 0, :, 4 * d:5 * d]).astype(BF16)


def out_projection(xs, mod, attn, o_f, o_b, gate, dn_norm_w, w_out, norm2_w):
    bsz, s, d = xs.shape
    tm = TOKEN_BLOCK
    tok = lambda b, j: (b, j, 0)
    full2 = lambda b, j: (0, 0)
    return pl.pallas_call(
        _outproj_kernel,
        out_shape=(jax.ShapeDtypeStruct((bsz, s, d), F32), jax.ShapeDtypeStruct((bsz, s, d), BF16)),
        grid=(bsz, s // tm),
        in_specs=[pl.BlockSpec((1, tm, d), tok),
                  pl.BlockSpec((1, 1, 1, mod.shape[-1]), _mod_index),
                  pl.BlockSpec((1, tm, ATTN_Q_W), tok),
                  pl.BlockSpec((1, tm, DN_W), tok),
                  pl.BlockSpec((1, tm, DN_W), tok),
                  pl.BlockSpec((1, tm, DN_W), tok),
                  pl.BlockSpec((1, HEAD_DIM), full2),
                  pl.BlockSpec(w_out.shape, full2),
                  pl.BlockSpec((1, d), full2)],
        out_specs=(pl.BlockSpec((1, tm, d), tok), pl.BlockSpec((1, tm, d), tok)),
        compiler_params=_params(("parallel", "parallel")),
        name="out_projection",
    )(xs, mod, attn, o_f, o_b, gate, dn_norm_w.reshape(1, HEAD_DIM), w_out, norm2_w.reshape(1, d))


def _peer_score_kernel(h2_ref, wq_ref, sk_ref, st_out):
    q = jnp.dot(h2_ref[0], wq_ref[...], preferred_element_type=F32).astype(BF16)
    for hp in range(2 * PEER_HEADS):
        st_out[0, hp] = lax.dot_general(sk_ref[hp], q[:, hp * PEER_HALF:(hp + 1) * PEER_HALF],
                                        (((1,), (1,)), ((), ())), preferred_element_type=F32)


def peer_scores(h2, wq, subkeys):
    bsz, s, d = h2.shape
    tm = TOKEN_BLOCK
    nhp = 2 * PEER_HEADS
    return pl.pallas_call(
        _peer_score_kernel,
        out_shape=jax.ShapeDtypeStruct((bsz, nhp, N_KEYS, s), F32),
        grid=(bsz, s // tm),
        in_specs=[pl.BlockSpec((1, tm, d), lambda b, j: (b, j, 0)),
                  pl.BlockSpec(wq.shape, lambda b, j: (0, 0)),
                  pl.BlockSpec(subkeys.shape, lambda b, j: (0, 0, 0))],
        out_specs=pl.BlockSpec((1, nhp, N_KEYS, tm), lambda b, j: (b, 0, 0, j)),
        compiler_params=_params(("parallel", "parallel")),
        name="peer_scores",
    )(h2, wq, subkeys)


def _sorted_top(s, k):
    out = []
    for _ in range(k):
        m = jnp.max(s, axis=0, keepdims=True)
        out.append(m)
        s = jnp.where(s == m, NEG_BIG, s)
    return out


def _peer_topk_kernel(st_ref, thr_out, e0_out, e1_out):
    for hd in range(PEER_HEADS):
        s0 = st_ref[0, 2 * hd]
        s1 = st_ref[0, 2 * hd + 1]
        nt = PEER_TOPK + 1
        a = _sorted_top(s0, nt)
        b = _sorted_top(s1, nt)
        cand = [a[i] + b[j] for i in range(nt) for j in range(nt) if (i + 1) * (j + 1) <= nt]
        top = []
        for _ in range(nt):
            m = functools.reduce(jnp.maximum, cand)
            top.append(m)
            cand = [jnp.where(t == m, NEG_BIG, t) for t in cand]
        tau = 0.5 * (top[PEER_TOPK - 1] + top[PEER_TOPK])
        smax = a[0] + b[0]
        z = jnp.exp(top[0] - smax)
        for t in top[1:PEER_TOPK]:
            z = z + jnp.exp(t - smax)
        thr_out[0, hd] = tau - s0
        e0_out[0, hd] = jnp.exp(s0 - a[0])
        e1_out[0, hd] = jnp.exp(s1 - b[0]) / z


def peer_topk(st):
    bsz, nhp, nk, s = st.shape
    tl = LANES
    spec = pl.BlockSpec((1, PEER_HEADS, nk, tl), lambda b, j: (b, 0, 0, j))
    shp = jax.ShapeDtypeStruct((bsz, PEER_HEADS, nk, s), F32)
    return pl.pallas_call(
        _peer_topk_kernel,
        out_shape=(shp, shp, shp),
        grid=(bsz, s // tl),
        in_specs=[pl.BlockSpec((1, nhp, nk, tl), lambda b, j: (b, 0, 0, j))],
        out_specs=(spec, spec, spec),
        compiler_params=_params(("parallel", "parallel")),
        name="peer_topk",
    )(st)


def _peer_expert_kernel(x_ref, mod_ref, h2_ref, u_ref, vt_ref, st_ref, thr_ref, e0_ref, e1_ref,
                        x_out, acc_sc, g_sc, *, ctx_len):
    tok_block = pl.program_id(1)
    ec = pl.program_id(2)
    n_ec = pl.num_programs(2)
    d = x_ref.shape[-1]
    tb = x_ref.shape[1]

    @pl.when(ec == 0)
    def _():
        acc_sc[...] = jnp.zeros(acc_sc.shape, F32)

    at = lax.dot_general(u_ref[...], h2_ref[0], (((1,), (1,)), ((), ())), preferred_element_type=F32)
    for ii in range(PEER_I_PER_STEP):
        rows = slice(ii * N_KEYS, (ii + 1) * N_KEYS)
        wt = jnp.zeros((N_KEYS, tb), F32)
        for hd in range(PEER_HEADS):
            sel = jnp.where(st_ref[0, 2 * hd + 1] >= thr_ref[0, hd, ii:ii + 1, :], e1_ref[0, hd], 0.0)
            wt = wt + sel * e0_ref[0, hd, ii:ii + 1, :]
        a = at[rows]
        act = 0.5 * a * (1.0 + lax.erf(a * (2.0 ** -0.5)))
        g_sc[rows, :] = (act * wt).astype(BF16)
    acc_sc[...] += jnp.dot(vt_ref[...], g_sc[...], preferred_element_type=F32)

    @pl.when(ec == n_ec - 1)
    def _():
        tok = tok_block * tb + lax.broadcasted_iota(jnp.int32, (tb, 1), 0)
        g2 = jnp.where(tok < ctx_len, mod_ref[0, 0, :, 5 * d:6 * d], mod_ref[0, 1, :, 5 * d:6 * d])
        x_out[0] = x_ref[0] + g2 * acc_sc[...].T


def peer_experts(xs, mod, h2, u_tab, vt_tab, st, thr, e0, e1, ctx_len):
    bsz, s, d = xs.shape
    tb = PEER_TOKEN_BLOCK if s % PEER_TOKEN_BLOCK == 0 else TOKEN_BLOCK
    ech = PEER_I_PER_STEP * N_KEYS
    n_exp = u_tab.shape[0]
    tok = lambda b, j, e: (b, j, 0)
    per_tok = lambda b, j, e: (b, 0, 0, j)
    per_i = lambda b, j, e: (b, 0, e, j)
    return pl.pallas_call(
        functools.partial(_peer_expert_kernel, ctx_len=ctx_len),
        out_shape=jax.ShapeDtypeStruct((bsz, s, d), F32),
        grid=(bsz, s // tb, n_exp // ech),
        in_specs=[pl.BlockSpec((1, tb, d), tok),
                  pl.BlockSpec((1, 2, 1, mod.shape[-1]), lambda b, j, e: (b, 0, 0, 0)),
                  pl.BlockSpec((1, tb, d), tok),
                  pl.BlockSpec((ech, d), lambda b, j, e: (e, 0)),
                  pl.BlockSpec((d, ech), lambda b, j, e: (0, e)),
                  pl.BlockSpec((1, 2 * PEER_HEADS, N_KEYS, tb), per_tok),
                  pl.BlockSpec((1, PEER_HEADS, PEER_I_PER_STEP, tb), per_i),
                  pl.BlockSpec((1, PEER_HEADS, PEER_I_PER_STEP, tb), per_i),
                  pl.BlockSpec((1, PEER_HEADS, N_KEYS, tb), per_tok)],
        out_specs=pl.BlockSpec((1, tb, d), tok),
        scratch_shapes=[pltpu.VMEM((d, tb), F32), pltpu.VMEM((ech, tb), BF16)],
        compiler_params=_params(("parallel", "parallel", "arbitrary")),
        name="peer_experts",
    )(xs, mod, h2, u_tab, vt_tab, st, thr, e0, e1)


def _rope_tables(ctx_len, n_lat):
    rows = n_lat // GRID_W
    row = jnp.repeat(jnp.arange(rows, dtype=F32), GRID_W)
    col = jnp.tile(jnp.arange(GRID_W, dtype=F32), rows)
    axis_dim = HEAD_DIM // 2
    inv_freq = ROPE_THETA ** (-jnp.arange(0, axis_dim, 2, dtype=F32) / axis_dim)
    ang_r = row[:, None] * inv_freq[None, :]
    ang_c = col[:, None] * inv_freq[None, :]
    ang = jnp.concatenate([ang_r, ang_r, ang_c, ang_c], axis=-1)
    cos, sin = jnp.cos(ang), jnp.sin(ang)
    first = (jnp.arange(HEAD_DIM) % (HEAD_DIM // 2)) < (HEAD_DIM // 4)
    sina = jnp.where(first, -sin, 0.0)
    sinb = jnp.where(first, 0.0, sin)
    pad = lambda t, v: jnp.concatenate([jnp.full((ctx_len, HEAD_DIM), v, F32), t], axis=0)
    return pad(cos, 1.0), pad(sina, 0.0), pad(sinb, 0.0)


def kernel(x, c, ctx, c_ctx, ada_w, ada_b, norm1_w, norm2_w, w_in, attn_qnorm_w, attn_knorm_w, dn_conv_w,
           dn_A_log, dn_dt_bias, dn_norm_w, w_out, peer_wq, peer_subkeys, peer_u, peer_v):
    bsz, n_lat, d = x.shape
    ctx_len = ctx.shape[1]
    depth = ada_w.shape[0]
    assert ctx_len == TOKEN_BLOCK and n_lat % PEER_TOKEN_BLOCK == 0 and bsz + 1 <= SUBLANES
    assert w_in.shape[-1] == IN_MAIN_W + N_GATE_COLS

    xs = jnp.concatenate([ctx, x], axis=1)
    cos, sina, sinb = _rope_tables(ctx_len, n_lat)

    cc = jnp.zeros((SUBLANES, d), F32).at[:bsz].set(c).at[bsz].set(c_ctx)
    mod_all = ada_modulation(cc, ada_w, ada_b)
    mod_ctx = jnp.broadcast_to(mod_all[:, bsz][:, None], (depth, bsz, 6 * d))
    mod = jnp.stack([mod_ctx, mod_all[:, :bsz]], axis=2)[:, :, :, None, :]

    for l in range(depth):
        w_main = w_in[l, :, :IN_MAIN_W].astype(BF16)
        w_gate = jnp.pad(w_in[l, :, IN_MAIN_W:], ((0, 0), (0, LANES - N_GATE_COLS))).astype(BF16)
        q, k, v, dnqkv, gate, ba = in_projection(xs, mod[l], norm1_w[l], w_main, w_gate, cos, sina, sinb,
                                                 attn_qnorm_w[l], attn_knorm_w[l])
        attn = attention(q, k, v, ctx_len)
        conv_w = jnp.pad(dn_conv_w[l], ((0, SUBLANES - CONV_K), (0, 0)))
        gate_par = jnp.zeros((SUBLANES, LANES), F32)
        gate_par = gate_par.at[0, 2 * DN_HEADS:4 * DN_HEADS].set(dn_A_log[l].reshape(-1))
        gate_par = gate_par.at[1, 2 * DN_HEADS:4 * DN_HEADS].set(dn_dt_bias[l].reshape(-1))
        o_f, o_b = dn_scan(*dn_prepare(dnqkv, ba, conv_w, gate_par, ctx_len), ctx_len)
        xs, h2 = out_projection(xs, mod[l], attn, o_f, o_b, gate, dn_norm_w[l], w_out[l].astype(BF16),
                                norm2_w[l])
        sk = peer_subkeys[l].reshape(2 * PEER_HEADS, N_KEYS, PEER_HALF).astype(BF16)
        st = peer_scores(h2, peer_wq[l].astype(BF16), sk)
        thr, e0, e1 = peer_topk(st)
        xs = peer_experts(xs, mod[l], h2, peer_u[l].astype(BF16), peer_v[l].T.astype(BF16), st, thr, e0, e1,
                          ctx_len)
    return xs[:, ctx_len:]
```

```python
import functools
import math

import jax
import jax.numpy as jnp
from jax import lax
from jax.experimental import pallas as pl
from jax.experimental.pallas import tpu as pltpu

F32 = jnp.float32
BF16 = jnp.bfloat16
HIGHEST = lax.Precision.HIGHEST

HEAD_DIM = 128
ATTN_HEADS = 4
ATTN_KV_HEADS = 2
ATTN_GROUP = ATTN_HEADS // ATTN_KV_HEADS
DN_HEADS = 4
ATTN_Q_W = ATTN_HEADS * HEAD_DIM
ATTN_KV_W = ATTN_KV_HEADS * HEAD_DIM
DN_W = DN_HEADS * HEAD_DIM
IN_MAIN_W = ATTN_Q_W + 2 * ATTN_KV_W + 3 * DN_W + DN_W
N_GATE_COLS = 4 * DN_HEADS
ROPE_THETA = 10000.0
GRID_W = 64
CONV_K = 5
CONV_PAD = CONV_K // 2
PEER_HEADS = 8
PEER_HALF = 128
N_KEYS = 128
PEER_TOPK = 16
EPS = 1e-6
NEG_BIG = -1e30

LANES = 128
SUBLANES = 8
TOKEN_BLOCK = 256
DN_CHUNK = 128
ATTN_KEY_BLOCK = 1280
PEER_TOKEN_BLOCK = 640
PEER_I_PER_STEP = 8
VMEM_LIMIT = 56 * 1024 * 1024


def _params(sem):
    return pltpu.CompilerParams(dimension_semantics=sem, vmem_limit_bytes=VMEM_LIMIT)


def _bf16_dot(a, b):
    return jnp.dot(a.astype(BF16), b.astype(BF16), preferred_element_type=F32)


def _split_dot(a, b):
    a_hi = a.astype(BF16)
    b_hi = b.astype(BF16)
    a_lo = (a - a_hi.astype(F32)).astype(BF16)
    b_lo = (b - b_hi.astype(F32)).astype(BF16)
    dot = functools.partial(jnp.dot, preferred_element_type=F32)
    return dot(a_hi, b_hi) + (dot(a_hi, b_lo) + dot(a_lo, b_hi))


def _bf16_dot_nt(a, b):
    return lax.dot_general(a.astype(BF16), b.astype(BF16), (((1,), (1,)), ((), ())),
                           preferred_element_type=F32)


def _ada_kernel(c_ref, w_ref, b_ref, o_ref):
    c = c_ref[...]
    a = c * jax.nn.sigmoid(c)
    o_ref[0] = jnp.dot(a, w_ref[0], preferred_element_type=F32, precision=HIGHEST) + b_ref[0]


def ada_modulation(cc, ada_w, ada_b):
    depth, d, n = ada_w.shape
    tn = 1536
    return pl.pallas_call(
        _ada_kernel,
        out_shape=jax.ShapeDtypeStruct((depth, SUBLANES, n), F32),
        grid=(depth, n // tn),
        in_specs=[pl.BlockSpec((SUBLANES, d), lambda l, j: (0, 0)),
                  pl.BlockSpec((1, d, tn), lambda l, j: (l, 0, j)),
                  pl.BlockSpec((1, 1, tn), lambda l, j: (l, 0, j))],
        out_specs=pl.BlockSpec((1, SUBLANES, tn), lambda l, j: (l, 0, j)),
        compiler_params=_params(("parallel", "parallel")),
        name="ada_modulation",
    )(cc, ada_w, ada_b.reshape(depth, 1, n))


def _mod_index(b, j):
    return (b, jnp.minimum(j, 1), 0, 0)


def _modulated_norm(x, nw, shift, scale):
    ms = jnp.mean(x * x, axis=-1, keepdims=True)
    y = x * lax.rsqrt(ms + EPS) * nw
    return y * (1.0 + scale) + shift


def _head_rmsnorm(x, w):
    return x * lax.rsqrt(jnp.mean(x * x, axis=-1, keepdims=True) + EPS) * w


def _inproj_kernel(x_ref, mod_ref, nw_ref, wm_ref, wg_ref, cos_ref, sina_ref, sinb_ref, qnw_ref, knw_ref,
                   q_out, k_out, v_out, dn_out, gate_out, ba_out):
    d = x_ref.shape[-1]
    x = x_ref[0]
    h = _modulated_norm(x, nw_ref[...], mod_ref[0, 0, :, 0:d], mod_ref[0, 0, :, d:2 * d]).astype(BF16)
    p = jnp.dot(h, wm_ref[...], preferred_element_type=F32)
    ba_out[0] = jnp.dot(h, wg_ref[...], preferred_element_type=F32)
    cos = cos_ref[...]
    sina = sina_ref[...]
    sinb = sinb_ref[...]

    def rope(t):
        return (t * cos + pltpu.roll(t, HEAD_DIM - HEAD_DIM // 4, 1) * sina
                + pltpu.roll(t, HEAD_DIM // 4, 1) * sinb)

    scale = HEAD_DIM ** -0.5
    for hd in range(ATTN_HEADS):
        qh = _head_rmsnorm(p[:, hd * HEAD_DIM:(hd + 1) * HEAD_DIM], qnw_ref[...])
        q_out[0, hd] = (rope(qh) * scale).astype(BF16)
    off = ATTN_Q_W
    for hd in range(ATTN_KV_HEADS):
        kh = _head_rmsnorm(p[:, off + hd * HEAD_DIM:off + (hd + 1) * HEAD_DIM], knw_ref[...])
        k_out[0, hd] = rope(kh).astype(BF16)
    off += ATTN_KV_W
    for hd in range(ATTN_KV_HEADS):
        v_out[0, hd] = p[:, off + hd * HEAD_DIM:off + (hd + 1) * HEAD_DIM].astype(BF16)
    off += ATTN_KV_W
    dn_out[0] = p[:, off:off + 3 * DN_W]
    off += 3 * DN_W
    gate_out[0] = p[:, off:off + DN_W]


def in_projection(xs, mod, norm_w, w_main, w_gate, cos, sina, sinb, qn_w, kn_w):
    bsz, s, d = xs.shape
    tm = TOKEN_BLOCK
    tok = lambda b, j: (b, j, 0)
    head_tok = lambda b, j: (b, 0, j, 0)
    full2 = lambda b, j: (0, 0)
    rope_spec = pl.BlockSpec((tm, HEAD_DIM), lambda b, j: (j, 0))
    return pl.pallas_call(
        _inproj_kernel,
        out_shape=(jax.ShapeDtypeStruct((bsz, ATTN_HEADS, s, HEAD_DIM), BF16),
                   jax.ShapeDtypeStruct((bsz, ATTN_KV_HEADS, s, HEAD_DIM), BF16),
                   jax.ShapeDtypeStruct((bsz, ATTN_KV_HEADS, s, HEAD_DIM), BF16),
                   jax.ShapeDtypeStruct((bsz, s, 3 * DN_W), F32),
                   jax.ShapeDtypeStruct((bsz, s, DN_W), F32),
                   jax.ShapeDtypeStruct((bsz, s, LANES), F32)),
        grid=(bsz, s // tm),
        in_specs=[pl.BlockSpec((1, tm, d), tok),
                  pl.BlockSpec((1, 1, 1, mod.shape[-1]), _mod_index),
                  pl.BlockSpec((1, d), full2),
                  pl.BlockSpec((d, IN_MAIN_W), full2),
                  pl.BlockSpec((d, LANES), full2),
                  rope_spec, rope_spec, rope_spec,
                  pl.BlockSpec((1, HEAD_DIM), full2),
                  pl.BlockSpec((1, HEAD_DIM), full2)],
        out_specs=(pl.BlockSpec((1, ATTN_HEADS, tm, HEAD_DIM), head_tok),
                   pl.BlockSpec((1, ATTN_KV_HEADS, tm, HEAD_DIM), head_tok),
                   pl.BlockSpec((1, ATTN_KV_HEADS, tm, HEAD_DIM), head_tok),
                   pl.BlockSpec((1, tm, 3 * DN_W), tok),
                   pl.BlockSpec((1, tm, DN_W), tok),
                   pl.BlockSpec((1, tm, LANES), tok)),
        compiler_params=_params(("parallel", "parallel")),
        name="in_projection",
    )(xs, mod, norm_w.reshape(1, d), w_main, w_gate, cos, sina, sinb,
      qn_w.reshape(1, HEAD_DIM), kn_w.reshape(1, HEAD_DIM))


def _attn_kernel(q_ref, k_ref, v_ref, o_ref, m_sc, l_sc, acc_sc, *, ctx_len):
    qi = pl.program_id(2)
    ki = pl.program_id(3)
    nk = pl.num_programs(3)
    tq = q_ref.shape[2]

    @pl.when(ki == 0)
    def _():
        m_sc[...] = jnp.full(m_sc.shape, NEG_BIG, F32)
        l_sc[...] = jnp.zeros(l_sc.shape, F32)
        acc_sc[...] = jnp.zeros(acc_sc.shape, F32)

    def step(ctx_only):
        q = q_ref[0].reshape(ATTN_GROUP * tq, HEAD_DIM)
        s = lax.dot_general(q, k_ref[0, 0], (((1,), (1,)), ((), ())), preferred_element_type=F32)
        if ctx_only:
            col = lax.broadcasted_iota(jnp.int32, s.shape, 1)
            s = jnp.where(col < ctx_len, s, NEG_BIG)
        m_prev = m_sc[...]
        m_new = jnp.maximum(m_prev, jnp.max(s, axis=-1, keepdims=True))
        p = jnp.exp(s - m_new)
        alpha = jnp.exp(m_prev - m_new)
        l_sc[...] = alpha * l_sc[...] + jnp.sum(p, axis=-1, keepdims=True)
        acc_sc[...] = alpha * acc_sc[...] + jnp.dot(p.astype(BF16), v_ref[0, 0], preferred_element_type=F32)
        m_sc[...] = m_new

    pl.when(qi > 0)(lambda: step(False))
    pl.when((qi == 0) & (ki == 0))(lambda: step(True))

    @pl.when(ki == nk - 1)
    def _():
        o = acc_sc[...] / l_sc[...]
        for g in range(ATTN_GROUP):
            o_ref[0, :, g * HEAD_DIM:(g + 1) * HEAD_DIM] = o[g * tq:(g + 1) * tq].astype(o_ref.dtype)


def attention(q, k, v, ctx_len):
    bsz, _, s, _ = q.shape
    tq = TOKEN_BLOCK
    tk = ATTN_KEY_BLOCK if s % ATTN_KEY_BLOCK == 0 else TOKEN_BLOCK
    assert ctx_len == tq and ctx_len <= tk
    return pl.pallas_call(
        functools.partial(_attn_kernel, ctx_len=ctx_len),
        out_shape=jax.ShapeDtypeStruct((bsz, s, ATTN_Q_W), BF16),
        grid=(bsz, ATTN_KV_HEADS, s // tq, s // tk),
        in_specs=[pl.BlockSpec((1, ATTN_GROUP, tq, HEAD_DIM), lambda b, h, i, j: (b, h, i, 0)),
                  pl.BlockSpec((1, 1, tk, HEAD_DIM), lambda b, h, i, j: (b, h, j, 0)),
                  pl.BlockSpec((1, 1, tk, HEAD_DIM), lambda b, h, i, j: (b, h, j, 0))],
        out_specs=pl.BlockSpec((1, tq, ATTN_GROUP * HEAD_DIM), lambda b, h, i, j: (b, i, h)),
        scratch_shapes=[pltpu.VMEM((ATTN_GROUP * tq, 1), F32),
                        pltpu.VMEM((ATTN_GROUP * tq, 1), F32),
                        pltpu.VMEM((ATTN_GROUP * tq, HEAD_DIM), F32)],
        compiler_params=_params(("parallel", "parallel", "parallel", "arbitrary")),
        name="attention",
    )(q, k, v)


def _dn_prep_kernel(main_ref, prev_ref, next_ref, ba_ref, cw_ref, gp_ref,
                    w_out, u_out, qg_out, kdt_out, qk_out, dl_out, ext_sc, *, ctx_chunks):
    j = pl.program_id(1)
    nj = pl.num_programs(1)
    c = DN_CHUNK
    has_prev = (j != 0) & (j != ctx_chunks)
    has_next = (j != ctx_chunks - 1) & (j != nj - 1)
    ext_sc[0:SUBLANES] = jnp.where(has_prev, prev_ref[0], 0.0)
    ext_sc[SUBLANES:SUBLANES + c] = main_ref[0]
    ext_sc[SUBLANES + c:2 * SUBLANES + c] = jnp.where(has_next, next_ref[0], 0.0)
    y = ext_sc[SUBLANES - CONV_PAD:SUBLANES - CONV_PAD + c] * cw_ref[0:1]
    for t in range(1, CONV_K):
        y = y + ext_sc[SUBLANES - CONV_PAD + t:SUBLANES - CONV_PAD + t + c] * cw_ref[t:t + 1]
    y = y * jax.nn.sigmoid(y)

    ba = ba_ref[0]
    beta_all = jax.nn.sigmoid(ba)
    g_all = -jnp.exp(gp_ref[0:1]) * jax.nn.softplus(ba + gp_ref[1:2])
    row = lax.broadcasted_iota(jnp.int32, (c, c), 0)
    col = lax.broadcasted_iota(jnp.int32, (c, c), 1)
    lower = (row >= col).astype(F32)
    upper = (row <= col).astype(F32)
    gc = (jnp.dot(lower, g_all, preferred_element_type=F32, precision=HIGHEST),
          jnp.dot(upper, g_all, preferred_element_type=F32, precision=HIGHEST))
    gct = (gc[0].T, gc[1].T)
    total = gc[0][c - 1:c]
    row2 = lax.broadcasted_iota(jnp.int32, (2 * c, 2 * c), 0)
    col2 = lax.broadcasted_iota(jnp.int32, (2 * c, 2 * c), 1)
    eye2 = (row2 == col2).astype(F32)
    zero = jnp.zeros((c, c), F32)

    for hd in range(DN_HEADS):
        q = y[:, hd * HEAD_DIM:(hd + 1) * HEAD_DIM]
        k = y[:, DN_W + hd * HEAD_DIM:DN_W + (hd + 1) * HEAD_DIM]
        v = y[:, 2 * DN_W + hd * HEAD_DIM:2 * DN_W + (hd + 1) * HEAD_DIM]
        q = q * lax.rsqrt(jnp.sum(q * q, axis=-1, keepdims=True) + EPS) * (HEAD_DIM ** -0.5)
        k = k * lax.rsqrt(jnp.sum(k * k, axis=-1, keepdims=True) + EPS)
        kk = _bf16_dot_nt(k, k)
        qk = _bf16_dot_nt(q, k)
        a_dir, rhs_dir = [], []
        for dr in range(2):
            cb = dr * DN_HEADS + hd
            cg = 2 * DN_HEADS + cb
            beta = beta_all[:, cb:cb + 1]
            gcol = gc[dr][:, cg:cg + 1]
            grow = gct[dr][cg:cg + 1, :]
            incl = (row >= col) if dr == 0 else (row <= col)
            strict = (row > col) if dr == 0 else (row < col)
            decay = jnp.exp(jnp.where(incl, gcol - grow, NEG_BIG))
            a_dir.append(jnp.where(strict, kk * beta * decay, 0.0))
            eg = jnp.exp(gcol)
            rhs_dir.append(jnp.concatenate([v * beta, k * (beta * eg)], axis=1))
            qg_out[0, dr, hd] = (q * eg).astype(BF16)
            tot = total[:, cg:cg + 1]
            kdt_out[0, dr, hd] = (k * jnp.exp(tot - gcol)).T.astype(BF16)
            qk_out[0, dr, hd] = (qk * decay).astype(BF16)
            dl_out[0, dr, hd, 0] = jnp.broadcast_to(jnp.exp(tot), (1, LANES))
        a2 = jnp.concatenate([jnp.concatenate([a_dir[0], zero], axis=1),
                              jnp.concatenate([zero, a_dir[1]], axis=1)], axis=0)
        base = SUBLANES
        a_base = jnp.where((row2 // base) == (col2 // base), a2, 0.0)
        x = eye2 - a_base
        pw = _split_dot(a_base, a_base)
        for it in range(int(math.log2(base)) - 1):
            x = x + _split_dot(x, pw)
            if it < int(math.log2(base)) - 2:
                pw = _split_dot(pw, pw)
        blk = base
        while blk < c:
            sibling = ((row2 // (2 * blk)) == (col2 // (2 * blk))) & ((row2 // blk) != (col2 // blk))
            x = x - _bf16_dot(_bf16_dot(x, jnp.where(sibling, a2, 0.0)), x)
            blk *= 2
        uw = _bf16_dot(x, jnp.concatenate(rhs_dir, axis=0))
        for dr in range(2):
            u_out[0, dr, hd] = uw[dr * c:(dr + 1) * c, 0:HEAD_DIM]
            w_out[0, dr, hd] = uw[dr * c:(dr + 1) * c, HEAD_DIM:2 * HEAD_DIM].astype(BF16)


def dn_prepare(dnqkv, ba, conv_w, gate_par, ctx_len):
    bsz, s, wdt = dnqkv.shape
    c = DN_CHUNK
    nc = s // c
    rows8 = s // SUBLANES
    per = c // SUBLANES
    chain = lambda b, j: (b, 0, 0, j, 0)
    return pl.pallas_call(
        functools.partial(_dn_prep_kernel, ctx_chunks=ctx_len // c),
        out_shape=(jax.ShapeDtypeStruct((bsz, 2, DN_HEADS, s, HEAD_DIM), BF16),
                   jax.ShapeDtypeStruct((bsz, 2, DN_HEADS, s, HEAD_DIM), F32),
                   jax.ShapeDtypeStruct((bsz, 2, DN_HEADS, s, HEAD_DIM), BF16),
                   jax.ShapeDtypeStruct((bsz, 2, DN_HEADS, HEAD_DIM, s), BF16),
                   jax.ShapeDtypeStruct((bsz, 2, DN_HEADS, s, c), BF16),
                   jax.ShapeDtypeStruct((bsz, 2, DN_HEADS, nc, 1, LANES), F32)),
        grid=(bsz, nc),
        in_specs=[pl.BlockSpec((1, c, wdt), lambda b, j: (b, j, 0)),
                  pl.BlockSpec((1, SUBLANES, wdt), lambda b, j: (b, jnp.maximum(j * per - 1, 0), 0)),
                  pl.BlockSpec((1, SUBLANES, wdt), lambda b, j: (b, jnp.minimum((j + 1) * per, rows8 - 1), 0)),
                  pl.BlockSpec((1, c, LANES), lambda b, j: (b, j, 0)),
                  pl.BlockSpec((SUBLANES, wdt), lambda b, j: (0, 0)),
                  pl.BlockSpec((SUBLANES, LANES), lambda b, j: (0, 0))],
        out_specs=(pl.BlockSpec((1, 2, DN_HEADS, c, HEAD_DIM), chain),
                   pl.BlockSpec((1, 2, DN_HEADS, c, HEAD_DIM), chain),
                   pl.BlockSpec((1, 2, DN_HEADS, c, HEAD_DIM), chain),
                   pl.BlockSpec((1, 2, DN_HEADS, HEAD_DIM, c), lambda b, j: (b, 0, 0, 0, j)),
                   pl.BlockSpec((1, 2, DN_HEADS, c, c), chain),
                   pl.BlockSpec((1, 2, DN_HEADS, 1, 1, LANES), lambda b, j: (b, 0, 0, j, 0, 0))),
        scratch_shapes=[pltpu.VMEM((c + 2 * SUBLANES, wdt), F32)],
        compiler_params=_params(("parallel", "parallel")),
        name="dn_prepare",
    )(dnqkv, dnqkv, dnqkv, ba, conv_w, gate_par)


def _dn_scan_kernel(*refs, bsz):
    ins = refs[:12]
    of_ref, ob_ref, s_sc = refs[12:]
    n = pl.program_id(0)

    @pl.when(n == 0)
    def _():
        s_sc[...] = jnp.zeros(s_sc.shape, F32)

    for dr in range(2):
        w_ref, u_ref, qg_ref, kdt_ref, qk_ref, dl_ref = ins[dr * 6:(dr + 1) * 6]
        o_ref = of_ref if dr == 0 else ob_ref
        for b in range(bsz):
            for hd in range(DN_HEADS):
                ci = (dr * bsz + b) * DN_HEADS + hd
                st = s_sc[ci]
                lhs = jnp.concatenate([w_ref[b, 0, hd], qg_ref[b, 0, hd]], axis=0)
                r = jnp.dot(lhs, st.astype(BF16), preferred_element_type=F32)
                v_new = (u_ref[b, 0, hd] - r[0:DN_CHUNK]).astype(BF16)
                o = r[DN_CHUNK:] + jnp.dot(qk_ref[b, 0, hd], v_new, preferred_element_type=F32)
                o_ref[b, :, hd * HEAD_DIM:(hd + 1) * HEAD_DIM] = o
                s_sc[ci] = st * dl_ref[b, 0, hd, 0] + jnp.dot(kdt_ref[b, 0, hd], v_new,
                                                              preferred_element_type=F32)


def dn_scan(w, u, qg, kdt, qk, dl, ctx_len):
    bsz, _, _, s, _ = w.shape
    c = DN_CHUNK
    nc = s // c
    cc = ctx_len // c

    def bwd_chunk(n):
        return jnp.where(n < cc, cc - 1 - n, nc - 1 - (n - cc))

    in_specs, args = [], []
    for dr in range(2):
        pos = (lambda n: n) if dr == 0 else bwd_chunk
        tokm = lambda n, dr=dr, pos=pos: (0, dr, 0, pos(n), 0)
        for arr in (w, u, qg):
            in_specs.append(pl.BlockSpec((bsz, 1, DN_HEADS, c, HEAD_DIM), tokm))
            args.append(arr)
        in_specs.append(pl.BlockSpec((bsz, 1, DN_HEADS, HEAD_DIM, c), lambda n, dr=dr, pos=pos: (0, dr, 0, 0, pos(n))))
        args.append(kdt)
        in_specs.append(pl.BlockSpec((bsz, 1, DN_HEADS, c, c), tokm))
        args.append(qk)
        in_specs.append(pl.BlockSpec((bsz, 1, DN_HEADS, 1, 1, LANES), lambda n, dr=dr, pos=pos: (0, dr, 0, pos(n), 0, 0)))
        args.append(dl)
    return pl.pallas_call(
        functools.partial(_dn_scan_kernel, bsz=bsz),
        out_shape=(jax.ShapeDtypeStruct((bsz, s, DN_W), F32), jax.ShapeDtypeStruct((bsz, s, DN_W), F32)),
        grid=(nc,),
        in_specs=in_specs,
        out_specs=(pl.BlockSpec((bsz, c, DN_W), lambda n: (0, n, 0)),
                   pl.BlockSpec((bsz, c, DN_W), lambda n: (0, bwd_chunk(n), 0))),
        scratch_shapes=[pltpu.VMEM((2 * bsz * DN_HEADS, HEAD_DIM, HEAD_DIM), F32)],
        compiler_params=_params(("arbitrary",)),
        name="dn_scan",
    )(*args)


def _outproj_kernel(x_ref, mod_ref, attn_ref, of_ref, ob_ref, gate_ref, dnw_ref, wo_ref, n2w_ref,
                    x_out, h2_out):
    d = x_ref.shape[-1]
    o = of_ref[0] + ob_ref[0]
    gate = gate_ref[0]
    parts = [attn_ref[0]]
    for hd in range(DN_HEADS):
        sl = slice(hd * HEAD_DIM, (hd + 1) * HEAD_DIM)
        g = gate[:, sl]
        parts.append((_head_rmsnorm(o[:, sl], dnw_ref[...]) * (g * jax.nn.sigmoid(g))).astype(BF16))
    mix = jnp.concatenate(parts, axis=1)
    y = jnp.dot(mix, wo_ref[...], preferred_element_type=F32)
    x = x_ref[0] + mod_ref[0, 0, :, 2 * d:3 * d] * y
    x_out[0] = x
    h2_out[0] = _modulated_norm(x, n2w_ref[...], mod_ref[0, 0, :, 3 * d:4 * d],
                                mod_ref[0, 0, :, 4 * d:5 * d]).astype(BF16)


def out_projection(xs, mod, attn, o_f, o_b, gate, dn_norm_w, w_out, norm2_w):
    bsz, s, d = xs.shape
    tm = TOKEN_BLOCK
    tok = lambda b, j: (b, j, 0)
    full2 = lambda b, j: (0, 0)
    return pl.pallas_call(
        _outproj_kernel,
        out_shape=(jax.ShapeDtypeStruct((bsz, s, d), F32), jax.ShapeDtypeStruct((bsz, s, d), BF16)),
        grid=(bsz, s // tm),
        in_specs=[pl.BlockSpec((1, tm, d), tok),
                  pl.BlockSpec((1, 1, 1, mod.shape[-1]), _mod_index),
                  pl.BlockSpec((1, tm, ATTN_Q_W), tok),
                  pl.BlockSpec((1, tm, DN_W), tok),
                  pl.BlockSpec((1, tm, DN_W), tok),
                  pl.BlockSpec((1, tm, DN_W), tok),
                  pl.BlockSpec((1, HEAD_DIM), full2),
                  pl.BlockSpec(w_out.shape, full2),
                  pl.BlockSpec((1, d), full2)],
        out_specs=(pl.BlockSpec((1, tm, d), tok), pl.BlockSpec((1, tm, d), tok)),
        compiler_params=_params(("parallel", "parallel")),
        name="out_projection",
    )(xs, mod, attn, o_f, o_b, gate, dn_norm_w.reshape(1, HEAD_DIM), w_out, norm2_w.reshape(1, d))


def _peer_score_kernel(h2_ref, wq_ref, sk_ref, st_out):
    q = jnp.dot(h2_ref[0], wq_ref[...], preferred_element_type=F32).astype(BF16)
    for hp in range(2 * PEER_HEADS):
        st_out[0, hp] = lax.dot_general(sk_ref[hp], q[:, hp * PEER_HALF:(hp + 1) * PEER_HALF],
                                        (((1,), (1,)), ((), ())), preferred_element_type=F32)


def peer_scores(h2, wq, subkeys):
    bsz, s, d = h2.shape
    tm = TOKEN_BLOCK
    nhp = 2 * PEER_HEADS
    return pl.pallas_call(
        _peer_score_kernel,
        out_shape=jax.ShapeDtypeStruct((bsz, nhp, N_KEYS, s), F32),
        grid=(bsz, s // tm),
        in_specs=[pl.BlockSpec((1, tm, d), lambda b, j: (b, j, 0)),
                  pl.BlockSpec(wq.shape, lambda b, j: (0, 0)),
                  pl.BlockSpec(subkeys.shape, lambda b, j: (0, 0, 0))],
        out_specs=pl.BlockSpec((1, nhp, N_KEYS, tm), lambda b, j: (b, 0, 0, j)),
        compiler_params=_params(("parallel", "parallel")),
        name="peer_scores",
    )(h2, wq, subkeys)


def _sorted_top(s, k, with_rank):
    out = []
    rank = jnp.full(s.shape, float(k), F32) if with_rank else None
    for r in range(k):
        m = jnp.max(s, axis=0, keepdims=True)
        out.append(m)
        hit = s == m
        if with_rank:
            rank = jnp.where(hit, float(r), rank)
        s = jnp.where(hit, NEG_BIG, s)
    return out, rank


def _paired_bf16_words(x):
    w = pltpu.bitcast(x.astype(BF16).astype(F32), jnp.uint32)
    return w | (w >> 16)


def _peer_topk_kernel(st_ref, cnt_out, e0_out, rank_out, e1_out):
    nt = PEER_TOPK + 1
    tops = ([], [])
    for hd in range(PEER_HEADS):
        a, _ = _sorted_top(st_ref[0, 2 * hd], nt, False)
        b, rank = _sorted_top(st_ref[0, 2 * hd + 1], nt, True)
        rank_out[0, hd] = rank.astype(BF16)
        tops[0].append(a)
        tops[1].append(b)
    a8 = [jnp.concatenate([tops[0][hd][r] for hd in range(PEER_HEADS)], axis=0) for r in range(nt)]
    b8 = [jnp.concatenate([tops[1][hd][r] for hd in range(PEER_HEADS)], axis=0) for r in range(nt)]
    cand = [a8[i] + b8[j] for i in range(nt) for j in range(nt) if (i + 1) * (j + 1) <= nt]
    top = []
    for _ in range(nt):
        m = functools.reduce(jnp.maximum, cand)
        top.append(m)
        cand = [jnp.where(t == m, NEG_BIG, t) for t in cand]
    tau8 = 0.5 * (top[PEER_TOPK - 1] + top[PEER_TOPK])
    smax = a8[0] + b8[0]
    z8 = jnp.exp(top[0] - smax)
    for t in top[1:PEER_TOPK]:
        z8 = z8 + jnp.exp(t - smax)
    rz8 = 1.0 / z8
    for hd in range(PEER_HEADS):
        s0 = st_ref[0, 2 * hd]
        s1 = st_ref[0, 2 * hd + 1]
        thr = tau8[hd:hd + 1] - s0
        cnt = jnp.zeros(s0.shape, F32)
        for r in range(nt):
            cnt = cnt + jnp.where(tops[1][hd][r] > thr, 1.0, 0.0)
        cnt_out[0, hd] = _paired_bf16_words(cnt)
        e0_out[0, hd] = _paired_bf16_words(jnp.exp(s0 - tops[0][hd][0]))
        e1_out[0, hd] = (jnp.exp(s1 - tops[1][hd][0]) * rz8[hd:hd + 1]).astype(BF16)


def peer_topk(st):
    bsz, nhp, nk, s = st.shape
    tl = LANES
    spec = pl.BlockSpec((1, PEER_HEADS, nk, tl), lambda b, j: (b, 0, 0, j))
    words = jax.ShapeDtypeStruct((bsz, PEER_HEADS, nk, s), jnp.uint32)
    halfs = jax.ShapeDtypeStruct((bsz, PEER_HEADS, nk, s), BF16)
    return pl.pallas_call(
        _peer_topk_kernel,
        out_shape=(words, words, halfs, halfs),
        grid=(bsz, s // tl),
        in_specs=[pl.BlockSpec((1, nhp, nk, tl), lambda b, j: (b, 0, 0, j))],
        out_specs=(spec, spec, spec, spec),
        compiler_params=_params(("parallel", "parallel")),
        name="peer_topk",
    )(st)


def _peer_expert_kernel(x_ref, mod_ref, h2_ref, u_ref, vt_ref, cnt_ref, e0_ref, rank_ref, e1_ref,
                        x_out, acc_sc, *, ctx_len):
    tok_block = pl.program_id(1)
    ec = pl.program_id(2)
    n_ec = pl.num_programs(2)
    d = x_ref.shape[-1]
    tb = x_ref.shape[1]
    pk = 2 * SUBLANES

    @pl.when(ec == 0)
    def _():
        acc_sc[...] = jnp.zeros(acc_sc.shape, F32)

    def row_tile(ref, hd, ii):
        return pltpu.bitcast(jnp.broadcast_to(ref[0, hd, ii:ii + 1, :], (SUBLANES, tb)), BF16)

    h2 = h2_ref[0]
    g = []
    for ii in range(PEER_I_PER_STEP):
        if ii % 2 == 0:
            at2 = lax.dot_general(u_ref[ii * N_KEYS:(ii + 2) * N_KEYS, :], h2, (((1,), (1,)), ((), ())),
                                  preferred_element_type=F32)
        at = at2[(ii % 2) * N_KEYS:(ii % 2 + 1) * N_KEYS]
        wt = [jnp.zeros((pk, tb), BF16) for _ in range(N_KEYS // pk)]
        for hd in range(PEER_HEADS):
            cnt = row_tile(cnt_ref, hd, ii)
            e0 = row_tile(e0_ref, hd, ii)
            for rt in range(N_KEYS // pk):
                rows = slice(rt * pk, (rt + 1) * pk)
                sel = jnp.where(rank_ref[0, hd, rows, :] < cnt, e1_ref[0, hd, rows, :], jnp.zeros((), BF16))
                wt[rt] = wt[rt] + sel * e0
        for rt in range(N_KEYS // pk):
            a = at[rt * pk:(rt + 1) * pk]
            act = 0.5 * a * (1.0 + lax.erf(a * (2.0 ** -0.5)))
            g.append(act.astype(BF16) * wt[rt])
    acc_sc[...] += jnp.dot(vt_ref[...], jnp.concatenate(g, axis=0), preferred_element_type=F32)

    @pl.when(ec == n_ec - 1)
    def _():
        tok = tok_block * tb + lax.broadcasted_iota(jnp.int32, (tb, 1), 0)
        g2 = jnp.where(tok < ctx_len, mod_ref[0, 0, :, 5 * d:6 * d], mod_ref[0, 1, :, 5 * d:6 * d])
        x_out[0] = x_ref[0] + g2 * acc_sc[...].T


def peer_experts(xs, mod, h2, u_tab, vt_tab, cnt, e0, rank, e1, ctx_len):
    bsz, s, d = xs.shape
    tb = PEER_TOKEN_BLOCK if s % PEER_TOKEN_BLOCK == 0 else TOKEN_BLOCK
    ech = PEER_I_PER_STEP * N_KEYS
    n_exp = u_tab.shape[0]
    tok = lambda b, j, e: (b, j, 0)
    per_tok = lambda b, j, e: (b, 0, 0, j)
    per_i = lambda b, j, e: (b, 0, e, j)
    return pl.pallas_call(
        functools.partial(_peer_expert_kernel, ctx_len=ctx_len),
        out_shape=jax.ShapeDtypeStruct((bsz, s, d), F32),
        grid=(bsz, s // tb, n_exp // ech),
        in_specs=[pl.BlockSpec((1, tb, d), tok),
                  pl.BlockSpec((1, 2, 1, mod.shape[-1]), lambda b, j, e: (b, 0, 0, 0)),
                  pl.BlockSpec((1, tb, d), tok),
                  pl.BlockSpec((ech, d), lambda b, j, e: (e, 0)),
                  pl.BlockSpec((d, ech), lambda b, j, e: (0, e)),
                  pl.BlockSpec((1, PEER_HEADS, PEER_I_PER_STEP, tb), per_i),
                  pl.BlockSpec((1, PEER_HEADS, PEER_I_PER_STEP, tb), per_i),
                  pl.BlockSpec((1, PEER_HEADS, N_KEYS, tb), per_tok),
                  pl.BlockSpec((1, PEER_HEADS, N_KEYS, tb), per_tok)],
        out_specs=pl.BlockSpec((1, tb, d), tok),
        scratch_shapes=[pltpu.VMEM((d, tb), F32)],
        compiler_params=_params(("parallel", "parallel", "arbitrary")),
        name="peer_experts",
    )(xs, mod, h2, u_tab, vt_tab, cnt, e0, rank, e1)


def _rope_tables(ctx_len, n_lat):
    rows = n_lat // GRID_W
    row = jnp.repeat(jnp.arange(rows, dtype=F32), GRID_W)
    col = jnp.tile(jnp.arange(GRID_W, dtype=F32), rows)
    axis_dim = HEAD_DIM // 2
    inv_freq = ROPE_THETA ** (-jnp.arange(0, axis_dim, 2, dtype=F32) / axis_dim)
    ang_r = row[:, None] * inv_freq[None, :]
    ang_c = col[:, None] * inv_freq[None, :]
    ang = jnp.concatenate([ang_r, ang_r, ang_c, ang_c], axis=-1)
    cos, sin = jnp.cos(ang), jnp.sin(ang)
    first = (jnp.arange(HEAD_DIM) % (HEAD_DIM // 2)) < (HEAD_DIM // 4)
    sina = jnp.where(first, -sin, 0.0)
    sinb = jnp.where(first, 0.0, sin)
    pad = lambda t, v: jnp.concatenate([jnp.full((ctx_len, HEAD_DIM), v, F32), t], axis=0)
    return pad(cos, 1.0), pad(sina, 0.0), pad(sinb, 0.0)


def kernel(x, c, ctx, c_ctx, ada_w, ada_b, norm1_w, norm2_w, w_in, attn_qnorm_w, attn_knorm_w, dn_conv_w,
           dn_A_log, dn_dt_bias, dn_norm_w, w_out, peer_wq, peer_subkeys, peer_u, peer_v):
    bsz, n_lat, d = x.shape
    ctx_len = ctx.shape[1]
    depth = ada_w.shape[0]
    assert ctx_len == TOKEN_BLOCK and n_lat % TOKEN_BLOCK == 0 and bsz + 1 <= SUBLANES
    assert w_in.shape[-1] == IN_MAIN_W + N_GATE_COLS

    xs = jnp.concatenate([ctx, x], axis=1)
    cos, sina, sinb = _rope_tables(ctx_len, n_lat)

    cc = jnp.zeros((SUBLANES, d), F32).at[:bsz].set(c).at[bsz].set(c_ctx)
    mod_all = ada_modulation(cc, ada_w, ada_b)
    mod_ctx = jnp.broadcast_to(mod_all[:, bsz][:, None], (depth, bsz, 6 * d))
    mod = jnp.stack([mod_ctx, mod_all[:, :bsz]], axis=2)[:, :, :, None, :]

    for l in range(depth):
        w_main = w_in[l, :, :IN_MAIN_W].astype(BF16)
        w_gate = jnp.pad(w_in[l, :, IN_MAIN_W:], ((0, 0), (0, LANES - N_GATE_COLS))).astype(BF16)
        q, k, v, dnqkv, gate, ba = in_projection(xs, mod[l], norm1_w[l], w_main, w_gate, cos, sina, sinb,
                                                 attn_qnorm_w[l], attn_knorm_w[l])
        attn = attention(q, k, v, ctx_len)
        conv_w = jnp.pad(dn_conv_w[l], ((0, SUBLANES - CONV_K), (0, 0)))
        gate_par = jnp.zeros((SUBLANES, LANES), F32)
        gate_par = gate_par.at[0, 2 * DN_HEADS:4 * DN_HEADS].set(dn_A_log[l].reshape(-1))
        gate_par = gate_par.at[1, 2 * DN_HEADS:4 * DN_HEADS].set(dn_dt_bias[l].reshape(-1))
        o_f, o_b = dn_scan(*dn_prepare(dnqkv, ba, conv_w, gate_par, ctx_len), ctx_len)
        xs, h2 = out_projection(xs, mod[l], attn, o_f, o_b, gate, dn_norm_w[l], w_out[l].astype(BF16),
                                norm2_w[l])
        sk = peer_subkeys[l].reshape(2 * PEER_HEADS, N_KEYS, PEER_HALF).astype(BF16)
        st = peer_scores(h2, peer_wq[l].astype(BF16), sk)
        cnt, e0, rank, e1 = peer_topk(st)
        xs = peer_experts(xs, mod[l], h2, peer_u[l].astype(BF16), peer_v[l].T.astype(BF16), cnt, e0, rank, e1,
                          ctx_len)
    return xs[:, ctx_len:]
```

```python
import functools
import math

import jax
import jax.numpy as jnp
from jax import lax
from jax.experimental import pallas as pl
from jax.experimental.pallas import tpu as pltpu

F32 = jnp.float32
BF16 = jnp.bfloat16
HIGHEST = lax.Precision.HIGHEST

HEAD_DIM = 128
ATTN_HEADS = 4
ATTN_KV_HEADS = 2
ATTN_GROUP = ATTN_HEADS // ATTN_KV_HEADS
DN_HEADS = 4
ATTN_Q_W = ATTN_HEADS * HEAD_DIM
ATTN_KV_W = ATTN_KV_HEADS * HEAD_DIM
DN_W = DN_HEADS * HEAD_DIM
IN_MAIN_W = ATTN_Q_W + 2 * ATTN_KV_W + 3 * DN_W + DN_W
N_GATE_COLS = 4 * DN_HEADS
ROPE_THETA = 10000.0
GRID_W = 64
CONV_K = 5
CONV_PAD = CONV_K // 2
PEER_HEADS = 8
PEER_HALF = 128
N_KEYS = 128
PEER_TOPK = 16
EPS = 1e-6
NEG_BIG = -1e30

LANES = 128
SUBLANES = 8
TOKEN_BLOCK = 256
DN_CHUNK = 128
ATTN_KEY_BLOCK = 1280
PEER_TOKEN_BLOCK = 640
PEER_I_PER_STEP = 8
VMEM_LIMIT = 56 * 1024 * 1024


def _params(sem):
    return pltpu.CompilerParams(dimension_semantics=sem, vmem_limit_bytes=VMEM_LIMIT)


def _bf16_dot(a, b):
    return jnp.dot(a.astype(BF16), b.astype(BF16), preferred_element_type=F32)


def _split_dot(a, b):
    a_hi = a.astype(BF16)
    b_hi = b.astype(BF16)
    a_lo = (a - a_hi.astype(F32)).astype(BF16)
    b_lo = (b - b_hi.astype(F32)).astype(BF16)
    dot = functools.partial(jnp.dot, preferred_element_type=F32)
    return dot(a_hi, b_hi) + (dot(a_hi, b_lo) + dot(a_lo, b_hi))


def _bf16_dot_nt(a, b):
    return lax.dot_general(a.astype(BF16), b.astype(BF16), (((1,), (1,)), ((), ())),
                           preferred_element_type=F32)


def _ada_kernel(c_ref, w_ref, b_ref, o_ref):
    c = c_ref[...]
    a = c * jax.nn.sigmoid(c)
    o_ref[0] = jnp.dot(a, w_ref[0], preferred_element_type=F32, precision=HIGHEST) + b_ref[0]


def ada_modulation(cc, ada_w, ada_b):
    depth, d, n = ada_w.shape
    tn = 1536
    return pl.pallas_call(
        _ada_kernel,
        out_shape=jax.ShapeDtypeStruct((depth, SUBLANES, n), F32),
        grid=(depth, n // tn),
        in_specs=[pl.BlockSpec((SUBLANES, d), lambda l, j: (0, 0)),
                  pl.BlockSpec((1, d, tn), lambda l, j: (l, 0, j)),
                  pl.BlockSpec((1, 1, tn), lambda l, j: (l, 0, j))],
        out_specs=pl.BlockSpec((1, SUBLANES, tn), lambda l, j: (l, 0, j)),
        compiler_params=_params(("parallel", "parallel")),
        name="ada_modulation",
    )(cc, ada_w, ada_b.reshape(depth, 1, n))


def _mod_index(b, j):
    return (b, jnp.minimum(j, 1), 0, 0)


def _modulated_norm(x, nw, shift, scale):
    ms = jnp.mean(x * x, axis=-1, keepdims=True)
    y = x * lax.rsqrt(ms + EPS) * nw
    return y * (1.0 + scale) + shift


def _head_rmsnorm(x, w):
    return x * lax.rsqrt(jnp.mean(x * x, axis=-1, keepdims=True) + EPS) * w


def _inproj_kernel(x_ref, mod_ref, nw_ref, wm_ref, wg_ref, cos_ref, sina_ref, sinb_ref, qnw_ref, knw_ref,
                   q_out, k_out, v_out, dn_out, gate_out, ba_out):
    d = x_ref.shape[-1]
    x = x_ref[0]
    h = _modulated_norm(x, nw_ref[...], mod_ref[0, 0, :, 0:d], mod_ref[0, 0, :, d:2 * d]).astype(BF16)
    p = jnp.dot(h, wm_ref[...], preferred_element_type=F32)
    ba_out[0] = jnp.dot(h, wg_ref[...], preferred_element_type=F32)
    cos = cos_ref[...]
    sina = sina_ref[...]
    sinb = sinb_ref[...]

    def rope(t):
        return (t * cos + pltpu.roll(t, HEAD_DIM - HEAD_DIM // 4, 1) * sina
                + pltpu.roll(t, HEAD_DIM // 4, 1) * sinb)

    scale = HEAD_DIM ** -0.5 * math.log2(math.e)
    for hd in range(ATTN_HEADS):
        qh = _head_rmsnorm(p[:, hd * HEAD_DIM:(hd + 1) * HEAD_DIM], qnw_ref[...])
        q_out[0, hd] = (rope(qh) * scale).astype(BF16)
    off = ATTN_Q_W
    for hd in range(ATTN_KV_HEADS):
        kh = _head_rmsnorm(p[:, off + hd * HEAD_DIM:off + (hd + 1) * HEAD_DIM], knw_ref[...])
        k_out[0, hd] = rope(kh).astype(BF16)
    off += ATTN_KV_W
    for hd in range(ATTN_KV_HEADS):
        v_out[0, hd] = p[:, off + hd * HEAD_DIM:off + (hd + 1) * HEAD_DIM].astype(BF16)
    off += ATTN_KV_W
    dn_out[0] = p[:, off:off + 3 * DN_W]
    off += 3 * DN_W
    gate_out[0] = p[:, off:off + DN_W]


def in_projection(xs, mod, norm_w, w_main, w_gate, cos, sina, sinb, qn_w, kn_w):
    bsz, s, d = xs.shape
    tm = TOKEN_BLOCK
    tok = lambda b, j: (b, j, 0)
    head_tok = lambda b, j: (b, 0, j, 0)
    full2 = lambda b, j: (0, 0)
    rope_spec = pl.BlockSpec((tm, HEAD_DIM), lambda b, j: (j, 0))
    return pl.pallas_call(
        _inproj_kernel,
        out_shape=(jax.ShapeDtypeStruct((bsz, ATTN_HEADS, s, HEAD_DIM), BF16),
                   jax.ShapeDtypeStruct((bsz, ATTN_KV_HEADS, s, HEAD_DIM), BF16),
                   jax.ShapeDtypeStruct((bsz, ATTN_KV_HEADS, s, HEAD_DIM), BF16),
                   jax.ShapeDtypeStruct((bsz, s, 3 * DN_W), F32),
                   jax.ShapeDtypeStruct((bsz, s, DN_W), F32),
                   jax.ShapeDtypeStruct((bsz, s, LANES), F32)),
        grid=(bsz, s // tm),
        in_specs=[pl.BlockSpec((1, tm, d), tok),
                  pl.BlockSpec((1, 1, 1, mod.shape[-1]), _mod_index),
                  pl.BlockSpec((1, d), full2),
                  pl.BlockSpec((d, IN_MAIN_W), full2),
                  pl.BlockSpec((d, LANES), full2),
                  rope_spec, rope_spec, rope_spec,
                  pl.BlockSpec((1, HEAD_DIM), full2),
                  pl.BlockSpec((1, HEAD_DIM), full2)],
        out_specs=(pl.BlockSpec((1, ATTN_HEADS, tm, HEAD_DIM), head_tok),
                   pl.BlockSpec((1, ATTN_KV_HEADS, tm, HEAD_DIM), head_tok),
                   pl.BlockSpec((1, ATTN_KV_HEADS, tm, HEAD_DIM), head_tok),
                   pl.BlockSpec((1, tm, 3 * DN_W), tok),
                   pl.BlockSpec((1, tm, DN_W), tok),
                   pl.BlockSpec((1, tm, LANES), tok)),
        compiler_params=_params(("parallel", "parallel")),
        name="in_projection",
    )(xs, mod, norm_w.reshape(1, d), w_main, w_gate, cos, sina, sinb,
      qn_w.reshape(1, HEAD_DIM), kn_w.reshape(1, HEAD_DIM))


def _attn_kernel(q_ref, k_ref, v_ref, o_ref, m_sc, l_sc, acc_sc):
    ki = pl.program_id(2)
    nk = pl.num_programs(2)

    @pl.when(ki == 0)
    def _():
        m_sc[...] = jnp.full(m_sc.shape, NEG_BIG, F32)
        l_sc[...] = jnp.zeros(l_sc.shape, F32)
        acc_sc[...] = jnp.zeros(acc_sc.shape, F32)

    scores = [lax.dot_general(q_ref[0, hd], k_ref[0, hd // ATTN_GROUP], (((1,), (1,)), ((), ())),
                              preferred_element_type=F32) for hd in range(ATTN_HEADS)]
    for hd in range(ATTN_HEADS):
        s = scores[hd]
        m_prev = m_sc[hd]
        m_new = jnp.maximum(m_prev, jnp.max(s, axis=-1, keepdims=True))
        p = jnp.exp2(s - m_new)
        alpha = jnp.exp2(m_prev - m_new)
        l_sc[hd] = alpha * l_sc[hd] + jnp.sum(p, axis=-1, keepdims=True)
        acc_sc[hd] = alpha * acc_sc[hd] + jnp.dot(p.astype(BF16), v_ref[0, hd // ATTN_GROUP],
                                                   preferred_element_type=F32)
        m_sc[hd] = m_new

    @pl.when(ki == nk - 1)
    def _():
        for hd in range(ATTN_HEADS):
            o_ref[0, :, hd * HEAD_DIM:(hd + 1) * HEAD_DIM] = (acc_sc[hd] / l_sc[hd]).astype(o_ref.dtype)


def _attention_call(q, k, v, q_block0, n_q_blocks, n_keys, tk, name):
    bsz = q.shape[0]
    tq = TOKEN_BLOCK
    return pl.pallas_call(
        _attn_kernel,
        out_shape=jax.ShapeDtypeStruct((bsz, n_q_blocks * tq, ATTN_Q_W), BF16),
        grid=(bsz, n_q_blocks, n_keys // tk),
        in_specs=[pl.BlockSpec((1, ATTN_HEADS, tq, HEAD_DIM), lambda b, i, j: (b, 0, i + q_block0, 0)),
                  pl.BlockSpec((1, ATTN_KV_HEADS, tk, HEAD_DIM), lambda b, i, j: (b, 0, j, 0)),
                  pl.BlockSpec((1, ATTN_KV_HEADS, tk, HEAD_DIM), lambda b, i, j: (b, 0, j, 0))],
        out_specs=pl.BlockSpec((1, tq, ATTN_Q_W), lambda b, i, j: (b, i, 0)),
        scratch_shapes=[pltpu.VMEM((ATTN_HEADS, tq, 1), F32),
                        pltpu.VMEM((ATTN_HEADS, tq, 1), F32),
                        pltpu.VMEM((ATTN_HEADS, tq, HEAD_DIM), F32)],
        compiler_params=_params(("parallel", "parallel", "arbitrary")),
        name=name,
    )(q, k, v)


def attention(q, k, v, ctx_len):
    s = q.shape[2]
    assert ctx_len == TOKEN_BLOCK
    tk = ATTN_KEY_BLOCK if s % ATTN_KEY_BLOCK == 0 else TOKEN_BLOCK
    attn_ctx = _attention_call(q, k, v, 0, 1, ctx_len, ctx_len, "attention_ctx")
    attn_lat = _attention_call(q, k, v, 1, s // TOKEN_BLOCK - 1, s, tk, "attention")
    return jnp.concatenate([attn_ctx, attn_lat], axis=1)


def _dn_prep_kernel(main_ref, prev_ref, next_ref, ba_ref, cw_ref, gp_ref,
                    w_out, u_out, qg_out, kdt_out, qk_out, dl_out, ext_sc, *, ctx_chunks):
    j = pl.program_id(1)
    nj = pl.num_programs(1)
    c = DN_CHUNK
    has_prev = (j != 0) & (j != ctx_chunks)
    has_next = (j != ctx_chunks - 1) & (j != nj - 1)
    ext_sc[0:SUBLANES] = jnp.where(has_prev, prev_ref[0], 0.0)
    ext_sc[SUBLANES:SUBLANES + c] = main_ref[0]
    ext_sc[SUBLANES + c:2 * SUBLANES + c] = jnp.where(has_next, next_ref[0], 0.0)
    y = ext_sc[SUBLANES - CONV_PAD:SUBLANES - CONV_PAD + c] * cw_ref[0:1]
    for t in range(1, CONV_K):
        y = y + ext_sc[SUBLANES - CONV_PAD + t:SUBLANES - CONV_PAD + t + c] * cw_ref[t:t + 1]
    y = y * jax.nn.sigmoid(y)

    ba = ba_ref[0]
    beta_all = jax.nn.sigmoid(ba)
    g_all = -jnp.exp(gp_ref[0:1]) * jax.nn.softplus(ba + gp_ref[1:2])
    row = lax.broadcasted_iota(jnp.int32, (c, c), 0)
    col = lax.broadcasted_iota(jnp.int32, (c, c), 1)
    lower = (row >= col).astype(BF16)
    g_hi = g_all.astype(BF16)
    g_r1 = g_all - g_hi.astype(F32)
    g_mid = g_r1.astype(BF16)
    g_lo = (g_r1 - g_mid.astype(F32)).astype(BF16)
    dotf = functools.partial(jnp.dot, preferred_element_type=F32)
    prefix = dotf(lower, g_hi) + (dotf(lower, g_mid) + dotf(lower, g_lo))
    total = prefix[c - 1:c]
    gc = (prefix, total - prefix + g_all)
    gct = (gc[0].T, gc[1].T)
    row2 = lax.broadcasted_iota(jnp.int32, (2 * c, 2 * c), 0)
    col2 = lax.broadcasted_iota(jnp.int32, (2 * c, 2 * c), 1)
    eye2 = (row2 == col2).astype(F32)
    zero = jnp.zeros((c, c), F32)

    heads = range(DN_HEADS)
    a2, rhs2 = [], []
    for hd in heads:
        q = y[:, hd * HEAD_DIM:(hd + 1) * HEAD_DIM]
        k = y[:, DN_W + hd * HEAD_DIM:DN_W + (hd + 1) * HEAD_DIM]
        v = y[:, 2 * DN_W + hd * HEAD_DIM:2 * DN_W + (hd + 1) * HEAD_DIM]
        q = q * lax.rsqrt(jnp.sum(q * q, axis=-1, keepdims=True) + EPS) * (HEAD_DIM ** -0.5)
        k = k * lax.rsqrt(jnp.sum(k * k, axis=-1, keepdims=True) + EPS)
        kk = _bf16_dot_nt(k, k)
        qk = _bf16_dot_nt(q, k)
        a_dir, rhs_dir = [], []
        for dr in range(2):
            cb = dr * DN_HEADS + hd
            cg = 2 * DN_HEADS + cb
            beta = beta_all[:, cb:cb + 1]
            gcol = gc[dr][:, cg:cg + 1]
            grow = gct[dr][cg:cg + 1, :]
            incl = (row >= col) if dr == 0 else (row <= col)
            strict = (row > col) if dr == 0 else (row < col)
            decay = jnp.exp(jnp.where(incl, gcol - grow, NEG_BIG))
            a_dir.append(jnp.where(strict, kk * beta * decay, 0.0))
            eg = jnp.exp(gcol)
            rhs_dir.append(jnp.concatenate([v * beta, k * (beta * eg)], axis=1))
            qg_out[0, dr, hd] = (q * eg).astype(BF16)
            tot = total[:, cg:cg + 1]
            kdt_out[0, dr, hd] = (k * jnp.exp(tot - gcol)).T.astype(BF16)
            qk_out[0, dr, hd] = (qk * decay).astype(BF16)
            dl_out[0, dr, hd, 0] = jnp.broadcast_to(jnp.exp(tot), (1, LANES))
        a2.append(jnp.concatenate([jnp.concatenate([a_dir[0], zero], axis=1),
                                   jnp.concatenate([zero, a_dir[1]], axis=1)], axis=0))
        rhs2.append(jnp.concatenate(rhs_dir, axis=0))
    base = SUBLANES
    base_mask = (row2 // base) == (col2 // base)
    a_base = [jnp.where(base_mask, a2[hd], 0.0) for hd in heads]
    x = [eye2 - a_base[hd] for hd in heads]
    pw = [_split_dot(a_base[hd], a_base[hd]) for hd in heads]
    for it in range(int(math.log2(base)) - 1):
        x = [x[hd] + _split_dot(x[hd], pw[hd]) for hd in heads]
        if it < int(math.log2(base)) - 2:
            pw = [_split_dot(pw[hd], pw[hd]) for hd in heads]
    blk = base
    while blk < c:
        sibling = ((row2 // (2 * blk)) == (col2 // (2 * blk))) & ((row2 // blk) != (col2 // blk))
        fold = [_bf16_dot(x[hd], jnp.where(sibling, a2[hd], 0.0)) for hd in heads]
        x = [x[hd] - _bf16_dot(fold[hd], x[hd]) for hd in heads]
        blk *= 2
    for hd in heads:
        uw = _bf16_dot(x[hd], rhs2[hd])
        for dr in range(2):
            u_out[0, dr, hd] = uw[dr * c:(dr + 1) * c, 0:HEAD_DIM]
            w_out[0, dr, hd] = uw[dr * c:(dr + 1) * c, HEAD_DIM:2 * HEAD_DIM].astype(BF16)


def dn_prepare(dnqkv, ba, conv_w, gate_par, ctx_len):
    bsz, s, wdt = dnqkv.shape
    c = DN_CHUNK
    nc = s // c
    rows8 = s // SUBLANES
    per = c // SUBLANES
    chain = lambda b, j: (b, 0, 0, j, 0)
    return pl.pallas_call(
        functools.partial(_dn_prep_kernel, ctx_chunks=ctx_len // c),
        out_shape=(jax.ShapeDtypeStruct((bsz, 2, DN_HEADS, s, HEAD_DIM), BF16),
                   jax.ShapeDtypeStruct((bsz, 2, DN_HEADS, s, HEAD_DIM), F32),
                   jax.ShapeDtypeStruct((bsz, 2, DN_HEADS, s, HEAD_DIM), BF16),
                   jax.ShapeDtypeStruct((bsz, 2, DN_HEADS, HEAD_DIM, s), BF16),
                   jax.ShapeDtypeStruct((bsz, 2, DN_HEADS, s, c), BF16),
                   jax.ShapeDtypeStruct((bsz, 2, DN_HEADS, nc, 1, LANES), F32)),
        grid=(bsz, nc),
        in_specs=[pl.BlockSpec((1, c, wdt), lambda b, j: (b, j, 0)),
                  pl.BlockSpec((1, SUBLANES, wdt), lambda b, j: (b, jnp.maximum(j * per - 1, 0), 0)),
                  pl.BlockSpec((1, SUBLANES, wdt), lambda b, j: (b, jnp.minimum((j + 1) * per, rows8 - 1), 0)),
                  pl.BlockSpec((1, c, LANES), lambda b, j: (b, j, 0)),
                  pl.BlockSpec((SUBLANES, wdt), lambda b, j: (0, 0)),
                  pl.BlockSpec((SUBLANES, LANES), lambda b, j: (0, 0))],
        out_specs=(pl.BlockSpec((1, 2, DN_HEADS, c, HEAD_DIM), chain),
                   pl.BlockSpec((1, 2, DN_HEADS, c, HEAD_DIM), chain),
                   pl.BlockSpec((1, 2, DN_HEADS, c, HEAD_DIM), chain),
                   pl.BlockSpec((1, 2, DN_HEADS, HEAD_DIM, c), lambda b, j: (b, 0, 0, 0, j)),
                   pl.BlockSpec((1, 2, DN_HEADS, c, c), chain),
                   pl.BlockSpec((1, 2, DN_HEADS, 1, 1, LANES), lambda b, j: (b, 0, 0, j, 0, 0))),
        scratch_shapes=[pltpu.VMEM((c + 2 * SUBLANES, wdt), F32)],
        compiler_params=_params(("parallel", "parallel")),
        name="dn_prepare",
    )(dnqkv, dnqkv, dnqkv, ba, conv_w, gate_par)


def _dn_scan_kernel(*refs, bsz):
    ins = refs[:12]
    of_ref, ob_ref, s_sc = refs[12:]
    n = pl.program_id(0)

    @pl.when(n == 0)
    def _():
        s_sc[...] = jnp.zeros(s_sc.shape, F32)

    for dr in range(2):
        w_ref, u_ref, qg_ref, kdt_ref, qk_ref, dl_ref = ins[dr * 6:(dr + 1) * 6]
        o_ref = of_ref if dr == 0 else ob_ref
        for b in range(bsz):
            for hd in range(DN_HEADS):
                ci = (dr * bsz + b) * DN_HEADS + hd
                st = s_sc[ci]
                lhs = jnp.concatenate([w_ref[b, 0, hd], qg_ref[b, 0, hd]], axis=0)
                r = jnp.dot(lhs, st.astype(BF16), preferred_element_type=F32)
                v_new = (u_ref[b, 0, hd] - r[0:DN_CHUNK]).astype(BF16)
                o = r[DN_CHUNK:] + jnp.dot(qk_ref[b, 0, hd], v_new, preferred_element_type=F32)
                o_ref[b, :, hd * HEAD_DIM:(hd + 1) * HEAD_DIM] = o
                s_sc[ci] = st * dl_ref[b, 0, hd, 0] + jnp.dot(kdt_ref[b, 0, hd], v_new,
                                                              preferred_element_type=F32)


def dn_scan(w, u, qg, kdt, qk, dl, ctx_len):
    bsz, _, _, s, _ = w.shape
    c = DN_CHUNK
    nc = s // c
    cc = ctx_len // c

    def bwd_chunk(n):
        return jnp.where(n < cc, cc - 1 - n, nc - 1 - (n - cc))

    in_specs, args = [], []
    for dr in range(2):
        pos = (lambda n: n) if dr == 0 else bwd_chunk
        tokm = lambda n, dr=dr, pos=pos: (0, dr, 0, pos(n), 0)
        for arr in (w, u, qg):
            in_specs.append(pl.BlockSpec((bsz, 1, DN_HEADS, c, HEAD_DIM), tokm))
            args.append(arr)
        in_specs.append(pl.BlockSpec((bsz, 1, DN_HEADS, HEAD_DIM, c), lambda n, dr=dr, pos=pos: (0, dr, 0, 0, pos(n))))
        args.append(kdt)
        in_specs.append(pl.BlockSpec((bsz, 1, DN_HEADS, c, c), tokm))
        args.append(qk)
        in_specs.append(pl.BlockSpec((bsz, 1, DN_HEADS, 1, 1, LANES), lambda n, dr=dr, pos=pos: (0, dr, 0, pos(n), 0, 0)))
        args.append(dl)
    return pl.pallas_call(
        functools.partial(_dn_scan_kernel, bsz=bsz),
        out_shape=(jax.ShapeDtypeStruct((bsz, s, DN_W), F32), jax.ShapeDtypeStruct((bsz, s, DN_W), F32)),
        grid=(nc,),
        in_specs=in_specs,
        out_specs=(pl.BlockSpec((bsz, c, DN_W), lambda n: (0, n, 0)),
                   pl.BlockSpec((bsz, c, DN_W), lambda n: (0, bwd_chunk(n), 0))),
        scratch_shapes=[pltpu.VMEM((2 * bsz * DN_HEADS, HEAD_DIM, HEAD_DIM), F32)],
        compiler_params=_params(("arbitrary",)),
        name="dn_scan",
    )(*args)


def _outproj_kernel(x_ref, mod_ref, attn_ref, of_ref, ob_ref, gate_ref, dnw_ref, wo_ref, n2w_ref,
                    x_out, h2_out):
    d = x_ref.shape[-1]
    o = of_ref[0] + ob_ref[0]
    gate = gate_ref[0]
    parts = [attn_ref[0]]
    for hd in range(DN_HEADS):
        sl = slice(hd * HEAD_DIM, (hd + 1) * HEAD_DIM)
        g = gate[:, sl]
        parts.append((_head_rmsnorm(o[:, sl], dnw_ref[...]) * (g * jax.nn.sigmoid(g))).astype(BF16))
    mix = jnp.concatenate(parts, axis=1)
    y = jnp.dot(mix, wo_ref[...], preferred_element_type=F32)
    x = x_ref[0] + mod_ref[0, 0, :, 2 * d:3 * d] * y
    x_out[0] = x
    h2_out[0] = _modulated_norm(x, n2w_ref[...], mod_ref[0, 0, :, 3 * d:4 * d],
                                mod_ref[0, 0, :, 4 * d:5 * d]).astype(BF16)


def out_projection(xs, mod, attn, o_f, o_b, gate, dn_norm_w, w_out, norm2_w):
    bsz, s, d = xs.shape
    tm = TOKEN_BLOCK
    tok = lambda b, j: (b, j, 0)
    full2 = lambda b, j: (0, 0)
    return pl.pallas_call(
        _outproj_kernel,
        out_shape=(jax.ShapeDtypeStruct((bsz, s, d), F32), jax.ShapeDtypeStruct((bsz, s, d), BF16)),
        grid=(bsz, s // tm),
        in_specs=[pl.BlockSpec((1, tm, d), tok),
                  pl.BlockSpec((1, 1, 1, mod.shape[-1]), _mod_index),
                  pl.BlockSpec((1, tm, ATTN_Q_W), tok),
                  pl.BlockSpec((1, tm, DN_W), tok),
                  pl.BlockSpec((1, tm, DN_W), tok),
                  pl.BlockSpec((1, tm, DN_W), tok),
                  pl.BlockSpec((1, HEAD_DIM), full2),
                  pl.BlockSpec(w_out.shape, full2),
                  pl.BlockSpec((1, d), full2)],
        out_specs=(pl.BlockSpec((1, tm, d), tok), pl.BlockSpec((1, tm, d), tok)),
        compiler_params=_params(("parallel", "parallel")),
        name="out_projection",
    )(xs, mod, attn, o_f, o_b, gate, dn_norm_w.reshape(1, HEAD_DIM), w_out, norm2_w.reshape(1, d))


def _peer_score_kernel(h2_ref, wq_ref, sk_ref, st_out):
    q = jnp.dot(h2_ref[0], wq_ref[...], preferred_element_type=F32).astype(BF16)
    for hp in range(2 * PEER_HEADS):
        st_out[0, hp] = lax.dot_general(sk_ref[hp], q[:, hp * PEER_HALF:(hp + 1) * PEER_HALF],
                                        (((1,), (1,)), ((), ())), preferred_element_type=F32)


def peer_scores(h2, wq, subkeys):
    bsz, s, d = h2.shape
    tm = TOKEN_BLOCK
    nhp = 2 * PEER_HEADS
    return pl.pallas_call(
        _peer_score_kernel,
        out_shape=jax.ShapeDtypeStruct((bsz, nhp, N_KEYS, s), F32),
        grid=(bsz, s // tm),
        in_specs=[pl.BlockSpec((1, tm, d), lambda b, j: (b, j, 0)),
                  pl.BlockSpec(wq.shape, lambda b, j: (0, 0)),
                  pl.BlockSpec(subkeys.shape, lambda b, j: (0, 0, 0))],
        out_specs=pl.BlockSpec((1, nhp, N_KEYS, tm), lambda b, j: (b, 0, 0, j)),
        compiler_params=_params(("parallel", "parallel")),
        name="peer_scores",
    )(h2, wq, subkeys)


def _sorted_top(s, k, with_rank):
    out = []
    rank = jnp.full(s.shape, float(k), F32) if with_rank else None
    for r in range(k):
        m = jnp.max(s, axis=0, keepdims=True)
        out.append(m)
        hit = s == m
        if with_rank:
            rank = jnp.where(hit, float(r), rank)
        s = jnp.where(hit, NEG_BIG, s)
    return out, rank


def _paired_bf16_words(x):
    w = pltpu.bitcast(x.astype(BF16).astype(F32), jnp.uint32)
    return w | (w >> 16)


def _peer_topk_kernel(st_ref, cnt_out, e0_out, rank_out, e1_out):
    nt = PEER_TOPK + 1
    tops = ([], [])
    for hd in range(PEER_HEADS):
        a, _ = _sorted_top(st_ref[0, 2 * hd], nt, False)
        b, rank = _sorted_top(st_ref[0, 2 * hd + 1], nt, True)
        rank_out[0, hd] = rank.astype(BF16)
        tops[0].append(a)
        tops[1].append(b)
    a8 = [jnp.concatenate([tops[0][hd][r] for hd in range(PEER_HEADS)], axis=0) for r in range(nt)]
    b8 = [jnp.concatenate([tops[1][hd][r] for hd in range(PEER_HEADS)], axis=0) for r in range(nt)]
    cand = [a8[i] + b8[j] for i in range(nt) for j in range(nt) if (i + 1) * (j + 1) <= nt]
    top = []
    for _ in range(nt):
        m = functools.reduce(jnp.maximum, cand)
        top.append(m)
        cand = [jnp.where(t == m, NEG_BIG, t) for t in cand]
    tau8 = 0.5 * (top[PEER_TOPK - 1] + top[PEER_TOPK])
    smax = a8[0] + b8[0]
    z8 = jnp.exp(top[0] - smax)
    for t in top[1:PEER_TOPK]:
        z8 = z8 + jnp.exp(t - smax)
    rz8 = 1.0 / z8
    for hd in range(PEER_HEADS):
        s0 = st_ref[0, 2 * hd]
        s1 = st_ref[0, 2 * hd + 1]
        thr = tau8[hd:hd + 1] - s0
        cnt = jnp.zeros(s0.shape, F32)
        for r in range(nt):
            cnt = cnt + jnp.where(tops[1][hd][r] > thr, 1.0, 0.0)
        cnt_out[0, hd] = _paired_bf16_words(cnt)
        e0_out[0, hd] = _paired_bf16_words(jnp.exp(s0 - tops[0][hd][0]))
        e1_out[0, hd] = (jnp.exp(s1 - tops[1][hd][0]) * rz8[hd:hd + 1]).astype(BF16)


def peer_topk(st):
    bsz, nhp, nk, s = st.shape
    tl = LANES
    spec = pl.BlockSpec((1, PEER_HEADS, nk, tl), lambda b, j: (b, 0, 0, j))
    words = jax.ShapeDtypeStruct((bsz, PEER_HEADS, nk, s), jnp.uint32)
    halfs = jax.ShapeDtypeStruct((bsz, PEER_HEADS, nk, s), BF16)
    return pl.pallas_call(
        _peer_topk_kernel,
        out_shape=(words, words, halfs, halfs),
        grid=(bsz, s // tl),
        in_specs=[pl.BlockSpec((1, nhp, nk, tl), lambda b, j: (b, 0, 0, j))],
        out_specs=(spec, spec, spec, spec),
        compiler_params=_params(("parallel", "parallel")),
        name="peer_topk",
    )(st)


def _peer_expert_kernel(x_ref, mod_ref, h2_ref, u_ref, vt_ref, cnt_ref, e0_ref, rank_ref, e1_ref,
                        x_out, acc_sc, *, ctx_len):
    tok_block = pl.program_id(1)
    ec = pl.program_id(2)
    n_ec = pl.num_programs(2)
    d = x_ref.shape[-1]
    tb = x_ref.shape[1]
    pk = 2 * SUBLANES

    @pl.when(ec == 0)
    def _():
        acc_sc[...] = jnp.zeros(acc_sc.shape, F32)

    def row_tile(ref, hd, ii):
        return pltpu.bitcast(jnp.broadcast_to(ref[0, hd, ii:ii + 1, :], (SUBLANES, tb)), BF16)

    h2 = h2_ref[0]
    g = []
    for ii in range(PEER_I_PER_STEP):
        if ii % 2 == 0:
            at2 = lax.dot_general(u_ref[ii * N_KEYS:(ii + 2) * N_KEYS, :], h2, (((1,), (1,)), ((), ())),
                                  preferred_element_type=F32)
        at = at2[(ii % 2) * N_KEYS:(ii % 2 + 1) * N_KEYS]
        wt = [jnp.zeros((pk, tb), BF16) for _ in range(N_KEYS // pk)]
        for hd in range(PEER_HEADS):
            cnt = row_tile(cnt_ref, hd, ii)
            e0 = row_tile(e0_ref, hd, ii)
            for rt in range(N_KEYS // pk):
                rows = slice(rt * pk, (rt + 1) * pk)
                sel = jnp.where(rank_ref[0, hd, rows, :] < cnt, e1_ref[0, hd, rows, :], jnp.zeros((), BF16))
                wt[rt] = wt[rt] + sel * e0
        for rt in range(N_KEYS // pk):
            a = at[rt * pk:(rt + 1) * pk]
            act = 0.5 * a * (1.0 + lax.erf(a * (2.0 ** -0.5)))
            g.append(act.astype(BF16) * wt[rt])
    acc_sc[...] += jnp.dot(vt_ref[...], jnp.concatenate(g, axis=0), preferred_element_type=F32)

    @pl.when(ec == n_ec - 1)
    def _():
        tok = tok_block * tb + lax.broadcasted_iota(jnp.int32, (tb, 1), 0)
        g2 = jnp.where(tok < ctx_len, mod_ref[0, 0, :, 5 * d:6 * d], mod_ref[0, 1, :, 5 * d:6 * d])
        x_out[0] = x_ref[0] + g2 * acc_sc[...].T


def peer_experts(xs, mod, h2, u_tab, vt_tab, cnt, e0, rank, e1, ctx_len):
    bsz, s, d = xs.shape
    tb = PEER_TOKEN_BLOCK if s % PEER_TOKEN_BLOCK == 0 else TOKEN_BLOCK
    ech = PEER_I_PER_STEP * N_KEYS
    n_exp = u_tab.shape[0]
    tok = lambda b, j, e: (b, j, 0)
    per_tok = lambda b, j, e: (b, 0, 0, j)
    per_i = lambda b, j, e: (b, 0, e, j)
    return pl.pallas_call(
        functools.partial(_peer_expert_kernel, ctx_len=ctx_len),
        out_shape=jax.ShapeDtypeStruct((bsz, s, d), F32),
        grid=(bsz, s // tb, n_exp // ech),
        in_specs=[pl.BlockSpec((1, tb, d), tok),
                  pl.BlockSpec((1, 2, 1, mod.shape[-1]), lambda b, j, e: (b, 0, 0, 0)),
                  pl.BlockSpec((1, tb, d), tok),
                  pl.BlockSpec((ech, d), lambda b, j, e: (e, 0)),
                  pl.BlockSpec((d, ech), lambda b, j, e: (0, e)),
                  pl.BlockSpec((1, PEER_HEADS, PEER_I_PER_STEP, tb), per_i),
                  pl.BlockSpec((1, PEER_HEADS, PEER_I_PER_STEP, tb), per_i),
                  pl.BlockSpec((1, PEER_HEADS, N_KEYS, tb), per_tok),
                  pl.BlockSpec((1, PEER_HEADS, N_KEYS, tb), per_tok)],
        out_specs=pl.BlockSpec((1, tb, d), tok),
        scratch_shapes=[pltpu.VMEM((d, tb), F32)],
        compiler_params=_params(("parallel", "parallel", "arbitrary")),
        name="peer_experts",
    )(xs, mod, h2, u_tab, vt_tab, cnt, e0, rank, e1)


def _rope_tables(ctx_len, n_lat):
    rows = n_lat // GRID_W
    row = jnp.repeat(jnp.arange(rows, dtype=F32), GRID_W)
    col = jnp.tile(jnp.arange(GRID_W, dtype=F32), rows)
    axis_dim = HEAD_DIM // 2
    inv_freq = ROPE_THETA ** (-jnp.arange(0, axis_dim, 2, dtype=F32) / axis_dim)
    ang_r = row[:, None] * inv_freq[None, :]
    ang_c = col[:, None] * inv_freq[None, :]
    ang = jnp.concatenate([ang_r, ang_r, ang_c, ang_c], axis=-1)
    cos, sin = jnp.cos(ang), jnp.sin(ang)
    first = (jnp.arange(HEAD_DIM) % (HEAD_DIM // 2)) < (HEAD_DIM // 4)
    sina = jnp.where(first, -sin, 0.0)
    sinb = jnp.where(first, 0.0, sin)
    pad = lambda t, v: jnp.concatenate([jnp.full((ctx_len, HEAD_DIM), v, F32), t], axis=0)
    return pad(cos, 1.0), pad(sina, 0.0), pad(sinb, 0.0)


def kernel(x, c, ctx, c_ctx, ada_w, ada_b, norm1_w, norm2_w, w_in, attn_qnorm_w, attn_knorm_w, dn_conv_w,
           dn_A_log, dn_dt_bias, dn_norm_w, w_out, peer_wq, peer_subkeys, peer_u, peer_v):
    bsz, n_lat, d = x.shape
    ctx_len = ctx.shape[1]
    depth = ada_w.shape[0]
    assert ctx_len == TOKEN_BLOCK and n_lat % TOKEN_BLOCK == 0 and bsz + 1 <= SUBLANES
    assert w_in.shape[-1] == IN_MAIN_W + N_GATE_COLS

    xs = jnp.concatenate([ctx, x], axis=1)
    cos, sina, sinb = _rope_tables(ctx_len, n_lat)

    cc = jnp.zeros((SUBLANES, d), F32).at[:bsz].set(c).at[bsz].set(c_ctx)
    mod_all = ada_modulation(cc, ada_w, ada_b)
    mod_ctx = jnp.broadcast_to(mod_all[:, bsz][:, None], (depth, bsz, 6 * d))
    mod = jnp.stack([mod_ctx, mod_all[:, :bsz]], axis=2)[:, :, :, None, :]

    for l in range(depth):
        w_main = w_in[l, :, :IN_MAIN_W].astype(BF16)
        w_gate = jnp.pad(w_in[l, :, IN_MAIN_W:], ((0, 0), (0, LANES - N_GATE_COLS))).astype(BF16)
        q, k, v, dnqkv, gate, ba = in_projection(xs, mod[l], norm1_w[l], w_main, w_gate, cos, sina, sinb,
                                                 attn_qnorm_w[l], attn_knorm_w[l])
        attn = attention(q, k, v, ctx_len)
        conv_w = jnp.pad(dn_conv_w[l], ((0, SUBLANES - CONV_K), (0, 0)))
        gate_par = jnp.zeros((SUBLANES, LANES), F32)
        gate_par = gate_par.at[0, 2 * DN_HEADS:4 * DN_HEADS].set(dn_A_log[l].reshape(-1))
        gate_par = gate_par.at[1, 2 * DN_HEADS:4 * DN_HEADS].set(dn_dt_bias[l].reshape(-1))
        o_f, o_b = dn_scan(*dn_prepare(dnqkv, ba, conv_w, gate_par, ctx_len), ctx_len)
        xs, h2 = out_projection(xs, mod[l], attn, o_f, o_b, gate, dn_norm_w[l], w_out[l].astype(BF16),
                                norm2_w[l])
        sk = peer_subkeys[l].reshape(2 * PEER_HEADS, N_KEYS, PEER_HALF).astype(BF16)
        st = peer_scores(h2, peer_wq[l].astype(BF16), sk)
        cnt, e0, rank, e1 = peer_topk(st)
        xs = peer_experts(xs, mod[l], h2, peer_u[l].astype(BF16), peer_v[l].T.astype(BF16), cnt, e0, rank, e1,
                          ctx_len)
    return xs[:, ctx_len:]
```

```python
import functools
import math

import jax
import jax.numpy as jnp
from jax import lax
from jax.experimental import pallas as pl
from jax.experimental.pallas import tpu as pltpu

F32 = jnp.float32
BF16 = jnp.bfloat16
HIGHEST = lax.Precision.HIGHEST

HEAD_DIM = 128
ATTN_HEADS = 4
ATTN_KV_HEADS = 2
ATTN_GROUP = ATTN_HEADS // ATTN_KV_HEADS
DN_HEADS = 4
ATTN_Q_W = ATTN_HEADS * HEAD_DIM
ATTN_KV_W = ATTN_KV_HEADS * HEAD_DIM
DN_W = DN_HEADS * HEAD_DIM
IN_MAIN_W = ATTN_Q_W + 2 * ATTN_KV_W + 3 * DN_W + DN_W
N_GATE_COLS = 4 * DN_HEADS
ROPE_THETA = 10000.0
GRID_W = 64
CONV_K = 5
CONV_PAD = CONV_K // 2
PEER_HEADS = 8
PEER_HALF = 128
N_KEYS = 128
PEER_TOPK = 16
EPS = 1e-6
NEG_BIG = -1e30
GELU_GATE_SCALE = 2.0 ** -0.5

LANES = 128
SUBLANES = 8
TOKEN_BLOCK = 256
DN_CHUNK = 128
ATTN_KEY_BLOCK = 1280
ATTN_Q_BLOCK = 512
ATTN_CHAIN_ROWS = 256
PEER_TOKEN_BLOCK = 640
PEER_I_PER_STEP = 8
VMEM_LIMIT = 56 * 1024 * 1024


def _params(sem):
    return pltpu.CompilerParams(dimension_semantics=sem, vmem_limit_bytes=VMEM_LIMIT)


def _bf16_dot(a, b):
    return jnp.dot(a.astype(BF16), b.astype(BF16), preferred_element_type=F32)


def _split_dot(a, b):
    a_hi = a.astype(BF16)
    b_hi = b.astype(BF16)
    a_lo = (a - a_hi.astype(F32)).astype(BF16)
    b_lo = (b - b_hi.astype(F32)).astype(BF16)
    dot = functools.partial(jnp.dot, preferred_element_type=F32)
    return dot(a_hi, b_hi) + (dot(a_hi, b_lo) + dot(a_lo, b_hi))


def _bf16_dot_nt(a, b):
    return lax.dot_general(a.astype(BF16), b.astype(BF16), (((1,), (1,)), ((), ())),
                           preferred_element_type=F32)


def _ada_kernel(c_ref, w_ref, b_ref, o_ref):
    c = c_ref[...]
    a = c * jax.nn.sigmoid(c)
    o_ref[0] = jnp.dot(a, w_ref[0], preferred_element_type=F32, precision=HIGHEST) + b_ref[0]


def ada_modulation(cc, ada_w, ada_b):
    depth, d, n = ada_w.shape
    tn = 1536
    return pl.pallas_call(
        _ada_kernel,
        out_shape=jax.ShapeDtypeStruct((depth, SUBLANES, n), F32),
        grid=(depth, n // tn),
        in_specs=[pl.BlockSpec((SUBLANES, d), lambda l, j: (0, 0)),
                  pl.BlockSpec((1, d, tn), lambda l, j: (l, 0, j)),
                  pl.BlockSpec((1, 1, tn), lambda l, j: (l, 0, j))],
        out_specs=pl.BlockSpec((1, SUBLANES, tn), lambda l, j: (l, 0, j)),
        compiler_params=_params(("parallel", "parallel")),
        name="ada_modulation",
    )(cc, ada_w, ada_b.reshape(depth, 1, n))


def _mod_index(b, j):
    return (b, jnp.minimum(j, 1), 0, 0)


def _modulated_norm(x, nw, shift, scale):
    ms = jnp.mean(x * x, axis=-1, keepdims=True)
    y = x * lax.rsqrt(ms + EPS) * nw
    return y * (1.0 + scale) + shift


def _head_rmsnorm(x, w):
    return x * lax.rsqrt(jnp.mean(x * x, axis=-1, keepdims=True) + EPS) * w


def _inproj_kernel(x_ref, mod_ref, nw_ref, wm_ref, wg_ref, cos_ref, sina_ref, sinb_ref, qnw_ref, knw_ref,
                   q_out, k_out, v_out, dn_out, gate_out, ba_out):
    d = x_ref.shape[-1]
    x = x_ref[0]
    h = _modulated_norm(x, nw_ref[...], mod_ref[0, 0, :, 0:d], mod_ref[0, 0, :, d:2 * d]).astype(BF16)
    p = jnp.dot(h, wm_ref[...], preferred_element_type=F32)
    ba_out[0] = jnp.dot(h, wg_ref[...], preferred_element_type=F32)
    cos = cos_ref[...]
    sina = sina_ref[...]
    sinb = sinb_ref[...]

    def rope(t):
        return (t * cos + pltpu.roll(t, HEAD_DIM - HEAD_DIM // 4, 1) * sina
                + pltpu.roll(t, HEAD_DIM // 4, 1) * sinb)

    scale = HEAD_DIM ** -0.5 * math.log2(math.e)
    for hd in range(ATTN_HEADS):
        qh = _head_rmsnorm(p[:, hd * HEAD_DIM:(hd + 1) * HEAD_DIM], qnw_ref[...])
        q_out[0, hd] = (rope(qh) * scale).astype(BF16)
    off = ATTN_Q_W
    for hd in range(ATTN_KV_HEADS):
        kh = _head_rmsnorm(p[:, off + hd * HEAD_DIM:off + (hd + 1) * HEAD_DIM], knw_ref[...])
        k_out[0, hd] = rope(kh).astype(BF16)
    off += ATTN_KV_W
    for hd in range(ATTN_KV_HEADS):
        v_out[0, hd] = p[:, off + hd * HEAD_DIM:off + (hd + 1) * HEAD_DIM].astype(BF16)
    off += ATTN_KV_W
    dn_out[0] = p[:, off:off + 3 * DN_W]
    off += 3 * DN_W
    gate_out[0] = p[:, off:off + DN_W]


def in_projection(xs, mod, norm_w, w_main, w_gate, cos, sina, sinb, qn_w, kn_w):
    bsz, s, d = xs.shape
    tm = TOKEN_BLOCK
    tok = lambda b, j: (b, j, 0)
    head_tok = lambda b, j: (b, 0, j, 0)
    full2 = lambda b, j: (0, 0)
    rope_spec = pl.BlockSpec((tm, HEAD_DIM), lambda b, j: (j, 0))
    return pl.pallas_call(
        _inproj_kernel,
        out_shape=(jax.ShapeDtypeStruct((bsz, ATTN_HEADS, s, HEAD_DIM), BF16),
                   jax.ShapeDtypeStruct((bsz, ATTN_KV_HEADS, s, HEAD_DIM), BF16),
                   jax.ShapeDtypeStruct((bsz, ATTN_KV_HEADS, s, HEAD_DIM), BF16),
                   jax.ShapeDtypeStruct((bsz, s, 3 * DN_W), F32),
                   jax.ShapeDtypeStruct((bsz, s, DN_W), F32),
                   jax.ShapeDtypeStruct((bsz, s, LANES), F32)),
        grid=(bsz, s // tm),
        in_specs=[pl.BlockSpec((1, tm, d), tok),
                  pl.BlockSpec((1, 1, 1, mod.shape[-1]), _mod_index),
                  pl.BlockSpec((1, d), full2),
                  pl.BlockSpec((d, IN_MAIN_W), full2),
                  pl.BlockSpec((d, LANES), full2),
                  rope_spec, rope_spec, rope_spec,
                  pl.BlockSpec((1, HEAD_DIM), full2),
                  pl.BlockSpec((1, HEAD_DIM), full2)],
        out_specs=(pl.BlockSpec((1, ATTN_HEADS, tm, HEAD_DIM), head_tok),
                   pl.BlockSpec((1, ATTN_KV_HEADS, tm, HEAD_DIM), head_tok),
                   pl.BlockSpec((1, ATTN_KV_HEADS, tm, HEAD_DIM), head_tok),
                   pl.BlockSpec((1, tm, 3 * DN_W), tok),
                   pl.BlockSpec((1, tm, DN_W), tok),
                   pl.BlockSpec((1, tm, LANES), tok)),
        compiler_params=_params(("parallel", "parallel")),
        name="in_projection",
    )(xs, mod, norm_w.reshape(1, d), w_main, w_gate, cos, sina, sinb,
      qn_w.reshape(1, HEAD_DIM), kn_w.reshape(1, HEAD_DIM))


def _attn_kernel(q_ref, k_ref, v_ref, o_ref, m_sc, l_sc, acc_sc):
    ki = pl.program_id(2)
    nk = pl.num_programs(2)

    @pl.when(ki == 0)
    def _():
        m_sc[...] = jnp.full(m_sc.shape, NEG_BIG, F32)
        l_sc[...] = jnp.zeros(l_sc.shape, F32)
        acc_sc[...] = jnp.zeros(acc_sc.shape, F32)

    tq = q_ref.shape[2]
    chains = [(hd, pl.ds(r0, ATTN_CHAIN_ROWS)) for hd in range(ATTN_HEADS)
              for r0 in range(0, tq, ATTN_CHAIN_ROWS)]
    scores = [lax.dot_general(q_ref[0, hd, rows, :], k_ref[0, hd // ATTN_GROUP], (((1,), (1,)), ((), ())),
                              preferred_element_type=F32) for hd, rows in chains]
    for (hd, rows), s in zip(chains, scores):
        m_prev = m_sc[hd, rows, :]
        m_new = jnp.maximum(m_prev, jnp.max(s, axis=-1, keepdims=True))
        p = jnp.exp2(s - m_new)
        alpha = jnp.exp2(m_prev - m_new)
        l_sc[hd, rows, :] = alpha * l_sc[hd, rows, :] + jnp.sum(p, axis=-1, keepdims=True)
        acc_sc[hd, rows, :] = alpha * acc_sc[hd, rows, :] + jnp.dot(
            p.astype(BF16), v_ref[0, hd // ATTN_GROUP], preferred_element_type=F32)
        m_sc[hd, rows, :] = m_new

    @pl.when(ki == nk - 1)
    def _():
        for hd in range(ATTN_HEADS):
            o_ref[0, :, hd * HEAD_DIM:(hd + 1) * HEAD_DIM] = (acc_sc[hd] / l_sc[hd]).astype(o_ref.dtype)


def _attention_call(q, k, v, tq, n_keys, tk, name):
    bsz = q.shape[0]
    n_q_blocks = q.shape[2] // tq
    return pl.pallas_call(
        _attn_kernel,
        out_shape=jax.ShapeDtypeStruct((bsz, n_q_blocks * tq, ATTN_Q_W), BF16),
        grid=(bsz, n_q_blocks, n_keys // tk),
        in_specs=[pl.BlockSpec((1, ATTN_HEADS, tq, HEAD_DIM), lambda b, i, j: (b, 0, i, 0)),
                  pl.BlockSpec((1, ATTN_KV_HEADS, tk, HEAD_DIM), lambda b, i, j: (b, 0, j, 0)),
                  pl.BlockSpec((1, ATTN_KV_HEADS, tk, HEAD_DIM), lambda b, i, j: (b, 0, j, 0))],
        out_specs=pl.BlockSpec((1, tq, ATTN_Q_W), lambda b, i, j: (b, i, 0)),
        scratch_shapes=[pltpu.VMEM((ATTN_HEADS, tq, 1), F32),
                        pltpu.VMEM((ATTN_HEADS, tq, 1), F32),
                        pltpu.VMEM((ATTN_HEADS, tq, HEAD_DIM), F32)],
        compiler_params=_params(("parallel", "parallel", "arbitrary")),
        name=name,
    )(q, k, v)


def attention(q, k, v, ctx_len):
    s = q.shape[2]
    assert ctx_len % ATTN_CHAIN_ROWS == 0 and (s - ctx_len) % ATTN_Q_BLOCK == 0
    tk = ATTN_KEY_BLOCK if s % ATTN_KEY_BLOCK == 0 else ATTN_CHAIN_ROWS
    attn_ctx = _attention_call(q[:, :, :ctx_len], k, v, ctx_len, ctx_len, ctx_len, "attention_ctx")
    attn_lat = _attention_call(q[:, :, ctx_len:], k, v, ATTN_Q_BLOCK, s, tk, "attention")
    return jnp.concatenate([attn_ctx, attn_lat], axis=1)


def _dn_prep_kernel(main_ref, prev_ref, next_ref, ba_ref, cw_ref, gp_ref,
                    w_out, u_out, qg_out, kdt_out, qk_out, dl_out, ext_sc, *, ctx_chunks):
    j = pl.program_id(1)
    nj = pl.num_programs(1)
    c = DN_CHUNK
    has_prev = (j != 0) & (j != ctx_chunks)
    has_next = (j != ctx_chunks - 1) & (j != nj - 1)
    ext_sc[0:SUBLANES] = jnp.where(has_prev, prev_ref[0], 0.0)
    ext_sc[SUBLANES:SUBLANES + c] = main_ref[0]
    ext_sc[SUBLANES + c:2 * SUBLANES + c] = jnp.where(has_next, next_ref[0], 0.0)
    y = ext_sc[SUBLANES - CONV_PAD:SUBLANES - CONV_PAD + c] * cw_ref[0:1]
    for t in range(1, CONV_K):
        y = y + ext_sc[SUBLANES - CONV_PAD + t:SUBLANES - CONV_PAD + t + c] * cw_ref[t:t + 1]
    y = y * jax.nn.sigmoid(y)

    ba = ba_ref[0]
    beta_all = jax.nn.sigmoid(ba)
    g_all = -jnp.exp(gp_ref[0:1]) * jax.nn.softplus(ba + gp_ref[1:2])
    row = lax.broadcasted_iota(jnp.int32, (c, c), 0)
    col = lax.broadcasted_iota(jnp.int32, (c, c), 1)
    lower = (row >= col).astype(BF16)
    g_hi = g_all.astype(BF16)
    g_r1 = g_all - g_hi.astype(F32)
    g_mid = g_r1.astype(BF16)
    g_lo = (g_r1 - g_mid.astype(F32)).astype(BF16)
    dotf = functools.partial(jnp.dot, preferred_element_type=F32)
    prefix = dotf(lower, g_hi) + (dotf(lower, g_mid) + dotf(lower, g_lo))
    total = prefix[c - 1:c]
    gc = (prefix, total - prefix + g_all)
    gct = (gc[0].T, gc[1].T)
    row2 = lax.broadcasted_iota(jnp.int32, (2 * c, 2 * c), 0)
    col2 = lax.broadcasted_iota(jnp.int32, (2 * c, 2 * c), 1)
    eye2 = (row2 == col2).astype(F32)
    zero = jnp.zeros((c, c), F32)

    heads = range(DN_HEADS)
    a2, rhs2 = [], []
    for hd in heads:
        q = y[:, hd * HEAD_DIM:(hd + 1) * HEAD_DIM]
        k = y[:, DN_W + hd * HEAD_DIM:DN_W + (hd + 1) * HEAD_DIM]
        v = y[:, 2 * DN_W + hd * HEAD_DIM:2 * DN_W + (hd + 1) * HEAD_DIM]
        q = q * lax.rsqrt(jnp.sum(q * q, axis=-1, keepdims=True) + EPS) * (HEAD_DIM ** -0.5)
        k = k * lax.rsqrt(jnp.sum(k * k, axis=-1, keepdims=True) + EPS)
        kk = _bf16_dot_nt(k, k)
        qk = _bf16_dot_nt(q, k)
        a_dir, rhs_dir = [], []
        for dr in range(2):
            cb = dr * DN_HEADS + hd
            cg = 2 * DN_HEADS + cb
            beta = beta_all[:, cb:cb + 1]
            gcol = gc[dr][:, cg:cg + 1]
            grow = gct[dr][cg:cg + 1, :]
            incl = (row >= col) if dr == 0 else (row <= col)
            strict = (row > col) if dr == 0 else (row < col)
            decay = jnp.exp(jnp.where(incl, gcol - grow, NEG_BIG))
            a_dir.append(jnp.where(strict, kk * beta * decay, 0.0))
            eg = jnp.exp(gcol)
            rhs_dir.append(jnp.concatenate([v * beta, k * (beta * eg)], axis=1))
            qg_out[0, dr, hd] = (q * eg).astype(BF16)
            tot = total[:, cg:cg + 1]
            kdt_out[0, dr, hd] = (k * jnp.exp(tot - gcol)).T.astype(BF16)
            qk_out[0, dr, hd] = (qk * decay).astype(BF16)
            dl_out[0, dr, hd, 0] = jnp.broadcast_to(jnp.exp(tot), (1, LANES))
        a2.append(jnp.concatenate([jnp.concatenate([a_dir[0], zero], axis=1),
                                   jnp.concatenate([zero, a_dir[1]], axis=1)], axis=0))
        rhs2.append(jnp.concatenate(rhs_dir, axis=0))
    base = SUBLANES
    base_mask = (row2 // base) == (col2 // base)
    a_base = [jnp.where(base_mask, a2[hd], 0.0) for hd in heads]
    x = [eye2 - a_base[hd] for hd in heads]
    pw = [_split_dot(a_base[hd], a_base[hd]) for hd in heads]
    for it in range(int(math.log2(base)) - 1):
        x = [x[hd] + _split_dot(x[hd], pw[hd]) for hd in heads]
        if it < int(math.log2(base)) - 2:
            pw = [_split_dot(pw[hd], pw[hd]) for hd in heads]
    blk = base
    while blk < c:
        sibling = ((row2 // (2 * blk)) == (col2 // (2 * blk))) & ((row2 // blk) != (col2 // blk))
        fold = [_bf16_dot(x[hd], jnp.where(sibling, a2[hd], 0.0)) for hd in heads]
        x = [x[hd] - _bf16_dot(fold[hd], x[hd]) for hd in heads]
        blk *= 2
    for hd in heads:
        uw = _bf16_dot(x[hd], rhs2[hd])
        for dr in range(2):
            u_out[0, dr, hd] = uw[dr * c:(dr + 1) * c, 0:HEAD_DIM]
            w_out[0, dr, hd] = uw[dr * c:(dr + 1) * c, HEAD_DIM:2 * HEAD_DIM].astype(BF16)


def dn_prepare(dnqkv, ba, conv_w, gate_par, ctx_len):
    bsz, s, wdt = dnqkv.shape
    c = DN_CHUNK
    nc = s // c
    rows8 = s // SUBLANES
    per = c // SUBLANES
    chain = lambda b, j: (b, 0, 0, j, 0)
    return pl.pallas_call(
        functools.partial(_dn_prep_kernel, ctx_chunks=ctx_len // c),
        out_shape=(jax.ShapeDtypeStruct((bsz, 2, DN_HEADS, s, HEAD_DIM), BF16),
                   jax.ShapeDtypeStruct((bsz, 2, DN_HEADS, s, HEAD_DIM), F32),
                   jax.ShapeDtypeStruct((bsz, 2, DN_HEADS, s, HEAD_DIM), BF16),
                   jax.ShapeDtypeStruct((bsz, 2, DN_HEADS, HEAD_DIM, s), BF16),
                   jax.ShapeDtypeStruct((bsz, 2, DN_HEADS, s, c), BF16),
                   jax.ShapeDtypeStruct((bsz, 2, DN_HEADS, nc, 1, LANES), F32)),
        grid=(bsz, nc),
        in_specs=[pl.BlockSpec((1, c, wdt), lambda b, j: (b, j, 0)),
                  pl.BlockSpec((1, SUBLANES, wdt), lambda b, j: (b, jnp.maximum(j * per - 1, 0), 0)),
                  pl.BlockSpec((1, SUBLANES, wdt), lambda b, j: (b, jnp.minimum((j + 1) * per, rows8 - 1), 0)),
                  pl.BlockSpec((1, c, LANES), lambda b, j: (b, j, 0)),
                  pl.BlockSpec((SUBLANES, wdt), lambda b, j: (0, 0)),
                  pl.BlockSpec((SUBLANES, LANES), lambda b, j: (0, 0))],
        out_specs=(pl.BlockSpec((1, 2, DN_HEADS, c, HEAD_DIM), chain),
                   pl.BlockSpec((1, 2, DN_HEADS, c, HEAD_DIM), chain),
                   pl.BlockSpec((1, 2, DN_HEADS, c, HEAD_DIM), chain),
                   pl.BlockSpec((1, 2, DN_HEADS, HEAD_DIM, c), lambda b, j: (b, 0, 0, 0, j)),
                   pl.BlockSpec((1, 2, DN_HEADS, c, c), chain),
                   pl.BlockSpec((1, 2, DN_HEADS, 1, 1, LANES), lambda b, j: (b, 0, 0, j, 0, 0))),
        scratch_shapes=[pltpu.VMEM((c + 2 * SUBLANES, wdt), F32)],
        compiler_params=_params(("parallel", "parallel")),
        name="dn_prepare",
    )(dnqkv, dnqkv, dnqkv, ba, conv_w, gate_par)


def _dn_scan_kernel(*refs, bsz):
    ins = refs[:12]
    of_ref, ob_ref, s_sc = refs[12:]
    n = pl.program_id(0)

    @pl.when(n == 0)
    def _():
        s_sc[...] = jnp.zeros(s_sc.shape, F32)

    chains = [(dr, b, hd) for dr in range(2) for b in range(bsz) for hd in range(DN_HEADS)]
    dotf = functools.partial(jnp.dot, preferred_element_type=F32)

    def inp(dr, k):
        return ins[dr * 6 + k]

    state = [s_sc[ci] for ci in range(len(chains))]
    r = [dotf(jnp.concatenate([inp(dr, 0)[b, 0, hd], inp(dr, 2)[b, 0, hd]], axis=0), state[ci].astype(BF16))
         for ci, (dr, b, hd) in enumerate(chains)]
    v_new = [(inp(dr, 1)[b, 0, hd] - r[ci][0:DN_CHUNK]).astype(BF16) for ci, (dr, b, hd) in enumerate(chains)]
    intra = [dotf(inp(dr, 4)[b, 0, hd], v_new[ci]) for ci, (dr, b, hd) in enumerate(chains)]
    upd = [dotf(inp(dr, 3)[b, 0, hd], v_new[ci]) for ci, (dr, b, hd) in enumerate(chains)]
    for ci, (dr, b, hd) in enumerate(chains):
        o_ref = of_ref if dr == 0 else ob_ref
        o_ref[b, :, hd * HEAD_DIM:(hd + 1) * HEAD_DIM] = r[ci][DN_CHUNK:] + intra[ci]
        s_sc[ci] = state[ci] * inp(dr, 5)[b, 0, hd, 0] + upd[ci]


def dn_scan(w, u, qg, kdt, qk, dl, ctx_len):
    bsz, _, _, s, _ = w.shape
    c = DN_CHUNK
    nc = s // c
    cc = ctx_len // c

    def bwd_chunk(n):
        return jnp.where(n < cc, cc - 1 - n, nc - 1 - (n - cc))

    in_specs, args = [], []
    for dr in range(2):
        pos = (lambda n: n) if dr == 0 else bwd_chunk
        tokm = lambda n, dr=dr, pos=pos: (0, dr, 0, pos(n), 0)
        for arr in (w, u, qg):
            in_specs.append(pl.BlockSpec((bsz, 1, DN_HEADS, c, HEAD_DIM), tokm))
            args.append(arr)
        in_specs.append(pl.BlockSpec((bsz, 1, DN_HEADS, HEAD_DIM, c), lambda n, dr=dr, pos=pos: (0, dr, 0, 0, pos(n))))
        args.append(kdt)
        in_specs.append(pl.BlockSpec((bsz, 1, DN_HEADS, c, c), tokm))
        args.append(qk)
        in_specs.append(pl.BlockSpec((bsz, 1, DN_HEADS, 1, 1, LANES), lambda n, dr=dr, pos=pos: (0, dr, 0, pos(n), 0, 0)))
        args.append(dl)
    return pl.pallas_call(
        functools.partial(_dn_scan_kernel, bsz=bsz),
        out_shape=(jax.ShapeDtypeStruct((bsz, s, DN_W), F32), jax.ShapeDtypeStruct((bsz, s, DN_W), F32)),
        grid=(nc,),
        in_specs=in_specs,
        out_specs=(pl.BlockSpec((bsz, c, DN_W), lambda n: (0, n, 0)),
                   pl.BlockSpec((bsz, c, DN_W), lambda n: (0, bwd_chunk(n), 0))),
        scratch_shapes=[pltpu.VMEM((2 * bsz * DN_HEADS, HEAD_DIM, HEAD_DIM), F32)],
        compiler_params=_params(("arbitrary",)),
        name="dn_scan",
    )(*args)


def _outproj_kernel(x_ref, mod_ref, attn_ref, of_ref, ob_ref, gate_ref, dnw_ref, wo_ref, n2w_ref,
                    x_out, h2_out):
    d = x_ref.shape[-1]
    o = of_ref[0] + ob_ref[0]
    gate = gate_ref[0]
    parts = [attn_ref[0]]
    for hd in range(DN_HEADS):
        sl = slice(hd * HEAD_DIM, (hd + 1) * HEAD_DIM)
        g = gate[:, sl]
        parts.append((_head_rmsnorm(o[:, sl], dnw_ref[...]) * (g * jax.nn.sigmoid(g))).astype(BF16))
    mix = jnp.concatenate(parts, axis=1)
    y = jnp.dot(mix, wo_ref[...], preferred_element_type=F32)
    x = x_ref[0] + mod_ref[0, 0, :, 2 * d:3 * d] * y
    x_out[0] = x
    h2_out[0] = _modulated_norm(x, n2w_ref[...], mod_ref[0, 0, :, 3 * d:4 * d],
                                mod_ref[0, 0, :, 4 * d:5 * d]).astype(BF16)


def out_projection(xs, mod, attn, o_f, o_b, gate, dn_norm_w, w_out, norm2_w):
    bsz, s, d = xs.shape
    tm = TOKEN_BLOCK
    tok = lambda b, j: (b, j, 0)
    full2 = lambda b, j: (0, 0)
    return pl.pallas_call(
        _outproj_kernel,
        out_shape=(jax.ShapeDtypeStruct((bsz, s, d), F32), jax.ShapeDtypeStruct((bsz, s, d), BF16)),
        grid=(bsz, s // tm),
        in_specs=[pl.BlockSpec((1, tm, d), tok),
                  pl.BlockSpec((1, 1, 1, mod.shape[-1]), _mod_index),
                  pl.BlockSpec((1, tm, ATTN_Q_W), tok),
                  pl.BlockSpec((1, tm, DN_W), tok),
                  pl.BlockSpec((1, tm, DN_W), tok),
                  pl.BlockSpec((1, tm, DN_W), tok),
                  pl.BlockSpec((1, HEAD_DIM), full2),
                  pl.BlockSpec(w_out.shape, full2),
                  pl.BlockSpec((1, d), full2)],
        out_specs=(pl.BlockSpec((1, tm, d), tok), pl.BlockSpec((1, tm, d), tok)),
        compiler_params=_params(("parallel", "parallel")),
        name="out_projection",
    )(xs, mod, attn, o_f, o_b, gate, dn_norm_w.reshape(1, HEAD_DIM), w_out, norm2_w.reshape(1, d))


def _peer_score_kernel(h2_ref, wq_ref, sk_ref, st_out):
    q = jnp.dot(h2_ref[0], wq_ref[...], preferred_element_type=F32).astype(BF16)
    for hp in range(2 * PEER_HEADS):
        st_out[0, hp] = lax.dot_general(sk_ref[hp], q[:, hp * PEER_HALF:(hp + 1) * PEER_HALF],
                                        (((1,), (1,)), ((), ())), preferred_element_type=F32)


def peer_scores(h2, wq, subkeys):
    bsz, s, d = h2.shape
    tm = TOKEN_BLOCK
    nhp = 2 * PEER_HEADS
    return pl.pallas_call(
        _peer_score_kernel,
        out_shape=jax.ShapeDtypeStruct((bsz, nhp, N_KEYS, s), F32),
        grid=(bsz, s // tm),
        in_specs=[pl.BlockSpec((1, tm, d), lambda b, j: (b, j, 0)),
                  pl.BlockSpec(wq.shape, lambda b, j: (0, 0)),
                  pl.BlockSpec(subkeys.shape, lambda b, j: (0, 0, 0))],
        out_specs=pl.BlockSpec((1, nhp, N_KEYS, tm), lambda b, j: (b, 0, 0, j)),
        compiler_params=_params(("parallel", "parallel")),
        name="peer_scores",
    )(h2, wq, subkeys)


def _sorted_top(s, k, with_rank):
    out = []
    rank = jnp.full(s.shape, float(k), F32) if with_rank else None
    for r in range(k):
        m = jnp.max(s, axis=0, keepdims=True)
        out.append(m)
        hit = s == m
        if with_rank:
            rank = jnp.where(hit, float(r), rank)
        s = jnp.where(hit, NEG_BIG, s)
    return out, rank


def _paired_bf16_words(x):
    w = pltpu.bitcast(x.astype(BF16).astype(F32), jnp.uint32)
    return w | (w >> 16)


def _peer_topk_kernel(st_ref, cnt_out, e0_out, rank_out, e1_out):
    nt = PEER_TOPK + 1
    tops = ([], [])
    for hd in range(PEER_HEADS):
        a, _ = _sorted_top(st_ref[0, 2 * hd], nt, False)
        b, rank = _sorted_top(st_ref[0, 2 * hd + 1], nt, True)
        rank_out[0, hd] = rank.astype(BF16)
        tops[0].append(a)
        tops[1].append(b)
    a8 = [jnp.concatenate([tops[0][hd][r] for hd in range(PEER_HEADS)], axis=0) for r in range(nt)]
    b8 = [jnp.concatenate([tops[1][hd][r] for hd in range(PEER_HEADS)], axis=0) for r in range(nt)]
    cand = [a8[i] + b8[j] for i in range(nt) for j in range(nt) if (i + 1) * (j + 1) <= nt]
    top = []
    for _ in range(nt):
        m = functools.reduce(jnp.maximum, cand)
        top.append(m)
        cand = [jnp.where(t == m, NEG_BIG, t) for t in cand]
    tau8 = 0.5 * (top[PEER_TOPK - 1] + top[PEER_TOPK])
    smax = a8[0] + b8[0]
    z8 = jnp.exp(top[0] - smax)
    for t in top[1:PEER_TOPK]:
        z8 = z8 + jnp.exp(t - smax)
    rz8 = 1.0 / z8
    for hd in range(PEER_HEADS):
        s0 = st_ref[0, 2 * hd]
        s1 = st_ref[0, 2 * hd + 1]
        thr = tau8[hd:hd + 1] - s0
        cnt = jnp.zeros(s0.shape, F32)
        for r in range(nt):
            cnt = cnt + jnp.where(tops[1][hd][r] > thr, 1.0, 0.0)
        cnt_out[0, hd] = _paired_bf16_words(cnt)
        e0_out[0, hd] = _paired_bf16_words(jnp.exp(s0 - tops[0][hd][0]))
        e1_out[0, hd] = (jnp.exp(s1 - tops[1][hd][0]) * (rz8[hd:hd + 1] * GELU_GATE_SCALE)).astype(BF16)


def peer_topk(st):
    bsz, nhp, nk, s = st.shape
    tl = LANES
    spec = pl.BlockSpec((1, PEER_HEADS, nk, tl), lambda b, j: (b, 0, 0, j))
    words = jax.ShapeDtypeStruct((bsz, PEER_HEADS, nk, s), jnp.uint32)
    halfs = jax.ShapeDtypeStruct((bsz, PEER_HEADS, nk, s), BF16)
    return pl.pallas_call(
        _peer_topk_kernel,
        out_shape=(words, words, halfs, halfs),
        grid=(bsz, s // tl),
        in_specs=[pl.BlockSpec((1, nhp, nk, tl), lambda b, j: (b, 0, 0, j))],
        out_specs=(spec, spec, spec, spec),
        compiler_params=_params(("parallel", "parallel")),
        name="peer_topk",
    )(st)


def _peer_expert_kernel(x_ref, mod_ref, h2_ref, u_ref, vt_ref, cnt_ref, e0_ref, rank_ref, e1_ref,
                        x_out, acc_sc, *, ctx_len):
    tok_block = pl.program_id(1)
    ec = pl.program_id(2)
    n_ec = pl.num_programs(2)
    d = x_ref.shape[-1]
    tb = x_ref.shape[1]
    pk = 2 * SUBLANES

    @pl.when(ec == 0)
    def _():
        acc_sc[...] = jnp.zeros(acc_sc.shape, F32)

    def row_tile(ref, hd, ii):
        return pltpu.bitcast(jnp.broadcast_to(ref[0, hd, ii:ii + 1, :], (SUBLANES, tb)), BF16)

    h2 = h2_ref[0]
    pair = 2 * N_KEYS
    n_pairs = PEER_I_PER_STEP // 2

    def activations(p):
        return lax.dot_general(u_ref[p * pair:(p + 1) * pair, :], h2, (((1,), (1,)), ((), ())),
                               preferred_element_type=F32)

    def gate_weights(p):
        tiles = []
        for ii in (2 * p, 2 * p + 1):
            wt = [jnp.zeros((pk, tb), BF16) for _ in range(N_KEYS // pk)]
            for hd in range(PEER_HEADS):
                cnt = row_tile(cnt_ref, hd, ii)
                e0 = row_tile(e0_ref, hd, ii)
                for rt in range(N_KEYS // pk):
                    rows = slice(rt * pk, (rt + 1) * pk)
                    sel = jnp.where(rank_ref[0, hd, rows, :] < cnt, e1_ref[0, hd, rows, :],
                                    jnp.zeros((), BF16))
                    wt[rt] = wt[rt] + sel * e0
            tiles += wt
        return tiles

    def gated(at2, tiles):
        g = []
        for rt, wt in enumerate(tiles):
            a = at2[rt * pk:(rt + 1) * pk]
            act = a * (1.0 + lax.erf(a))
            g.append(act.astype(BF16) * wt)
        return jnp.concatenate(g, axis=0)

    wt_next = gate_weights(0)
    at_next = activations(0)
    out = []
    for p in range(n_pairs):
        at_cur, wt_cur = at_next, wt_next
        if p + 1 < n_pairs:
            at_next = activations(p + 1)
            wt_next = gate_weights(p + 1)
        out.append(jnp.dot(vt_ref[:, p * pair:(p + 1) * pair], gated(at_cur, wt_cur),
                           preferred_element_type=F32))
    while len(out) > 1:
        out = [out[i] + out[i + 1] for i in range(0, len(out), 2)]
    acc_sc[...] += out[0]

    @pl.when(ec == n_ec - 1)
    def _():
        tok = tok_block * tb + lax.broadcasted_iota(jnp.int32, (tb, 1), 0)
        g2 = jnp.where(tok < ctx_len, mod_ref[0, 0, :, 5 * d:6 * d], mod_ref[0, 1, :, 5 * d:6 * d])
        x_out[0] = x_ref[0] + g2 * acc_sc[...].T


def peer_experts(xs, mod, h2, u_tab, vt_tab, cnt, e0, rank, e1, ctx_len):
    bsz, s, d = xs.shape
    tb = PEER_TOKEN_BLOCK if s % PEER_TOKEN_BLOCK == 0 else TOKEN_BLOCK
    ech = PEER_I_PER_STEP * N_KEYS
    n_exp = u_tab.shape[0]
    tok = lambda b, j, e: (b, j, 0)
    per_tok = lambda b, j, e: (b, 0, 0, j)
    per_i = lambda b, j, e: (b, 0, e, j)
    return pl.pallas_call(
        functools.partial(_peer_expert_kernel, ctx_len=ctx_len),
        out_shape=jax.ShapeDtypeStruct((bsz, s, d), F32),
        grid=(bsz, s // tb, n_exp // ech),
        in_specs=[pl.BlockSpec((1, tb, d), tok),
                  pl.BlockSpec((1, 2, 1, mod.shape[-1]), lambda b, j, e: (b, 0, 0, 0)),
                  pl.BlockSpec((1, tb, d), tok),
                  pl.BlockSpec((ech, d), lambda b, j, e: (e, 0)),
                  pl.BlockSpec((d, ech), lambda b, j, e: (0, e)),
                  pl.BlockSpec((1, PEER_HEADS, PEER_I_PER_STEP, tb), per_i),
                  pl.BlockSpec((1, PEER_HEADS, PEER_I_PER_STEP, tb), per_i),
                  pl.BlockSpec((1, PEER_HEADS, N_KEYS, tb), per_tok),
                  pl.BlockSpec((1, PEER_HEADS, N_KEYS, tb), per_tok)],
        out_specs=pl.BlockSpec((1, tb, d), tok),
        scratch_shapes=[pltpu.VMEM((d, tb), F32)],
        compiler_params=_params(("parallel", "parallel", "arbitrary")),
        name="peer_experts",
    )(xs, mod, h2, u_tab, vt_tab, cnt, e0, rank, e1)


def _rope_tables(ctx_len, n_lat):
    rows = n_lat // GRID_W
    row = jnp.repeat(jnp.arange(rows, dtype=F32), GRID_W)
    col = jnp.tile(jnp.arange(GRID_W, dtype=F32), rows)
    axis_dim = HEAD_DIM // 2
    inv_freq = ROPE_THETA ** (-jnp.arange(0, axis_dim, 2, dtype=F32) / axis_dim)
    ang_r = row[:, None] * inv_freq[None, :]
    ang_c = col[:, None] * inv_freq[None, :]
    ang = jnp.concatenate([ang_r, ang_r, ang_c, ang_c], axis=-1)
    cos, sin = jnp.cos(ang), jnp.sin(ang)
    first = (jnp.arange(HEAD_DIM) % (HEAD_DIM // 2)) < (HEAD_DIM // 4)
    sina = jnp.where(first, -sin, 0.0)
    sinb = jnp.where(first, 0.0, sin)
    pad = lambda t, v: jnp.concatenate([jnp.full((ctx_len, HEAD_DIM), v, F32), t], axis=0)
    return pad(cos, 1.0), pad(sina, 0.0), pad(sinb, 0.0)


def kernel(x, c, ctx, c_ctx, ada_w, ada_b, norm1_w, norm2_w, w_in, attn_qnorm_w, attn_knorm_w, dn_conv_w,
           dn_A_log, dn_dt_bias, dn_norm_w, w_out, peer_wq, peer_subkeys, peer_u, peer_v):
    bsz, n_lat, d = x.shape
    ctx_len = ctx.shape[1]
    depth = ada_w.shape[0]
    assert ctx_len == TOKEN_BLOCK and n_lat % TOKEN_BLOCK == 0 and bsz + 1 <= SUBLANES
    assert w_in.shape[-1] == IN_MAIN_W + N_GATE_COLS

    xs = jnp.concatenate([ctx, x], axis=1)
    cos, sina, sinb = _rope_tables(ctx_len, n_lat)

    cc = jnp.zeros((SUBLANES, d), F32).at[:bsz].set(c).at[bsz].set(c_ctx)
    mod_all = ada_modulation(cc, ada_w, ada_b)
    mod_ctx = jnp.broadcast_to(mod_all[:, bsz][:, None], (depth, bsz, 6 * d))
    mod = jnp.stack([mod_ctx, mod_all[:, :bsz]], axis=2)[:, :, :, None, :]

    for l in range(depth):
        w_main = w_in[l, :, :IN_MAIN_W].astype(BF16)
        w_gate = jnp.pad(w_in[l, :, IN_MAIN_W:], ((0, 0), (0, LANES - N_GATE_COLS))).astype(BF16)
        q, k, v, dnqkv, gate, ba = in_projection(xs, mod[l], norm1_w[l], w_main, w_gate, cos, sina, sinb,
                                                 attn_qnorm_w[l], attn_knorm_w[l])
        attn = attention(q, k, v, ctx_len)
        conv_w = jnp.pad(dn_conv_w[l], ((0, SUBLANES - CONV_K), (0, 0)))
        gate_par = jnp.zeros((SUBLANES, LANES), F32)
        gate_par = gate_par.at[0, 2 * DN_HEADS:4 * DN_HEADS].set(dn_A_log[l].reshape(-1))
        gate_par = gate_par.at[1, 2 * DN_HEADS:4 * DN_HEADS].set(dn_dt_bias[l].reshape(-1))
        o_f, o_b = dn_scan(*dn_prepare(dnqkv, ba, conv_w, gate_par, ctx_len), ctx_len)
        xs, h2 = out_projection(xs, mod[l], attn, o_f, o_b, gate, dn_norm_w[l], w_out[l].astype(BF16),
                                norm2_w[l])
        sk = peer_subkeys[l].reshape(2 * PEER_HEADS, N_KEYS, PEER_HALF).astype(BF16)
        st = peer_scores(h2, peer_wq[l].astype(BF16), sk)
        cnt, e0, rank, e1 = peer_topk(st)
        xs = peer_experts(xs, mod[l], h2, (peer_u[l] * GELU_GATE_SCALE).astype(BF16), peer_v[l].T.astype(BF16), cnt, e0, rank, e1,
                          ctx_len)
    return xs[:, ctx_len:]
```

```python
import functools
import math

import jax
import jax.numpy as jnp
from jax import lax
from jax.experimental import pallas as pl
from jax.experimental.pallas import tpu as pltpu

F32 = jnp.float32
BF16 = jnp.bfloat16
HIGHEST = lax.Precision.HIGHEST

HEAD_DIM = 128
ATTN_HEADS = 4
ATTN_KV_HEADS = 2
ATTN_GROUP = ATTN_HEADS // ATTN_KV_HEADS
DN_HEADS = 4
ATTN_Q_W = ATTN_HEADS * HEAD_DIM
ATTN_KV_W = ATTN_KV_HEADS * HEAD_DIM
DN_W = DN_HEADS * HEAD_DIM
IN_MAIN_W = ATTN_Q_W + 2 * ATTN_KV_W + 3 * DN_W + DN_W
N_GATE_COLS = 4 * DN_HEADS
ROPE_THETA = 10000.0
GRID_W = 64
CONV_K = 5
CONV_PAD = CONV_K // 2
PEER_HEADS = 8
PEER_HALF = 128
N_KEYS = 128
PEER_TOPK = 16
EPS = 1e-6
NEG_BIG = -1e30
GELU_GATE_SCALE = 2.0 ** -0.5

LANES = 128
SUBLANES = 8
MXU_TILE = 256
TOKEN_BLOCK = 256
DN_CHUNK = 128
DN_BASE_BLOCK = 4
ATTN_KEY_BLOCK = 1280
ATTN_Q_BLOCK = 1024
ATTN_CHAIN_ROWS = 256
PEER_TOKEN_BLOCK = 1280
PEER_I_PER_STEP = 8
VMEM_LIMIT = 56 * 1024 * 1024


def _params(sem):
    return pltpu.CompilerParams(dimension_semantics=sem, vmem_limit_bytes=VMEM_LIMIT)


def _bf16_dot(a, b):
    return jnp.dot(a.astype(BF16), b.astype(BF16), preferred_element_type=F32)


def _split_dot(a, b):
    a_hi = a.astype(BF16)
    b_hi = b.astype(BF16)
    a_lo = (a - a_hi.astype(F32)).astype(BF16)
    b_lo = (b - b_hi.astype(F32)).astype(BF16)
    dot = functools.partial(jnp.dot, preferred_element_type=F32)
    return dot(a_hi, b_hi) + (dot(a_hi, b_lo) + dot(a_lo, b_hi))


def _bf16_dot_nt(a, b):
    return lax.dot_general(a.astype(BF16), b.astype(BF16), (((1,), (1,)), ((), ())),
                           preferred_element_type=F32)


def _ada_kernel(c_ref, w_ref, b_ref, o_ref):
    c = c_ref[...]
    a = c * jax.nn.sigmoid(c)
    o_ref[0] = jnp.dot(a, w_ref[0], preferred_element_type=F32, precision=HIGHEST) + b_ref[0]


def ada_modulation(cc, ada_w, ada_b):
    depth, d, n = ada_w.shape
    tn = 1536
    return pl.pallas_call(
        _ada_kernel,
        out_shape=jax.ShapeDtypeStruct((depth, SUBLANES, n), F32),
        grid=(depth, n // tn),
        in_specs=[pl.BlockSpec((SUBLANES, d), lambda l, j: (0, 0)),
                  pl.BlockSpec((1, d, tn), lambda l, j: (l, 0, j)),
                  pl.BlockSpec((1, 1, tn), lambda l, j: (l, 0, j))],
        out_specs=pl.BlockSpec((1, SUBLANES, tn), lambda l, j: (l, 0, j)),
        compiler_params=_params(("parallel", "parallel")),
        name="ada_modulation",
    )(cc, ada_w, ada_b.reshape(depth, 1, n))


def _mod_index(b, j):
    return (b, jnp.minimum(j, 1), 0, 0)


def _modulated_norm(x, nw, shift, scale):
    ms = jnp.mean(x * x, axis=-1, keepdims=True)
    y = x * lax.rsqrt(ms + EPS) * nw
    return y * (1.0 + scale) + shift


def _head_rmsnorm(x, w):
    return x * lax.rsqrt(jnp.mean(x * x, axis=-1, keepdims=True) + EPS) * w


def _inproj_kernel(x_ref, mod_ref, nw_ref, wm_ref, wg_ref, cos_ref, sina_ref, sinb_ref, qnw_ref, knw_ref,
                   q_out, k_out, v_out, dn_out, gate_out, ba_out):
    d = x_ref.shape[-1]
    x = x_ref[0]
    h = _modulated_norm(x, nw_ref[...], mod_ref[0, 0, :, 0:d], mod_ref[0, 0, :, d:2 * d]).astype(BF16)
    p = jnp.dot(h, wm_ref[...], preferred_element_type=F32)
    ba_out[0] = jnp.dot(h, wg_ref[...], preferred_element_type=F32)
    cos = cos_ref[...]
    sina = sina_ref[...]
    sinb = sinb_ref[...]

    def rope(t):
        return (t * cos + pltpu.roll(t, HEAD_DIM - HEAD_DIM // 4, 1) * sina
                + pltpu.roll(t, HEAD_DIM // 4, 1) * sinb)

    scale = HEAD_DIM ** -0.5 * math.log2(math.e)
    for hd in range(ATTN_HEADS):
        qh = _head_rmsnorm(p[:, hd * HEAD_DIM:(hd + 1) * HEAD_DIM], qnw_ref[...])
        q_out[0, hd] = (rope(qh) * scale).astype(BF16)
    off = ATTN_Q_W
    for hd in range(ATTN_KV_HEADS):
        kh = _head_rmsnorm(p[:, off + hd * HEAD_DIM:off + (hd + 1) * HEAD_DIM], knw_ref[...])
        k_out[0, hd] = rope(kh).astype(BF16)
    off += ATTN_KV_W
    for hd in range(ATTN_KV_HEADS):
        v_out[0, hd] = p[:, off + hd * HEAD_DIM:off + (hd + 1) * HEAD_DIM].astype(BF16)
    off += ATTN_KV_W
    dn_out[0] = p[:, off:off + 3 * DN_W]
    off += 3 * DN_W
    gate_out[0] = p[:, off:off + DN_W]


def in_projection(xs, mod, norm_w, w_main, w_gate, cos, sina, sinb, qn_w, kn_w):
    bsz, s, d = xs.shape
    tm = TOKEN_BLOCK
    tok = lambda b, j: (b, j, 0)
    head_tok = lambda b, j: (b, 0, j, 0)
    full2 = lambda b, j: (0, 0)
    rope_spec = pl.BlockSpec((tm, HEAD_DIM), lambda b, j: (j, 0))
    return pl.pallas_call(
        _inproj_kernel,
        out_shape=(jax.ShapeDtypeStruct((bsz, ATTN_HEADS, s, HEAD_DIM), BF16),
                   jax.ShapeDtypeStruct((bsz, ATTN_KV_HEADS, s, HEAD_DIM), BF16),
                   jax.ShapeDtypeStruct((bsz, ATTN_KV_HEADS, s, HEAD_DIM), BF16),
                   jax.ShapeDtypeStruct((bsz, s, 3 * DN_W), F32),
                   jax.ShapeDtypeStruct((bsz, s, DN_W), F32),
                   jax.ShapeDtypeStruct((bsz, s, LANES), F32)),
        grid=(bsz, s // tm),
        in_specs=[pl.BlockSpec((1, tm, d), tok),
                  pl.BlockSpec((1, 1, 1, mod.shape[-1]), _mod_index),
                  pl.BlockSpec((1, d), full2),
                  pl.BlockSpec((d, IN_MAIN_W), full2),
                  pl.BlockSpec((d, LANES), full2),
                  rope_spec, rope_spec, rope_spec,
                  pl.BlockSpec((1, HEAD_DIM), full2),
                  pl.BlockSpec((1, HEAD_DIM), full2)],
        out_specs=(pl.BlockSpec((1, ATTN_HEADS, tm, HEAD_DIM), head_tok),
                   pl.BlockSpec((1, ATTN_KV_HEADS, tm, HEAD_DIM), head_tok),
                   pl.BlockSpec((1, ATTN_KV_HEADS, tm, HEAD_DIM), head_tok),
                   pl.BlockSpec((1, tm, 3 * DN_W), tok),
                   pl.BlockSpec((1, tm, DN_W), tok),
                   pl.BlockSpec((1, tm, LANES), tok)),
        compiler_params=_params(("parallel", "parallel")),
        name="in_projection",
    )(xs, mod, norm_w.reshape(1, d), w_main, w_gate, cos, sina, sinb,
      qn_w.reshape(1, HEAD_DIM), kn_w.reshape(1, HEAD_DIM))


def _attn_kernel(q_ref, k_ref, v_ref, o_ref, m_sc, l_sc, acc_sc):
    ki = pl.program_id(2)
    nk = pl.num_programs(2)

    @pl.when(ki == 0)
    def _():
        m_sc[...] = jnp.full(m_sc.shape, NEG_BIG, F32)
        l_sc[...] = jnp.zeros(l_sc.shape, F32)
        acc_sc[...] = jnp.zeros(acc_sc.shape, F32)

    tq = q_ref.shape[2]
    chains = [(hd, pl.ds(r0, ATTN_CHAIN_ROWS)) for hd in range(ATTN_HEADS)
              for r0 in range(0, tq, ATTN_CHAIN_ROWS)]
    scores = [lax.dot_general(q_ref[0, hd, rows, :], k_ref[0, hd // ATTN_GROUP], (((1,), (1,)), ((), ())),
                              preferred_element_type=F32) for hd, rows in chains]
    for (hd, rows), s in zip(chains, scores):
        m_prev = m_sc[hd, rows, :]
        m_new = jnp.maximum(m_prev, jnp.max(s, axis=-1, keepdims=True))
        p = jnp.exp2(s - m_new)
        alpha = jnp.exp2(m_prev - m_new)
        l_sc[hd, rows, :] = alpha * l_sc[hd, rows, :] + jnp.sum(p, axis=-1, keepdims=True)
        acc_sc[hd, rows, :] = alpha * acc_sc[hd, rows, :] + jnp.dot(
            p.astype(BF16), v_ref[0, hd // ATTN_GROUP], preferred_element_type=F32)
        m_sc[hd, rows, :] = m_new

    @pl.when(ki == nk - 1)
    def _():
        for hd in range(ATTN_HEADS):
            o_ref[0, :, hd * HEAD_DIM:(hd + 1) * HEAD_DIM] = (acc_sc[hd] / l_sc[hd]).astype(o_ref.dtype)


def _attention_call(q, k, v, tq, n_keys, tk, name):
    bsz = q.shape[0]
    n_q_blocks = q.shape[2] // tq
    return pl.pallas_call(
        _attn_kernel,
        out_shape=jax.ShapeDtypeStruct((bsz, n_q_blocks * tq, ATTN_Q_W), BF16),
        grid=(bsz, n_q_blocks, n_keys // tk),
        in_specs=[pl.BlockSpec((1, ATTN_HEADS, tq, HEAD_DIM), lambda b, i, j: (b, 0, i, 0)),
                  pl.BlockSpec((1, ATTN_KV_HEADS, tk, HEAD_DIM), lambda b, i, j: (b, 0, j, 0)),
                  pl.BlockSpec((1, ATTN_KV_HEADS, tk, HEAD_DIM), lambda b, i, j: (b, 0, j, 0))],
        out_specs=pl.BlockSpec((1, tq, ATTN_Q_W), lambda b, i, j: (b, i, 0)),
        scratch_shapes=[pltpu.VMEM((ATTN_HEADS, tq, 1), F32),
                        pltpu.VMEM((ATTN_HEADS, tq, 1), F32),
                        pltpu.VMEM((ATTN_HEADS, tq, HEAD_DIM), F32)],
        compiler_params=_params(("parallel", "parallel", "arbitrary")),
        name=name,
    )(q, k, v)


def attention(q, k, v, ctx_len):
    s = q.shape[2]
    assert ctx_len % ATTN_CHAIN_ROWS == 0 and (s - ctx_len) % ATTN_CHAIN_ROWS == 0
    tk = ATTN_KEY_BLOCK if s % ATTN_KEY_BLOCK == 0 else ATTN_CHAIN_ROWS
    tq = ATTN_Q_BLOCK if (s - ctx_len) % ATTN_Q_BLOCK == 0 else ATTN_CHAIN_ROWS
    attn_ctx = _attention_call(q[:, :, :ctx_len], k, v, ctx_len, ctx_len, ctx_len, "attention_ctx")
    attn_lat = _attention_call(q[:, :, ctx_len:], k, v, tq, s, tk, "attention")
    return jnp.concatenate([attn_ctx, attn_lat], axis=1)


def _dn_prep_kernel(main_ref, prev_ref, next_ref, ba_ref, cw_ref, gp_ref,
                    w_out, u_out, qg_out, kdt_out, qk_out, dl_out, ext_sc, *, ctx_chunks):
    j = pl.program_id(1)
    nj = pl.num_programs(1)
    c = DN_CHUNK
    has_prev = (j != 0) & (j != ctx_chunks)
    has_next = (j != ctx_chunks - 1) & (j != nj - 1)
    ext_sc[0:SUBLANES] = jnp.where(has_prev, prev_ref[0], 0.0)
    ext_sc[SUBLANES:SUBLANES + c] = main_ref[0]
    ext_sc[SUBLANES + c:2 * SUBLANES + c] = jnp.where(has_next, next_ref[0], 0.0)
    y = ext_sc[SUBLANES - CONV_PAD:SUBLANES - CONV_PAD + c] * cw_ref[0:1]
    for t in range(1, CONV_K):
        y = y + ext_sc[SUBLANES - CONV_PAD + t:SUBLANES - CONV_PAD + t + c] * cw_ref[t:t + 1]
    y = y * jax.nn.sigmoid(y)

    ba = ba_ref[0]
    beta_all = jax.nn.sigmoid(ba)
    g_all = -jnp.exp(gp_ref[0:1]) * jax.nn.softplus(ba + gp_ref[1:2])
    row = lax.broadcasted_iota(jnp.int32, (c, c), 0)
    col = lax.broadcasted_iota(jnp.int32, (c, c), 1)
    lower = (row >= col).astype(BF16)
    g_hi = g_all.astype(BF16)
    g_r1 = g_all - g_hi.astype(F32)
    g_mid = g_r1.astype(BF16)
    g_lo = (g_r1 - g_mid.astype(F32)).astype(BF16)
    dotf = functools.partial(jnp.dot, preferred_element_type=F32)
    prefix = dotf(lower, g_hi) + (dotf(lower, g_mid) + dotf(lower, g_lo))
    total = prefix[c - 1:c]
    gc = (prefix, total - prefix + g_all)
    gct = (gc[0].T, gc[1].T)
    row2 = lax.broadcasted_iota(jnp.int32, (2 * c, 2 * c), 0)
    col2 = lax.broadcasted_iota(jnp.int32, (2 * c, 2 * c), 1)
    eye2 = (row2 == col2).astype(F32)
    zero = jnp.zeros((c, c), F32)

    heads = range(DN_HEADS)
    a2, rhs2 = [], []
    for hd in heads:
        q = y[:, hd * HEAD_DIM:(hd + 1) * HEAD_DIM]
        k = y[:, DN_W + hd * HEAD_DIM:DN_W + (hd + 1) * HEAD_DIM]
        v = y[:, 2 * DN_W + hd * HEAD_DIM:2 * DN_W + (hd + 1) * HEAD_DIM]
        q = q * lax.rsqrt(jnp.sum(q * q, axis=-1, keepdims=True) + EPS) * (HEAD_DIM ** -0.5)
        k = k * lax.rsqrt(jnp.sum(k * k, axis=-1, keepdims=True) + EPS)
        kk = _bf16_dot_nt(k, k)
        qk = _bf16_dot_nt(q, k)
        a_dir, rhs_dir = [], []
        for dr in range(2):
            cb = dr * DN_HEADS + hd
            cg = 2 * DN_HEADS + cb
            beta = beta_all[:, cb:cb + 1]
            gcol = gc[dr][:, cg:cg + 1]
            grow = gct[dr][cg:cg + 1, :]
            incl = (row >= col) if dr == 0 else (row <= col)
            strict = (row > col) if dr == 0 else (row < col)
            decay = jnp.exp(jnp.where(incl, gcol - grow, NEG_BIG))
            a_dir.append(jnp.where(strict, kk * beta * decay, 0.0))
            eg = jnp.exp(gcol)
            rhs_dir.append(jnp.concatenate([v * beta, k * (beta * eg)], axis=1))
            qg_out[0, dr, hd] = (q * eg).astype(BF16)
            tot = total[:, cg:cg + 1]
            kdt_out[0, dr, hd] = (k * jnp.exp(tot - gcol)).T.astype(BF16)
            qk_out[0, dr, hd] = (qk * decay).astype(BF16)
            dl_out[0, dr, hd, 0] = jnp.broadcast_to(jnp.exp(tot), (1, LANES))
        a2.append(jnp.concatenate([jnp.concatenate([a_dir[0], zero], axis=1),
                                   jnp.concatenate([zero, a_dir[1]], axis=1)], axis=0))
        rhs2.append(jnp.concatenate(rhs_dir, axis=0))
    base = DN_BASE_BLOCK
    base_mask = (row2 // base) == (col2 // base)
    a_base = [jnp.where(base_mask, a2[hd], 0.0) for hd in heads]
    x = [eye2 - a_base[hd] for hd in heads]
    pw = [_split_dot(a_base[hd], a_base[hd]) for hd in heads]
    for it in range(int(math.log2(base)) - 1):
        x = [x[hd] + _split_dot(x[hd], pw[hd]) for hd in heads]
        if it < int(math.log2(base)) - 2:
            pw = [_split_dot(pw[hd], pw[hd]) for hd in heads]
    blk = base
    while blk < c:
        sibling = ((row2 // (2 * blk)) == (col2 // (2 * blk))) & ((row2 // blk) != (col2 // blk))
        fold = [_bf16_dot(x[hd], jnp.where(sibling, a2[hd], 0.0)) for hd in heads]
        x = [x[hd] - _bf16_dot(fold[hd], x[hd]) for hd in heads]
        blk *= 2
    for hd in heads:
        uw = _bf16_dot(x[hd], rhs2[hd])
        for dr in range(2):
            u_out[0, dr, hd] = uw[dr * c:(dr + 1) * c, 0:HEAD_DIM]
            w_out[0, dr, hd] = uw[dr * c:(dr + 1) * c, HEAD_DIM:2 * HEAD_DIM].astype(BF16)


def dn_prepare(dnqkv, ba, conv_w, gate_par, ctx_len):
    bsz, s, wdt = dnqkv.shape
    c = DN_CHUNK
    nc = s // c
    rows8 = s // SUBLANES
    per = c // SUBLANES
    chain = lambda b, j: (b, 0, 0, j, 0)
    return pl.pallas_call(
        functools.partial(_dn_prep_kernel, ctx_chunks=ctx_len // c),
        out_shape=(jax.ShapeDtypeStruct((bsz, 2, DN_HEADS, s, HEAD_DIM), BF16),
                   jax.ShapeDtypeStruct((bsz, 2, DN_HEADS, s, HEAD_DIM), F32),
                   jax.ShapeDtypeStruct((bsz, 2, DN_HEADS, s, HEAD_DIM), BF16),
                   jax.ShapeDtypeStruct((bsz, 2, DN_HEADS, HEAD_DIM, s), BF16),
                   jax.ShapeDtypeStruct((bsz, 2, DN_HEADS, s, c), BF16),
                   jax.ShapeDtypeStruct((bsz, 2, DN_HEADS, nc, 1, LANES), F32)),
        grid=(bsz, nc),
        in_specs=[pl.BlockSpec((1, c, wdt), lambda b, j: (b, j, 0)),
                  pl.BlockSpec((1, SUBLANES, wdt), lambda b, j: (b, jnp.maximum(j * per - 1, 0), 0)),
                  pl.BlockSpec((1, SUBLANES, wdt), lambda b, j: (b, jnp.minimum((j + 1) * per, rows8 - 1), 0)),
                  pl.BlockSpec((1, c, LANES), lambda b, j: (b, j, 0)),
                  pl.BlockSpec((SUBLANES, wdt), lambda b, j: (0, 0)),
                  pl.BlockSpec((SUBLANES, LANES), lambda b, j: (0, 0))],
        out_specs=(pl.BlockSpec((1, 2, DN_HEADS, c, HEAD_DIM), chain),
                   pl.BlockSpec((1, 2, DN_HEADS, c, HEAD_DIM), chain),
                   pl.BlockSpec((1, 2, DN_HEADS, c, HEAD_DIM), chain),
                   pl.BlockSpec((1, 2, DN_HEADS, HEAD_DIM, c), lambda b, j: (b, 0, 0, 0, j)),
                   pl.BlockSpec((1, 2, DN_HEADS, c, c), chain),
                   pl.BlockSpec((1, 2, DN_HEADS, 1, 1, LANES), lambda b, j: (b, 0, 0, j, 0, 0))),
        scratch_shapes=[pltpu.VMEM((c + 2 * SUBLANES, wdt), F32)],
        compiler_params=_params(("parallel", "parallel")),
        name="dn_prepare",
    )(dnqkv, dnqkv, dnqkv, ba, conv_w, gate_par)


def _dn_scan_kernel(*refs, bsz):
    ins = refs[:12]
    of_ref, ob_ref, s_sc = refs[12:]
    n = pl.program_id(0)

    @pl.when(n == 0)
    def _():
        s_sc[...] = jnp.zeros(s_sc.shape, F32)

    chains = [(dr, b, hd) for dr in range(2) for b in range(bsz) for hd in range(DN_HEADS)]
    dotf = functools.partial(jnp.dot, preferred_element_type=F32)

    def inp(dr, k):
        return ins[dr * 6 + k]

    state = [s_sc[ci] for ci in range(len(chains))]
    r = [dotf(jnp.concatenate([inp(dr, 0)[b, 0, hd], inp(dr, 2)[b, 0, hd]], axis=0), state[ci].astype(BF16))
         for ci, (dr, b, hd) in enumerate(chains)]
    v_new = [(inp(dr, 1)[b, 0, hd] - r[ci][0:DN_CHUNK]).astype(BF16) for ci, (dr, b, hd) in enumerate(chains)]
    intra = [dotf(inp(dr, 4)[b, 0, hd], v_new[ci]) for ci, (dr, b, hd) in enumerate(chains)]
    upd = [dotf(inp(dr, 3)[b, 0, hd], v_new[ci]) for ci, (dr, b, hd) in enumerate(chains)]
    for ci, (dr, b, hd) in enumerate(chains):
        o_ref = of_ref if dr == 0 else ob_ref
        o_ref[b, :, hd * HEAD_DIM:(hd + 1) * HEAD_DIM] = r[ci][DN_CHUNK:] + intra[ci]
        s_sc[ci] = state[ci] * inp(dr, 5)[b, 0, hd, 0] + upd[ci]


def dn_scan(w, u, qg, kdt, qk, dl, ctx_len):
    bsz, _, _, s, _ = w.shape
    c = DN_CHUNK
    nc = s // c
    cc = ctx_len // c

    def bwd_chunk(n):
        return jnp.where(n < cc, cc - 1 - n, nc - 1 - (n - cc))

    in_specs, args = [], []
    for dr in range(2):
        pos = (lambda n: n) if dr == 0 else bwd_chunk
        tokm = lambda n, dr=dr, pos=pos: (0, dr, 0, pos(n), 0)
        for arr in (w, u, qg):
            in_specs.append(pl.BlockSpec((bsz, 1, DN_HEADS, c, HEAD_DIM), tokm))
            args.append(arr)
        in_specs.append(pl.BlockSpec((bsz, 1, DN_HEADS, HEAD_DIM, c), lambda n, dr=dr, pos=pos: (0, dr, 0, 0, pos(n))))
        args.append(kdt)
        in_specs.append(pl.BlockSpec((bsz, 1, DN_HEADS, c, c), tokm))
        args.append(qk)
        in_specs.append(pl.BlockSpec((bsz, 1, DN_HEADS, 1, 1, LANES), lambda n, dr=dr, pos=pos: (0, dr, 0, pos(n), 0, 0)))
        args.append(dl)
    return pl.pallas_call(
        functools.partial(_dn_scan_kernel, bsz=bsz),
        out_shape=(jax.ShapeDtypeStruct((bsz, s, DN_W), F32), jax.ShapeDtypeStruct((bsz, s, DN_W), F32)),
        grid=(nc,),
        in_specs=in_specs,
        out_specs=(pl.BlockSpec((bsz, c, DN_W), lambda n: (0, n, 0)),
                   pl.BlockSpec((bsz, c, DN_W), lambda n: (0, bwd_chunk(n), 0))),
        scratch_shapes=[pltpu.VMEM((2 * bsz * DN_HEADS, HEAD_DIM, HEAD_DIM), F32)],
        compiler_params=_params(("arbitrary",)),
        name="dn_scan",
    )(*args)


def _outproj_kernel(x_ref, mod_ref, attn_ref, of_ref, ob_ref, gate_ref, dnw_ref, wo_ref, n2w_ref,
                    x_out, h2_out):
    d = x_ref.shape[-1]
    o = of_ref[0] + ob_ref[0]
    gate = gate_ref[0]
    parts = [attn_ref[0]]
    for hd in range(DN_HEADS):
        sl = slice(hd * HEAD_DIM, (hd + 1) * HEAD_DIM)
        g = gate[:, sl]
        parts.append((_head_rmsnorm(o[:, sl], dnw_ref[...]) * (g * jax.nn.sigmoid(g))).astype(BF16))
    mix = jnp.concatenate(parts, axis=1)
    y = jnp.dot(mix, wo_ref[...], preferred_element_type=F32)
    x = x_ref[0] + mod_ref[0, 0, :, 2 * d:3 * d] * y
    x_out[0] = x
    h2_out[0] = _modulated_norm(x, n2w_ref[...], mod_ref[0, 0, :, 3 * d:4 * d],
                                mod_ref[0, 0, :, 4 * d:5 * d]).astype(BF16)


def out_projection(xs, mod, attn, o_f, o_b, gate, dn_norm_w, w_out, norm2_w):
    bsz, s, d = xs.shape
    tm = TOKEN_BLOCK
    tok = lambda b, j: (b, j, 0)
    full2 = lambda b, j: (0, 0)
    return pl.pallas_call(
        _outproj_kernel,
        out_shape=(jax.ShapeDtypeStruct((bsz, s, d), F32), jax.ShapeDtypeStruct((bsz, s, d), BF16)),
        grid=(bsz, s // tm),
        in_specs=[pl.BlockSpec((1, tm, d), tok),
                  pl.BlockSpec((1, 1, 1, mod.shape[-1]), _mod_index),
                  pl.BlockSpec((1, tm, ATTN_Q_W), tok),
                  pl.BlockSpec((1, tm, DN_W), tok),
                  pl.BlockSpec((1, tm, DN_W), tok),
                  pl.BlockSpec((1, tm, DN_W), tok),
                  pl.BlockSpec((1, HEAD_DIM), full2),
                  pl.BlockSpec(w_out.shape, full2),
                  pl.BlockSpec((1, d), full2)],
        out_specs=(pl.BlockSpec((1, tm, d), tok), pl.BlockSpec((1, tm, d), tok)),
        compiler_params=_params(("parallel", "parallel")),
        name="out_projection",
    )(xs, mod, attn, o_f, o_b, gate, dn_norm_w.reshape(1, HEAD_DIM), w_out, norm2_w.reshape(1, d))


def _peer_score_kernel(h2_ref, wq_ref, sk_ref, st_out):
    q = jnp.dot(h2_ref[0], wq_ref[...], preferred_element_type=F32).astype(BF16)
    for hp in range(2 * PEER_HEADS):
        st_out[0, hp] = lax.dot_general(sk_ref[hp], q[:, hp * PEER_HALF:(hp + 1) * PEER_HALF],
                                        (((1,), (1,)), ((), ())), preferred_element_type=F32)


def peer_scores(h2, wq, subkeys):
    bsz, s, d = h2.shape
    tm = TOKEN_BLOCK
    nhp = 2 * PEER_HEADS
    return pl.pallas_call(
        _peer_score_kernel,
        out_shape=jax.ShapeDtypeStruct((bsz, nhp, N_KEYS, s), F32),
        grid=(bsz, s // tm),
        in_specs=[pl.BlockSpec((1, tm, d), lambda b, j: (b, j, 0)),
                  pl.BlockSpec(wq.shape, lambda b, j: (0, 0)),
                  pl.BlockSpec(subkeys.shape, lambda b, j: (0, 0, 0))],
        out_specs=pl.BlockSpec((1, nhp, N_KEYS, tm), lambda b, j: (b, 0, 0, j)),
        compiler_params=_params(("parallel", "parallel")),
        name="peer_scores",
    )(h2, wq, subkeys)


def _sorted_top(s, k, with_rank):
    out = []
    rank = jnp.full(s.shape, float(k), F32) if with_rank else None
    for r in range(k):
        m = jnp.max(s, axis=0, keepdims=True)
        out.append(m)
        hit = s == m
        if with_rank:
            rank = jnp.where(hit, float(r), rank)
        s = jnp.where(hit, NEG_BIG, s)
    return out, rank


def _paired_bf16_words(x):
    w = pltpu.bitcast(x.astype(BF16).astype(F32), jnp.uint32)
    return w | (w >> 16)


def _peer_topk_kernel(st_ref, cnt_out, e0_out, rank_out, e1_out):
    nt = PEER_TOPK + 1
    tops = ([], [])
    for hd in range(PEER_HEADS):
        a, _ = _sorted_top(st_ref[0, 2 * hd], nt, False)
        b, rank = _sorted_top(st_ref[0, 2 * hd + 1], nt, True)
        rank_out[0, hd] = rank.astype(BF16)
        tops[0].append(a)
        tops[1].append(b)
    a8 = [jnp.concatenate([tops[0][hd][r] for hd in range(PEER_HEADS)], axis=0) for r in range(nt)]
    b8 = [jnp.concatenate([tops[1][hd][r] for hd in range(PEER_HEADS)], axis=0) for r in range(nt)]
    cand = [a8[i] + b8[j] for i in range(nt) for j in range(nt) if (i + 1) * (j + 1) <= nt]
    top = []
    for _ in range(nt):
        m = functools.reduce(jnp.maximum, cand)
        top.append(m)
        cand = [jnp.where(t == m, NEG_BIG, t) for t in cand]
    tau8 = 0.5 * (top[PEER_TOPK - 1] + top[PEER_TOPK])
    smax = a8[0] + b8[0]
    z8 = jnp.exp(top[0] - smax)
    for t in top[1:PEER_TOPK]:
        z8 = z8 + jnp.exp(t - smax)
    rz8 = 1.0 / z8
    for hd in range(PEER_HEADS):
        s0 = st_ref[0, 2 * hd]
        s1 = st_ref[0, 2 * hd + 1]
        thr = tau8[hd:hd + 1] - s0
        cnt = jnp.zeros(s0.shape, F32)
        for r in range(nt):
            cnt = cnt + jnp.where(tops[1][hd][r] > thr, 1.0, 0.0)
        cnt_out[0, hd] = _paired_bf16_words(cnt)
        e0_out[0, hd] = _paired_bf16_words(jnp.exp(s0 - tops[0][hd][0]))
        e1_out[0, hd] = (jnp.exp(s1 - tops[1][hd][0]) * (rz8[hd:hd + 1] * GELU_GATE_SCALE)).astype(BF16)


def peer_topk(st):
    bsz, nhp, nk, s = st.shape
    tl = LANES
    spec = pl.BlockSpec((1, PEER_HEADS, nk, tl), lambda b, j: (b, 0, 0, j))
    words = jax.ShapeDtypeStruct((bsz, PEER_HEADS, nk, s), jnp.uint32)
    halfs = jax.ShapeDtypeStruct((bsz, PEER_HEADS, nk, s), BF16)
    return pl.pallas_call(
        _peer_topk_kernel,
        out_shape=(words, words, halfs, halfs),
        grid=(bsz, s // tl),
        in_specs=[pl.BlockSpec((1, nhp, nk, tl), lambda b, j: (b, 0, 0, j))],
        out_specs=(spec, spec, spec, spec),
        compiler_params=_params(("parallel", "parallel")),
        name="peer_topk",
    )(st)


def _peer_expert_kernel(x_ref, mod_ref, h2_ref, u_ref, vt_ref, cnt_ref, e0_ref, rank_ref, e1_ref,
                        x_out, acc_sc, *, ctx_len):
    tok_block = pl.program_id(1)
    ec = pl.program_id(2)
    n_ec = pl.num_programs(2)
    d = x_ref.shape[-1]
    tb = x_ref.shape[1]
    pk = 2 * SUBLANES

    @pl.when(ec == 0)
    def _():
        acc_sc[...] = jnp.zeros(acc_sc.shape, F32)

    def row_tile(ref, hd, ii):
        return pltpu.bitcast(jnp.broadcast_to(ref[0, hd, ii:ii + 1, :], (SUBLANES, tb)), BF16)

    h2 = h2_ref[0]
    pair = 2 * N_KEYS
    n_pairs = PEER_I_PER_STEP // 2

    def activations(p):
        return lax.dot_general(u_ref[p * pair:(p + 1) * pair, :], h2, (((1,), (1,)), ((), ())),
                               preferred_element_type=F32)

    def gate_weights(p):
        tiles = []
        for ii in (2 * p, 2 * p + 1):
            wt = [jnp.zeros((pk, tb), BF16) for _ in range(N_KEYS // pk)]
            for hd in range(PEER_HEADS):
                cnt = row_tile(cnt_ref, hd, ii)
                e0 = row_tile(e0_ref, hd, ii)
                for rt in range(N_KEYS // pk):
                    rows = slice(rt * pk, (rt + 1) * pk)
                    sel = jnp.where(rank_ref[0, hd, rows, :] < cnt, e1_ref[0, hd, rows, :],
                                    jnp.zeros((), BF16))
                    wt[rt] = wt[rt] + sel * e0
            tiles += wt
        return tiles

    def gated(at2, tiles):
        g = []
        for rt, wt in enumerate(tiles):
            a = at2[rt * pk:(rt + 1) * pk]
            act = a * (1.0 + lax.erf(a))
            g.append(act.astype(BF16) * wt)
        return jnp.concatenate(g, axis=0)

    g = [gated(activations(p), gate_weights(p)) for p in range(n_pairs)]
    acc_sc[...] += jnp.dot(vt_ref[...], jnp.concatenate(g, axis=0), preferred_element_type=F32)

    @pl.when(ec == n_ec - 1)
    def _():
        tok = tok_block * tb + lax.broadcasted_iota(jnp.int32, (tb, 1), 0)
        g2 = jnp.where(tok < ctx_len, mod_ref[0, 0, :, 5 * d:6 * d], mod_ref[0, 1, :, 5 * d:6 * d])
        x_out[0] = x_ref[0] + g2 * acc_sc[...].T


def peer_experts(xs, mod, h2, u_tab, vt_tab, cnt, e0, rank, e1, ctx_len):
    bsz, s, d = xs.shape
    tb = PEER_TOKEN_BLOCK if s % PEER_TOKEN_BLOCK == 0 else TOKEN_BLOCK
    ech = PEER_I_PER_STEP * N_KEYS
    n_exp = u_tab.shape[0]
    tok = lambda b, j, e: (b, j, 0)
    per_tok = lambda b, j, e: (b, 0, 0, j)
    per_i = lambda b, j, e: (b, 0, e, j)
    once = dict(pipeline_mode=pl.Buffered(1))
    return pl.pallas_call(
        functools.partial(_peer_expert_kernel, ctx_len=ctx_len),
        out_shape=jax.ShapeDtypeStruct((bsz, s, d), F32),
        grid=(bsz, s // tb, n_exp // ech),
        in_specs=[pl.BlockSpec((1, tb, d), tok, **once),
                  pl.BlockSpec((1, 2, 1, mod.shape[-1]), lambda b, j, e: (b, 0, 0, 0)),
                  pl.BlockSpec((1, tb, d), tok, **once),
                  pl.BlockSpec((ech, d), lambda b, j, e: (e, 0)),
                  pl.BlockSpec((d, ech), lambda b, j, e: (0, e)),
                  pl.BlockSpec((1, PEER_HEADS, PEER_I_PER_STEP, tb), per_i),
                  pl.BlockSpec((1, PEER_HEADS, PEER_I_PER_STEP, tb), per_i),
                  pl.BlockSpec((1, PEER_HEADS, N_KEYS, tb), per_tok, **once),
                  pl.BlockSpec((1, PEER_HEADS, N_KEYS, tb), per_tok, **once)],
        out_specs=pl.BlockSpec((1, tb, d), tok, **once),
        scratch_shapes=[pltpu.VMEM((d, tb), F32)],
        compiler_params=_params(("parallel", "parallel", "arbitrary")),
        name="peer_experts",
    )(xs, mod, h2, u_tab, vt_tab, cnt, e0, rank, e1)


def _rope_tables(ctx_len, n_lat):
    rows = n_lat // GRID_W
    row = jnp.repeat(jnp.arange(rows, dtype=F32), GRID_W)
    col = jnp.tile(jnp.arange(GRID_W, dtype=F32), rows)
    axis_dim = HEAD_DIM // 2
    inv_freq = ROPE_THETA ** (-jnp.arange(0, axis_dim, 2, dtype=F32) / axis_dim)
    ang_r = row[:, None] * inv_freq[None, :]
    ang_c = col[:, None] * inv_freq[None, :]
    ang = jnp.concatenate([ang_r, ang_r, ang_c, ang_c], axis=-1)
    cos, sin = jnp.cos(ang), jnp.sin(ang)
    first = (jnp.arange(HEAD_DIM) % (HEAD_DIM // 2)) < (HEAD_DIM // 4)
    sina = jnp.where(first, -sin, 0.0)
    sinb = jnp.where(first, 0.0, sin)
    pad = lambda t, v: jnp.concatenate([jnp.full((ctx_len, HEAD_DIM), v, F32), t], axis=0)
    return pad(cos, 1.0), pad(sina, 0.0), pad(sinb, 0.0)


def kernel(x, c, ctx, c_ctx, ada_w, ada_b, norm1_w, norm2_w, w_in, attn_qnorm_w, attn_knorm_w, dn_conv_w,
           dn_A_log, dn_dt_bias, dn_norm_w, w_out, peer_wq, peer_subkeys, peer_u, peer_v):
    bsz, n_lat, d = x.shape
    ctx_len = ctx.shape[1]
    depth = ada_w.shape[0]
    assert ctx_len == TOKEN_BLOCK and n_lat % TOKEN_BLOCK == 0 and bsz + 1 <= SUBLANES
    assert w_in.shape[-1] == IN_MAIN_W + N_GATE_COLS

    xs = jnp.concatenate([ctx, x], axis=1)
    cos, sina, sinb = _rope_tables(ctx_len, n_lat)

    cc = jnp.zeros((SUBLANES, d), F32).at[:bsz].set(c).at[bsz].set(c_ctx)
    mod_all = ada_modulation(cc, ada_w, ada_b)
    mod_ctx = jnp.broadcast_to(mod_all[:, bsz][:, None], (depth, bsz, 6 * d))
    mod = jnp.stack([mod_ctx, mod_all[:, :bsz]], axis=2)[:, :, :, None, :]

    for l in range(depth):
        w_main = w_in[l, :, :IN_MAIN_W].astype(BF16)
        w_gate = jnp.pad(w_in[l, :, IN_MAIN_W:], ((0, 0), (0, LANES - N_GATE_COLS))).astype(BF16)
        q, k, v, dnqkv, gate, ba = in_projection(xs, mod[l], norm1_w[l], w_main, w_gate, cos, sina, sinb,
                                                 attn_qnorm_w[l], attn_knorm_w[l])
        attn = attention(q, k, v, ctx_len)
        conv_w = jnp.pad(dn_conv_w[l], ((0, SUBLANES - CONV_K), (0, 0)))
        gate_par = jnp.zeros((SUBLANES, LANES), F32)
        gate_par = gate_par.at[0, 2 * DN_HEADS:4 * DN_HEADS].set(dn_A_log[l].reshape(-1))
        gate_par = gate_par.at[1, 2 * DN_HEADS:4 * DN_HEADS].set(dn_dt_bias[l].reshape(-1))
        o_f, o_b = dn_scan(*dn_prepare(dnqkv, ba, conv_w, gate_par, ctx_len), ctx_len)
        xs, h2 = out_projection(xs, mod[l], attn, o_f, o_b, gate, dn_norm_w[l], w_out[l].astype(BF16),
                                norm2_w[l])
        sk = peer_subkeys[l].reshape(2 * PEER_HEADS, N_KEYS, PEER_HALF).astype(BF16)
        st = peer_scores(h2, peer_wq[l].astype(BF16), sk)
        cnt, e0, rank, e1 = peer_topk(st)
        xs = peer_experts(xs, mod[l], h2, (peer_u[l] * GELU_GATE_SCALE).astype(BF16), peer_v[l].T.astype(BF16), cnt, e0, rank, e1,
                          ctx_len)
    return xs[:, ctx_len:]
```

```python
import functools
import math

import jax
import jax.numpy as jnp
from jax import lax
from jax.experimental import pallas as pl
from jax.experimental.pallas import tpu as pltpu

F32 = jnp.float32
BF16 = jnp.bfloat16
HIGHEST = lax.Precision.HIGHEST

HEAD_DIM = 128
ATTN_HEADS = 4
ATTN_KV_HEADS = 2
ATTN_GROUP = ATTN_HEADS // ATTN_KV_HEADS
DN_HEADS = 4
ATTN_Q_W = ATTN_HEADS * HEAD_DIM
ATTN_KV_W = ATTN_KV_HEADS * HEAD_DIM
DN_W = DN_HEADS * HEAD_DIM
IN_MAIN_W = ATTN_Q_W + 2 * ATTN_KV_W + 3 * DN_W + DN_W
N_GATE_COLS = 4 * DN_HEADS
ROPE_THETA = 10000.0
GRID_W = 64
CONV_K = 5
CONV_PAD = CONV_K // 2
PEER_HEADS = 8
PEER_HALF = 128
N_KEYS = 128
PEER_TOPK = 16
EPS = 1e-6
NEG_BIG = -1e30
GELU_GATE_SCALE = 2.0 ** -0.5

LANES = 128
SUBLANES = 8
MXU_TILE = 256
TOKEN_BLOCK = 256
DN_CHUNK = 128
DN_BASE_BLOCK = 4
ATTN_KEY_BLOCK = 1280
ATTN_Q_BLOCK = 1024
ATTN_CHAIN_ROWS = 256
PEER_TOKEN_BLOCK = 1280
PEER_I_PER_STEP = 8
VMEM_LIMIT = 56 * 1024 * 1024


def _params(sem):
    return pltpu.CompilerParams(dimension_semantics=sem, vmem_limit_bytes=VMEM_LIMIT)


def _bf16_dot(a, b):
    return jnp.dot(a.astype(BF16), b.astype(BF16), preferred_element_type=F32)


def _split_dot(a, b):
    a_hi = a.astype(BF16)
    b_hi = b.astype(BF16)
    a_lo = (a - a_hi.astype(F32)).astype(BF16)
    b_lo = (b - b_hi.astype(F32)).astype(BF16)
    dot = functools.partial(jnp.dot, preferred_element_type=F32)
    return dot(a_hi, b_hi) + (dot(a_hi, b_lo) + dot(a_lo, b_hi))


def _bf16_dot_nt(a, b):
    return lax.dot_general(a.astype(BF16), b.astype(BF16), (((1,), (1,)), ((), ())),
                           preferred_element_type=F32)


def _ada_kernel(c_ref, w_ref, b_ref, o_ref):
    c = c_ref[...]
    a = c * jax.nn.sigmoid(c)
    o_ref[0] = jnp.dot(a, w_ref[0], preferred_element_type=F32, precision=HIGHEST) + b_ref[0]


def ada_modulation(cc, ada_w, ada_b):
    depth, d, n = ada_w.shape
    tn = 1536
    return pl.pallas_call(
        _ada_kernel,
        out_shape=jax.ShapeDtypeStruct((depth, SUBLANES, n), F32),
        grid=(depth, n // tn),
        in_specs=[pl.BlockSpec((SUBLANES, d), lambda l, j: (0, 0)),
                  pl.BlockSpec((1, d, tn), lambda l, j: (l, 0, j)),
                  pl.BlockSpec((1, 1, tn), lambda l, j: (l, 0, j))],
        out_specs=pl.BlockSpec((1, SUBLANES, tn), lambda l, j: (l, 0, j)),
        compiler_params=_params(("parallel", "parallel")),
        name="ada_modulation",
    )(cc, ada_w, ada_b.reshape(depth, 1, n))


def _mod_index(b, j):
    return (b, jnp.minimum(j, 1), 0, 0)


def _modulated_norm(x, nw, shift, scale):
    ms = jnp.mean(x * x, axis=-1, keepdims=True)
    y = x * lax.rsqrt(ms + EPS) * nw
    return y * (1.0 + scale) + shift


def _head_rmsnorm(x, w):
    return x * lax.rsqrt(jnp.mean(x * x, axis=-1, keepdims=True) + EPS) * w


def _inproj_kernel(x_ref, mod_ref, nw_ref, wm_ref, wg_ref, cos_ref, sina_ref, sinb_ref, qnw_ref, knw_ref,
                   q_out, k_out, v_out, dn_out, gate_out, ba_out):
    d = x_ref.shape[-1]
    x = x_ref[0]
    h = _modulated_norm(x, nw_ref[...], mod_ref[0, 0, :, 0:d], mod_ref[0, 0, :, d:2 * d]).astype(BF16)
    p = jnp.dot(h, wm_ref[...], preferred_element_type=F32)
    ba_out[0] = jnp.dot(h, wg_ref[...], preferred_element_type=F32)
    cos = cos_ref[...]
    sina = sina_ref[...]
    sinb = sinb_ref[...]

    def rope(t):
        return (t * cos + pltpu.roll(t, HEAD_DIM - HEAD_DIM // 4, 1) * sina
                + pltpu.roll(t, HEAD_DIM // 4, 1) * sinb)

    scale = HEAD_DIM ** -0.5 * math.log2(math.e)
    for hd in range(ATTN_HEADS):
        qh = _head_rmsnorm(p[:, hd * HEAD_DIM:(hd + 1) * HEAD_DIM], qnw_ref[...])
        q_out[0, hd] = (rope(qh) * scale).astype(BF16)
    off = ATTN_Q_W
    for hd in range(ATTN_KV_HEADS):
        kh = _head_rmsnorm(p[:, off + hd * HEAD_DIM:off + (hd + 1) * HEAD_DIM], knw_ref[...])
        k_out[0, hd] = rope(kh).astype(BF16)
    off += ATTN_KV_W
    for hd in range(ATTN_KV_HEADS):
        v_out[0, hd, :, 0:HEAD_DIM] = p[:, off + hd * HEAD_DIM:off + (hd + 1) * HEAD_DIM].astype(BF16)
        v_out[0, hd, :, HEAD_DIM:2 * HEAD_DIM] = jnp.ones((x.shape[0], HEAD_DIM), BF16)
    off += ATTN_KV_W
    dn_out[0] = p[:, off:off + 3 * DN_W]
    off += 3 * DN_W
    gate_out[0] = p[:, off:off + DN_W]


def in_projection(xs, mod, norm_w, w_main, w_gate, cos, sina, sinb, qn_w, kn_w):
    bsz, s, d = xs.shape
    tm = TOKEN_BLOCK
    tok = lambda b, j: (b, j, 0)
    head_tok = lambda b, j: (b, 0, j, 0)
    full2 = lambda b, j: (0, 0)
    rope_spec = pl.BlockSpec((tm, HEAD_DIM), lambda b, j: (j, 0))
    return pl.pallas_call(
        _inproj_kernel,
        out_shape=(jax.ShapeDtypeStruct((bsz, ATTN_HEADS, s, HEAD_DIM), BF16),
                   jax.ShapeDtypeStruct((bsz, ATTN_KV_HEADS, s, HEAD_DIM), BF16),
                   jax.ShapeDtypeStruct((bsz, ATTN_KV_HEADS, s, 2 * HEAD_DIM), BF16),
                   jax.ShapeDtypeStruct((bsz, s, 3 * DN_W), F32),
                   jax.ShapeDtypeStruct((bsz, s, DN_W), F32),
                   jax.ShapeDtypeStruct((bsz, s, LANES), F32)),
        grid=(bsz, s // tm),
        in_specs=[pl.BlockSpec((1, tm, d), tok),
                  pl.BlockSpec((1, 1, 1, mod.shape[-1]), _mod_index),
                  pl.BlockSpec((1, d), full2),
                  pl.BlockSpec((d, IN_MAIN_W), full2),
                  pl.BlockSpec((d, LANES), full2),
                  rope_spec, rope_spec, rope_spec,
                  pl.BlockSpec((1, HEAD_DIM), full2),
                  pl.BlockSpec((1, HEAD_DIM), full2)],
        out_specs=(pl.BlockSpec((1, ATTN_HEADS, tm, HEAD_DIM), head_tok),
                   pl.BlockSpec((1, ATTN_KV_HEADS, tm, HEAD_DIM), head_tok),
                   pl.BlockSpec((1, ATTN_KV_HEADS, tm, 2 * HEAD_DIM), head_tok),
                   pl.BlockSpec((1, tm, 3 * DN_W), tok),
                   pl.BlockSpec((1, tm, DN_W), tok),
                   pl.BlockSpec((1, tm, LANES), tok)),
        compiler_params=_params(("parallel", "parallel")),
        name="in_projection",
    )(xs, mod, norm_w.reshape(1, d), w_main, w_gate, cos, sina, sinb,
      qn_w.reshape(1, HEAD_DIM), kn_w.reshape(1, HEAD_DIM))


def _attn_kernel(q_ref, k_ref, v_ref, o_ref, m_sc, acc_sc):
    ki = pl.program_id(2)
    nk = pl.num_programs(2)

    @pl.when(ki == 0)
    def _():
        m_sc[...] = jnp.full(m_sc.shape, NEG_BIG, F32)
        acc_sc[...] = jnp.zeros(acc_sc.shape, F32)

    tq = q_ref.shape[2]
    chains = [(hd, pl.ds(r0, ATTN_CHAIN_ROWS)) for hd in range(ATTN_HEADS)
              for r0 in range(0, tq, ATTN_CHAIN_ROWS)]
    scores = [lax.dot_general(q_ref[0, hd, rows, :], k_ref[0, hd // ATTN_GROUP], (((1,), (1,)), ((), ())),
                              preferred_element_type=F32) for hd, rows in chains]
    for (hd, rows), s in zip(chains, scores):
        m_prev = m_sc[hd, rows, :]
        m_new = jnp.maximum(m_prev, jnp.max(s, axis=-1, keepdims=True))
        p = jnp.exp2(s - m_new)
        alpha = jnp.exp2(m_prev - m_new)
        acc_sc[hd, rows, :] = alpha * acc_sc[hd, rows, :] + jnp.dot(
            p.astype(BF16), v_ref[0, hd // ATTN_GROUP], preferred_element_type=F32)
        m_sc[hd, rows, :] = m_new

    @pl.when(ki == nk - 1)
    def _():
        for hd in range(ATTN_HEADS):
            o_ref[0, :, hd * HEAD_DIM:(hd + 1) * HEAD_DIM] = (
                acc_sc[hd, :, 0:HEAD_DIM] / acc_sc[hd, :, HEAD_DIM:2 * HEAD_DIM]).astype(o_ref.dtype)


def _attention_call(q, k, v, tq, n_keys, tk, name):
    bsz = q.shape[0]
    n_q_blocks = q.shape[2] // tq
    return pl.pallas_call(
        _attn_kernel,
        out_shape=jax.ShapeDtypeStruct((bsz, n_q_blocks * tq, ATTN_Q_W), BF16),
        grid=(bsz, n_q_blocks, n_keys // tk),
        in_specs=[pl.BlockSpec((1, ATTN_HEADS, tq, HEAD_DIM), lambda b, i, j: (b, 0, i, 0)),
                  pl.BlockSpec((1, ATTN_KV_HEADS, tk, HEAD_DIM), lambda b, i, j: (b, 0, j, 0)),
                  pl.BlockSpec((1, ATTN_KV_HEADS, tk, 2 * HEAD_DIM), lambda b, i, j: (b, 0, j, 0))],
        out_specs=pl.BlockSpec((1, tq, ATTN_Q_W), lambda b, i, j: (b, i, 0)),
        scratch_shapes=[pltpu.VMEM((ATTN_HEADS, tq, 1), F32),
                        pltpu.VMEM((ATTN_HEADS, tq, 2 * HEAD_DIM), F32)],
        compiler_params=_params(("parallel", "parallel", "arbitrary")),
        name=name,
    )(q, k, v)


def attention(q, k, v, ctx_len):
    s = q.shape[2]
    assert ctx_len % ATTN_CHAIN_ROWS == 0 and (s - ctx_len) % ATTN_CHAIN_ROWS == 0
    tk = ATTN_KEY_BLOCK if s % ATTN_KEY_BLOCK == 0 else ATTN_CHAIN_ROWS
    tq = ATTN_Q_BLOCK if (s - ctx_len) % ATTN_Q_BLOCK == 0 else ATTN_CHAIN_ROWS
    attn_ctx = _attention_call(q[:, :, :ctx_len], k, v, ctx_len, ctx_len, ctx_len, "attention_ctx")
    attn_lat = _attention_call(q[:, :, ctx_len:], k, v, tq, s, tk, "attention")
    return jnp.concatenate([attn_ctx, attn_lat], axis=1)


def _dn_prep_kernel(main_ref, prev_ref, next_ref, ba_ref, cw_ref, gp_ref,
                    w_out, u_out, qg_out, kdt_out, qk_out, dl_out, ext_sc, *, ctx_chunks):
    j = pl.program_id(1)
    nj = pl.num_programs(1)
    c = DN_CHUNK
    has_prev = (j != 0) & (j != ctx_chunks)
    has_next = (j != ctx_chunks - 1) & (j != nj - 1)
    ext_sc[0:SUBLANES] = jnp.where(has_prev, prev_ref[0], 0.0)
    ext_sc[SUBLANES:SUBLANES + c] = main_ref[0]
    ext_sc[SUBLANES + c:2 * SUBLANES + c] = jnp.where(has_next, next_ref[0], 0.0)
    y = ext_sc[SUBLANES - CONV_PAD:SUBLANES - CONV_PAD + c] * cw_ref[0:1]
    for t in range(1, CONV_K):
        y = y + ext_sc[SUBLANES - CONV_PAD + t:SUBLANES - CONV_PAD + t + c] * cw_ref[t:t + 1]
    y = y * jax.nn.sigmoid(y)

    ba = ba_ref[0]
    beta_all = jax.nn.sigmoid(ba)
    g_all = -jnp.exp(gp_ref[0:1]) * jax.nn.softplus(ba + gp_ref[1:2])
    row = lax.broadcasted_iota(jnp.int32, (c, c), 0)
    col = lax.broadcasted_iota(jnp.int32, (c, c), 1)
    lower = (row >= col).astype(BF16)
    g_hi = g_all.astype(BF16)
    g_r1 = g_all - g_hi.astype(F32)
    g_mid = g_r1.astype(BF16)
    g_lo = (g_r1 - g_mid.astype(F32)).astype(BF16)
    dotf = functools.partial(jnp.dot, preferred_element_type=F32)
    prefix = dotf(lower, g_hi) + (dotf(lower, g_mid) + dotf(lower, g_lo))
    total = prefix[c - 1:c]
    gc = (prefix, total - prefix + g_all)
    gct = (gc[0].T, gc[1].T)
    row2 = lax.broadcasted_iota(jnp.int32, (2 * c, 2 * c), 0)
    col2 = lax.broadcasted_iota(jnp.int32, (2 * c, 2 * c), 1)
    eye2 = (row2 == col2).astype(F32)
    zero = jnp.zeros((c, c), F32)

    heads = range(DN_HEADS)
    a2, rhs2 = [], []
    for hd in heads:
        q = y[:, hd * HEAD_DIM:(hd + 1) * HEAD_DIM]
        k = y[:, DN_W + hd * HEAD_DIM:DN_W + (hd + 1) * HEAD_DIM]
        v = y[:, 2 * DN_W + hd * HEAD_DIM:2 * DN_W + (hd + 1) * HEAD_DIM]
        q = q * lax.rsqrt(jnp.sum(q * q, axis=-1, keepdims=True) + EPS) * (HEAD_DIM ** -0.5)
        k = k * lax.rsqrt(jnp.sum(k * k, axis=-1, keepdims=True) + EPS)
        kk = _bf16_dot_nt(k, k)
        qk = _bf16_dot_nt(q, k)
        a_dir, rhs_dir = [], []
        for dr in range(2):
            cb = dr * DN_HEADS + hd
            cg = 2 * DN_HEADS + cb
            beta = beta_all[:, cb:cb + 1]
            gcol = gc[dr][:, cg:cg + 1]
            grow = gct[dr][cg:cg + 1, :]
            incl = (row >= col) if dr == 0 else (row <= col)
            strict = (row > col) if dr == 0 else (row < col)
            decay = jnp.exp(jnp.where(incl, gcol - grow, NEG_BIG))
            a_dir.append(jnp.where(strict, kk * beta * decay, 0.0))
            eg = jnp.exp(gcol)
            rhs_dir.append(jnp.concatenate([v * beta, k * (beta * eg)], axis=1))
            qg_out[0, dr, hd] = (q * eg).astype(BF16)
            tot = total[:, cg:cg + 1]
            kdt_out[0, dr, hd] = (k * jnp.exp(tot - gcol)).T.astype(BF16)
            qk_out[0, dr, hd] = (qk * decay).astype(BF16)
            dl_out[0, dr, hd, 0] = jnp.broadcast_to(jnp.exp(tot), (1, LANES))
        a2.append(jnp.concatenate([jnp.concatenate([a_dir[0], zero], axis=1),
                                   jnp.concatenate([zero, a_dir[1]], axis=1)], axis=0))
        rhs2.append(jnp.concatenate(rhs_dir, axis=0))
    base = DN_BASE_BLOCK
    base_mask = (row2 // base) == (col2 // base)
    a_base = [jnp.where(base_mask, a2[hd], 0.0) for hd in heads]
    x = [eye2 - a_base[hd] for hd in heads]
    pw = [_split_dot(a_base[hd], a_base[hd]) for hd in heads]
    for it in range(int(math.log2(base)) - 1):
        x = [x[hd] + _split_dot(x[hd], pw[hd]) for hd in heads]
        if it < int(math.log2(base)) - 2:
            pw = [_split_dot(pw[hd], pw[hd]) for hd in heads]
    blk = base
    while blk < c:
        sibling = ((row2 // (2 * blk)) == (col2 // (2 * blk))) & ((row2 // blk) != (col2 // blk))
        fold = [_bf16_dot(x[hd], jnp.where(sibling, a2[hd], 0.0)) for hd in heads]
        x = [x[hd] - _bf16_dot(fold[hd], x[hd]) for hd in heads]
        blk *= 2
    for hd in heads:
        uw = _bf16_dot(x[hd], rhs2[hd])
        for dr in range(2):
            u_out[0, dr, hd] = uw[dr * c:(dr + 1) * c, 0:HEAD_DIM]
            w_out[0, dr, hd] = uw[dr * c:(dr + 1) * c, HEAD_DIM:2 * HEAD_DIM].astype(BF16)


def dn_prepare(dnqkv, ba, conv_w, gate_par, ctx_len):
    bsz, s, wdt = dnqkv.shape
    c = DN_CHUNK
    nc = s // c
    rows8 = s // SUBLANES
    per = c // SUBLANES
    chain = lambda b, j: (b, 0, 0, j, 0)
    return pl.pallas_call(
        functools.partial(_dn_prep_kernel, ctx_chunks=ctx_len // c),
        out_shape=(jax.ShapeDtypeStruct((bsz, 2, DN_HEADS, s, HEAD_DIM), BF16),
                   jax.ShapeDtypeStruct((bsz, 2, DN_HEADS, s, HEAD_DIM), F32),
                   jax.ShapeDtypeStruct((bsz, 2, DN_HEADS, s, HEAD_DIM), BF16),
                   jax.ShapeDtypeStruct((bsz, 2, DN_HEADS, HEAD_DIM, s), BF16),
                   jax.ShapeDtypeStruct((bsz, 2, DN_HEADS, s, c), BF16),
                   jax.ShapeDtypeStruct((bsz, 2, DN_HEADS, nc, 1, LANES), F32)),
        grid=(bsz, nc),
        in_specs=[pl.BlockSpec((1, c, wdt), lambda b, j: (b, j, 0)),
                  pl.BlockSpec((1, SUBLANES, wdt), lambda b, j: (b, jnp.maximum(j * per - 1, 0), 0)),
                  pl.BlockSpec((1, SUBLANES, wdt), lambda b, j: (b, jnp.minimum((j + 1) * per, rows8 - 1), 0)),
                  pl.BlockSpec((1, c, LANES), lambda b, j: (b, j, 0)),
                  pl.BlockSpec((SUBLANES, wdt), lambda b, j: (0, 0)),
                  pl.BlockSpec((SUBLANES, LANES), lambda b, j: (0, 0))],
        out_specs=(pl.BlockSpec((1, 2, DN_HEADS, c, HEAD_DIM), chain),
                   pl.BlockSpec((1, 2, DN_HEADS, c, HEAD_DIM), chain),
                   pl.BlockSpec((1, 2, DN_HEADS, c, HEAD_DIM), chain),
                   pl.BlockSpec((1, 2, DN_HEADS, HEAD_DIM, c), lambda b, j: (b, 0, 0, 0, j)),
                   pl.BlockSpec((1, 2, DN_HEADS, c, c), chain),
                   pl.BlockSpec((1, 2, DN_HEADS, 1, 1, LANES), lambda b, j: (b, 0, 0, j, 0, 0))),
        scratch_shapes=[pltpu.VMEM((c + 2 * SUBLANES, wdt), F32)],
        compiler_params=_params(("parallel", "parallel")),
        name="dn_prepare",
    )(dnqkv, dnqkv, dnqkv, ba, conv_w, gate_par)


def _dn_scan_kernel(*refs, bsz):
    ins = refs[:12]
    of_ref, ob_ref, s_sc = refs[12:]
    n = pl.program_id(0)

    @pl.when(n == 0)
    def _():
        s_sc[...] = jnp.zeros(s_sc.shape, F32)

    chains = [(dr, b, hd) for dr in range(2) for b in range(bsz) for hd in range(DN_HEADS)]
    dotf = functools.partial(jnp.dot, preferred_element_type=F32)

    def inp(dr, k):
        return ins[dr * 6 + k]

    state = [s_sc[ci] for ci in range(len(chains))]
    r = [dotf(jnp.concatenate([inp(dr, 0)[b, 0, hd], inp(dr, 2)[b, 0, hd]], axis=0), state[ci].astype(BF16))
         for ci, (dr, b, hd) in enumerate(chains)]
    v_new = [(inp(dr, 1)[b, 0, hd] - r[ci][0:DN_CHUNK]).astype(BF16) for ci, (dr, b, hd) in enumerate(chains)]
    intra = [dotf(inp(dr, 4)[b, 0, hd], v_new[ci]) for ci, (dr, b, hd) in enumerate(chains)]
    upd = [dotf(inp(dr, 3)[b, 0, hd], v_new[ci]) for ci, (dr, b, hd) in enumerate(chains)]
    for ci, (dr, b, hd) in enumerate(chains):
        o_ref = of_ref if dr == 0 else ob_ref
        o_ref[b, :, hd * HEAD_DIM:(hd + 1) * HEAD_DIM] = r[ci][DN_CHUNK:] + intra[ci]
        s_sc[ci] = state[ci] * inp(dr, 5)[b, 0, hd, 0] + upd[ci]


def dn_scan(w, u, qg, kdt, qk, dl, ctx_len):
    bsz, _, _, s, _ = w.shape
    c = DN_CHUNK
    nc = s // c
    cc = ctx_len // c

    def bwd_chunk(n):
        return jnp.where(n < cc, cc - 1 - n, nc - 1 - (n - cc))

    in_specs, args = [], []
    for dr in range(2):
        pos = (lambda n: n) if dr == 0 else bwd_chunk
        tokm = lambda n, dr=dr, pos=pos: (0, dr, 0, pos(n), 0)
        for arr in (w, u, qg):
            in_specs.append(pl.BlockSpec((bsz, 1, DN_HEADS, c, HEAD_DIM), tokm))
            args.append(arr)
        in_specs.append(pl.BlockSpec((bsz, 1, DN_HEADS, HEAD_DIM, c), lambda n, dr=dr, pos=pos: (0, dr, 0, 0, pos(n))))
        args.append(kdt)
        in_specs.append(pl.BlockSpec((bsz, 1, DN_HEADS, c, c), tokm))
        args.append(qk)
        in_specs.append(pl.BlockSpec((bsz, 1, DN_HEADS, 1, 1, LANES), lambda n, dr=dr, pos=pos: (0, dr, 0, pos(n), 0, 0)))
        args.append(dl)
    return pl.pallas_call(
        functools.partial(_dn_scan_kernel, bsz=bsz),
        out_shape=(jax.ShapeDtypeStruct((bsz, s, DN_W), F32), jax.ShapeDtypeStruct((bsz, s, DN_W), F32)),
        grid=(nc,),
        in_specs=in_specs,
        out_specs=(pl.BlockSpec((bsz, c, DN_W), lambda n: (0, n, 0)),
                   pl.BlockSpec((bsz, c, DN_W), lambda n: (0, bwd_chunk(n), 0))),
        scratch_shapes=[pltpu.VMEM((2 * bsz * DN_HEADS, HEAD_DIM, HEAD_DIM), F32)],
        compiler_params=_params(("arbitrary",)),
        name="dn_scan",
    )(*args)


def _outproj_kernel(x_ref, mod_ref, attn_ref, of_ref, ob_ref, gate_ref, dnw_ref, wo_ref, n2w_ref,
                    x_out, h2_out):
    d = x_ref.shape[-1]
    o = of_ref[0] + ob_ref[0]
    gate = gate_ref[0]
    parts = [attn_ref[0]]
    for hd in range(DN_HEADS):
        sl = slice(hd * HEAD_DIM, (hd + 1) * HEAD_DIM)
        g = gate[:, sl]
        parts.append((_head_rmsnorm(o[:, sl], dnw_ref[...]) * (g * jax.nn.sigmoid(g))).astype(BF16))
    mix = jnp.concatenate(parts, axis=1)
    y = jnp.dot(mix, wo_ref[...], preferred_element_type=F32)
    x = x_ref[0] + mod_ref[0, 0, :, 2 * d:3 * d] * y
    x_out[0] = x
    h2_out[0] = _modulated_norm(x, n2w_ref[...], mod_ref[0, 0, :, 3 * d:4 * d],
                                mod_ref[0, 0, :, 4 * d:5 * d]).astype(BF16)


def out_projection(xs, mod, attn, o_f, o_b, gate, dn_norm_w, w_out, norm2_w):
    bsz, s, d = xs.shape
    tm = TOKEN_BLOCK
    tok = lambda b, j: (b, j, 0)
    full2 = lambda b, j: (0, 0)
    return pl.pallas_call(
        _outproj_kernel,
        out_shape=(jax.ShapeDtypeStruct((bsz, s, d), F32), jax.ShapeDtypeStruct((bsz, s, d), BF16)),
        grid=(bsz, s // tm),
        in_specs=[pl.BlockSpec((1, tm, d), tok),
                  pl.BlockSpec((1, 1, 1, mod.shape[-1]), _mod_index),
                  pl.BlockSpec((1, tm, ATTN_Q_W), tok),
                  pl.BlockSpec((1, tm, DN_W), tok),
                  pl.BlockSpec((1, tm, DN_W), tok),
                  pl.BlockSpec((1, tm, DN_W), tok),
                  pl.BlockSpec((1, HEAD_DIM), full2),
                  pl.BlockSpec(w_out.shape, full2),
                  pl.BlockSpec((1, d), full2)],
        out_specs=(pl.BlockSpec((1, tm, d), tok), pl.BlockSpec((1, tm, d), tok)),
        compiler_params=_params(("parallel", "parallel")),
        name="out_projection",
    )(xs, mod, attn, o_f, o_b, gate, dn_norm_w.reshape(1, HEAD_DIM), w_out, norm2_w.reshape(1, d))


def _peer_score_kernel(h2_ref, wq_ref, sk_ref, st_out):
    q = jnp.dot(h2_ref[0], wq_ref[...], preferred_element_type=F32).astype(BF16)
    for hp in range(2 * PEER_HEADS):
        st_out[0, hp] = lax.dot_general(sk_ref[hp], q[:, hp * PEER_HALF:(hp + 1) * PEER_HALF],
                                        (((1,), (1,)), ((), ())), preferred_element_type=F32)


def peer_scores(h2, wq, subkeys):
    bsz, s, d = h2.shape
    tm = TOKEN_BLOCK
    nhp = 2 * PEER_HEADS
    return pl.pallas_call(
        _peer_score_kernel,
        out_shape=jax.ShapeDtypeStruct((bsz, nhp, N_KEYS, s), F32),
        grid=(bsz, s // tm),
        in_specs=[pl.BlockSpec((1, tm, d), lambda b, j: (b, j, 0)),
                  pl.BlockSpec(wq.shape, lambda b, j: (0, 0)),
                  pl.BlockSpec(subkeys.shape, lambda b, j: (0, 0, 0))],
        out_specs=pl.BlockSpec((1, nhp, N_KEYS, tm), lambda b, j: (b, 0, 0, j)),
        compiler_params=_params(("parallel", "parallel")),
        name="peer_scores",
    )(h2, wq, subkeys)


def _sorted_top(s, k, with_rank):
    out = []
    rank = jnp.full(s.shape, float(k), F32) if with_rank else None
    for r in range(k):
        m = jnp.max(s, axis=0, keepdims=True)
        out.append(m)
        hit = s == m
        if with_rank:
            rank = jnp.where(hit, float(r), rank)
        s = jnp.where(hit, NEG_BIG, s)
    return out, rank


def _paired_bf16_words(x):
    w = pltpu.bitcast(x.astype(BF16).astype(F32), jnp.uint32)
    return w | (w >> 16)


def _peer_topk_kernel(st_ref, cnt_out, e0_out, rank_out, e1_out):
    nt = PEER_TOPK + 1
    tops = ([], [])
    for hd in range(PEER_HEADS):
        a, _ = _sorted_top(st_ref[0, 2 * hd], nt, False)
        b, rank = _sorted_top(st_ref[0, 2 * hd + 1], nt, True)
        rank_out[0, hd] = rank.astype(BF16)
        tops[0].append(a)
        tops[1].append(b)
    a8 = [jnp.concatenate([tops[0][hd][r] for hd in range(PEER_HEADS)], axis=0) for r in range(nt)]
    b8 = [jnp.concatenate([tops[1][hd][r] for hd in range(PEER_HEADS)], axis=0) for r in range(nt)]
    cand = [a8[i] + b8[j] for i in range(nt) for j in range(nt) if (i + 1) * (j + 1) <= nt]
    top = []
    for _ in range(nt):
        m = functools.reduce(jnp.maximum, cand)
        top.append(m)
        cand = [jnp.where(t == m, NEG_BIG, t) for t in cand]
    tau8 = 0.5 * (top[PEER_TOPK - 1] + top[PEER_TOPK])
    smax = a8[0] + b8[0]
    z8 = jnp.exp(top[0] - smax)
    for t in top[1:PEER_TOPK]:
        z8 = z8 + jnp.exp(t - smax)
    rz8 = 1.0 / z8
    for hd in range(PEER_HEADS):
        s0 = st_ref[0, 2 * hd]
        s1 = st_ref[0, 2 * hd + 1]
        thr = tau8[hd:hd + 1] - s0
        cnt = jnp.zeros(s0.shape, F32)
        for r in range(nt):
            cnt = cnt + jnp.where(tops[1][hd][r] > thr, 1.0, 0.0)
        cnt_out[0, hd] = _paired_bf16_words(cnt)
        e0_out[0, hd] = _paired_bf16_words(jnp.exp(s0 - tops[0][hd][0]))
        e1_out[0, hd] = (jnp.exp(s1 - tops[1][hd][0]) * (rz8[hd:hd + 1] * GELU_GATE_SCALE)).astype(BF16)


def peer_topk(st):
    bsz, nhp, nk, s = st.shape
    tl = LANES
    spec = pl.BlockSpec((1, PEER_HEADS, nk, tl), lambda b, j: (b, 0, 0, j))
    words = jax.ShapeDtypeStruct((bsz, PEER_HEADS, nk, s), jnp.uint32)
    halfs = jax.ShapeDtypeStruct((bsz, PEER_HEADS, nk, s), BF16)
    return pl.pallas_call(
        _peer_topk_kernel,
        out_shape=(words, words, halfs, halfs),
        grid=(bsz, s // tl),
        in_specs=[pl.BlockSpec((1, nhp, nk, tl), lambda b, j: (b, 0, 0, j))],
        out_specs=(spec, spec, spec, spec),
        compiler_params=_params(("parallel", "parallel")),
        name="peer_topk",
    )(st)


def _peer_expert_kernel(x_ref, mod_ref, h2_ref, u_ref, vt_ref, cnt_ref, e0_ref, rank_ref, e1_ref,
                        x_out, acc_sc, *, ctx_len):
    tok_block = pl.program_id(1)
    ec = pl.program_id(2)
    n_ec = pl.num_programs(2)
    d = x_ref.shape[-1]
    tb = x_ref.shape[1]
    pk = 2 * SUBLANES

    @pl.when(ec == 0)
    def _():
        acc_sc[...] = jnp.zeros(acc_sc.shape, F32)

    def row_tile(ref, hd, ii):
        return pltpu.bitcast(jnp.broadcast_to(ref[0, hd, ii:ii + 1, :], (SUBLANES, tb)), BF16)

    h2 = h2_ref[0]
    pair = 2 * N_KEYS
    n_pairs = PEER_I_PER_STEP // 2

    def activations(p):
        return lax.dot_general(u_ref[p * pair:(p + 1) * pair, :], h2, (((1,), (1,)), ((), ())),
                               preferred_element_type=F32)

    def gate_weights(p):
        tiles = []
        for ii in (2 * p, 2 * p + 1):
            wt = [jnp.zeros((pk, tb), BF16) for _ in range(N_KEYS // pk)]
            for hd in range(PEER_HEADS):
                cnt = row_tile(cnt_ref, hd, ii)
                e0 = row_tile(e0_ref, hd, ii)
                for rt in range(N_KEYS // pk):
                    rows = slice(rt * pk, (rt + 1) * pk)
                    sel = jnp.where(rank_ref[0, hd, rows, :] < cnt, e1_ref[0, hd, rows, :],
                                    jnp.zeros((), BF16))
                    wt[rt] = wt[rt] + sel * e0
            tiles += wt
        return tiles

    def gated(at2, tiles):
        g = []
        for rt, wt in enumerate(tiles):
            a = at2[rt * pk:(rt + 1) * pk]
            act = a * (1.0 + lax.erf(a))
            g.append(act.astype(BF16) * wt)
        return jnp.concatenate(g, axis=0)

    g = [gated(activations(p), gate_weights(p)) for p in range(n_pairs)]
    acc_sc[...] += jnp.dot(vt_ref[...], jnp.concatenate(g, axis=0), preferred_element_type=F32)

    @pl.when(ec == n_ec - 1)
    def _():
        tok = tok_block * tb + lax.broadcasted_iota(jnp.int32, (tb, 1), 0)
        g2 = jnp.where(tok < ctx_len, mod_ref[0, 0, :, 5 * d:6 * d], mod_ref[0, 1, :, 5 * d:6 * d])
        x_out[0] = x_ref[0] + g2 * acc_sc[...].T


def peer_experts(xs, mod, h2, u_tab, vt_tab, cnt, e0, rank, e1, ctx_len):
    bsz, s, d = xs.shape
    tb = PEER_TOKEN_BLOCK if s % PEER_TOKEN_BLOCK == 0 else TOKEN_BLOCK
    ech = PEER_I_PER_STEP * N_KEYS
    n_exp = u_tab.shape[0]
    tok = lambda b, j, e: (b, j, 0)
    per_tok = lambda b, j, e: (b, 0, 0, j)
    per_i = lambda b, j, e: (b, 0, e, j)
    once = dict(pipeline_mode=pl.Buffered(1))
    return pl.pallas_call(
        functools.partial(_peer_expert_kernel, ctx_len=ctx_len),
        out_shape=jax.ShapeDtypeStruct((bsz, s, d), F32),
        grid=(bsz, s // tb, n_exp // ech),
        in_specs=[pl.BlockSpec((1, tb, d), tok, **once),
                  pl.BlockSpec((1, 2, 1, mod.shape[-1]), lambda b, j, e: (b, 0, 0, 0)),
                  pl.BlockSpec((1, tb, d), tok, **once),
                  pl.BlockSpec((ech, d), lambda b, j, e: (e, 0)),
                  pl.BlockSpec((d, ech), lambda b, j, e: (0, e)),
                  pl.BlockSpec((1, PEER_HEADS, PEER_I_PER_STEP, tb), per_i),
                  pl.BlockSpec((1, PEER_HEADS, PEER_I_PER_STEP, tb), per_i),
                  pl.BlockSpec((1, PEER_HEADS, N_KEYS, tb), per_tok, **once),
                  pl.BlockSpec((1, PEER_HEADS, N_KEYS, tb), per_tok, **once)],
        out_specs=pl.BlockSpec((1, tb, d), tok, **once),
        scratch_shapes=[pltpu.VMEM((d, tb), F32)],
        compiler_params=_params(("parallel", "parallel", "arbitrary")),
        name="peer_experts",
    )(xs, mod, h2, u_tab, vt_tab, cnt, e0, rank, e1)


def _rope_tables(ctx_len, n_lat):
    rows = n_lat // GRID_W
    row = jnp.repeat(jnp.arange(rows, dtype=F32), GRID_W)
    col = jnp.tile(jnp.arange(GRID_W, dtype=F32), rows)
    axis_dim = HEAD_DIM // 2
    inv_freq = ROPE_THETA ** (-jnp.arange(0, axis_dim, 2, dtype=F32) / axis_dim)
    ang_r = row[:, None] * inv_freq[None, :]
    ang_c = col[:, None] * inv_freq[None, :]
    ang = jnp.concatenate([ang_r, ang_r, ang_c, ang_c], axis=-1)
    cos, sin = jnp.cos(ang), jnp.sin(ang)
    first = (jnp.arange(HEAD_DIM) % (HEAD_DIM // 2)) < (HEAD_DIM // 4)
    sina = jnp.where(first, -sin, 0.0)
    sinb = jnp.where(first, 0.0, sin)
    pad = lambda t, v: jnp.concatenate([jnp.full((ctx_len, HEAD_DIM), v, F32), t], axis=0)
    return pad(cos, 1.0), pad(sina, 0.0), pad(sinb, 0.0)


def kernel(x, c, ctx, c_ctx, ada_w, ada_b, norm1_w, norm2_w, w_in, attn_qnorm_w, attn_knorm_w, dn_conv_w,
           dn_A_log, dn_dt_bias, dn_norm_w, w_out, peer_wq, peer_subkeys, peer_u, peer_v):
    bsz, n_lat, d = x.shape
    ctx_len = ctx.shape[1]
    depth = ada_w.shape[0]
    assert ctx_len == TOKEN_BLOCK and n_lat % TOKEN_BLOCK == 0 and bsz + 1 <= SUBLANES
    assert w_in.shape[-1] == IN_MAIN_W + N_GATE_COLS

    xs = jnp.concatenate([ctx, x], axis=1)
    cos, sina, sinb = _rope_tables(ctx_len, n_lat)

    cc = jnp.zeros((SUBLANES, d), F32).at[:bsz].set(c).at[bsz].set(c_ctx)
    mod_all = ada_modulation(cc, ada_w, ada_b)
    mod_ctx = jnp.broadcast_to(mod_all[:, bsz][:, None], (depth, bsz, 6 * d))
    mod = jnp.stack([mod_ctx, mod_all[:, :bsz]], axis=2)[:, :, :, None, :]

    for l in range(depth):
        w_main = w_in[l, :, :IN_MAIN_W].astype(BF16)
        w_gate = jnp.pad(w_in[l, :, IN_MAIN_W:], ((0, 0), (0, LANES - N_GATE_COLS))).astype(BF16)
        q, k, v, dnqkv, gate, ba = in_projection(xs, mod[l], norm1_w[l], w_main, w_gate, cos, sina, sinb,
                                                 attn_qnorm_w[l], attn_knorm_w[l])
        attn = attention(q, k, v, ctx_len)
        conv_w = jnp.pad(dn_conv_w[l], ((0, SUBLANES - CONV_K), (0, 0)))
        gate_par = jnp.zeros((SUBLANES, LANES), F32)
        gate_par = gate_par.at[0, 2 * DN_HEADS:4 * DN_HEADS].set(dn_A_log[l].reshape(-1))
        gate_par = gate_par.at[1, 2 * DN_HEADS:4 * DN_HEADS].set(dn_dt_bias[l].reshape(-1))
        o_f, o_b = dn_scan(*dn_prepare(dnqkv, ba, conv_w, gate_par, ctx_len), ctx_len)
        xs, h2 = out_projection(xs, mod[l], attn, o_f, o_b, gate, dn_norm_w[l], w_out[l].astype(BF16),
                                norm2_w[l])
        sk = peer_subkeys[l].reshape(2 * PEER_HEADS, N_KEYS, PEER_HALF).astype(BF16)
        st = peer_scores(h2, peer_wq[l].astype(BF16), sk)
        cnt, e0, rank, e1 = peer_topk(st)
        xs = peer_experts(xs, mod[l], h2, (peer_u[l] * GELU_GATE_SCALE).astype(BF16), peer_v[l].T.astype(BF16), cnt, e0, rank, e1,
                          ctx_len)
    return xs[:, ctx_len:]
```

```python
import functools
import math

import jax
import jax.numpy as jnp
from jax import lax
from jax.experimental import pallas as pl
from jax.experimental.pallas import tpu as pltpu

F32 = jnp.float32
BF16 = jnp.bfloat16
HIGHEST = lax.Precision.HIGHEST

HEAD_DIM = 128
ATTN_HEADS = 4
ATTN_KV_HEADS = 2
ATTN_GROUP = ATTN_HEADS // ATTN_KV_HEADS
DN_HEADS = 4
ATTN_Q_W = ATTN_HEADS * HEAD_DIM
ATTN_KV_W = ATTN_KV_HEADS * HEAD_DIM
DN_W = DN_HEADS * HEAD_DIM
IN_MAIN_W = ATTN_Q_W + 2 * ATTN_KV_W + 3 * DN_W + DN_W
N_GATE_COLS = 4 * DN_HEADS
ROPE_THETA = 10000.0
GRID_W = 64
CONV_K = 5
CONV_PAD = CONV_K // 2
PEER_HEADS = 8
PEER_HALF = 128
N_KEYS = 128
PEER_TOPK = 16
EPS = 1e-6
NEG_BIG = -1e30
GELU_GATE_SCALE = 2.0 ** -0.5

LANES = 128
SUBLANES = 8
MXU_TILE = 256
TOKEN_BLOCK = 256
DN_CHUNK = 128
DN_BASE_BLOCK = 4
ATTN_KEY_BLOCK = 1280
ATTN_Q_BLOCK = 1024
ATTN_CHAIN_ROWS = 256
PEER_TOKEN_BLOCK = 1280
PEER_I_PER_STEP = 8
VMEM_LIMIT = 56 * 1024 * 1024


def _params(sem):
    return pltpu.CompilerParams(dimension_semantics=sem, vmem_limit_bytes=VMEM_LIMIT)


def _bf16_dot(a, b):
    return jnp.dot(a.astype(BF16), b.astype(BF16), preferred_element_type=F32)


def _split_dot(a, b):
    a_hi = a.astype(BF16)
    b_hi = b.astype(BF16)
    a_lo = (a - a_hi.astype(F32)).astype(BF16)
    b_lo = (b - b_hi.astype(F32)).astype(BF16)
    dot = functools.partial(jnp.dot, preferred_element_type=F32)
    return dot(a_hi, b_hi) + (dot(a_hi, b_lo) + dot(a_lo, b_hi))


def _bf16_dot_nt(a, b):
    return lax.dot_general(a.astype(BF16), b.astype(BF16), (((1,), (1,)), ((), ())),
                           preferred_element_type=F32)


def _ada_kernel(c_ref, w_ref, b_ref, o_ref):
    c = c_ref[...]
    a = c * jax.nn.sigmoid(c)
    o_ref[0] = jnp.dot(a, w_ref[0], preferred_element_type=F32, precision=HIGHEST) + b_ref[0]


def ada_modulation(cc, ada_w, ada_b):
    depth, d, n = ada_w.shape
    tn = 1536
    return pl.pallas_call(
        _ada_kernel,
        out_shape=jax.ShapeDtypeStruct((depth, SUBLANES, n), F32),
        grid=(depth, n // tn),
        in_specs=[pl.BlockSpec((SUBLANES, d), lambda l, j: (0, 0)),
                  pl.BlockSpec((1, d, tn), lambda l, j: (l, 0, j)),
                  pl.BlockSpec((1, 1, tn), lambda l, j: (l, 0, j))],
        out_specs=pl.BlockSpec((1, SUBLANES, tn), lambda l, j: (l, 0, j)),
        compiler_params=_params(("parallel", "parallel")),
        name="ada_modulation",
    )(cc, ada_w, ada_b.reshape(depth, 1, n))


def _mod_index(b, j):
    return (b, jnp.minimum(j, 1), 0, 0)


def _modulated_norm(x, nw, shift, scale):
    ms = jnp.mean(x * x, axis=-1, keepdims=True)
    y = x * lax.rsqrt(ms + EPS) * nw
    return y * (1.0 + scale) + shift


def _head_rmsnorm(x, w):
    return x * lax.rsqrt(jnp.mean(x * x, axis=-1, keepdims=True) + EPS) * w


def _inproj_kernel(x_ref, mod_ref, nw_ref, wm_ref, wg_ref, cos_ref, sina_ref, sinb_ref, qnw_ref, knw_ref,
                   qc_out, ql_out, k_out, v_out, dn_out, gate_out, ba_out):
    is_ctx = pl.program_id(1) == 0
    d = x_ref.shape[-1]
    x = x_ref[0]
    h = _modulated_norm(x, nw_ref[...], mod_ref[0, 0, :, 0:d], mod_ref[0, 0, :, d:2 * d]).astype(BF16)
    p = jnp.dot(h, wm_ref[...], preferred_element_type=F32)
    ba_out[0] = jnp.dot(h, wg_ref[...], preferred_element_type=F32)
    cos = cos_ref[...]
    sina = sina_ref[...]
    sinb = sinb_ref[...]

    def rope(t):
        return (t * cos + pltpu.roll(t, HEAD_DIM - HEAD_DIM // 4, 1) * sina
                + pltpu.roll(t, HEAD_DIM // 4, 1) * sinb)

    scale = HEAD_DIM ** -0.5 * math.log2(math.e)
    qs = [(rope(_head_rmsnorm(p[:, hd * HEAD_DIM:(hd + 1) * HEAD_DIM], qnw_ref[...])) * scale).astype(BF16)
          for hd in range(ATTN_HEADS)]

    @pl.when(is_ctx)
    def _():
        for hd in range(ATTN_HEADS):
            qc_out[0, hd] = qs[hd]

    @pl.when(jnp.logical_not(is_ctx))
    def _():
        for hd in range(ATTN_HEADS):
            ql_out[0, hd] = qs[hd]
    off = ATTN_Q_W
    for hd in range(ATTN_KV_HEADS):
        kh = _head_rmsnorm(p[:, off + hd * HEAD_DIM:off + (hd + 1) * HEAD_DIM], knw_ref[...])
        k_out[0, hd] = rope(kh).astype(BF16)
    off += ATTN_KV_W
    for hd in range(ATTN_KV_HEADS):
        v_out[0, hd, :, 0:HEAD_DIM] = p[:, off + hd * HEAD_DIM:off + (hd + 1) * HEAD_DIM].astype(BF16)
        v_out[0, hd, :, HEAD_DIM:2 * HEAD_DIM] = jnp.ones((x.shape[0], HEAD_DIM), BF16)
    off += ATTN_KV_W
    dn_out[0] = p[:, off:off + 3 * DN_W]
    off += 3 * DN_W
    gate_out[0] = p[:, off:off + DN_W]


def in_projection(xs, mod, norm_w, w_main, w_gate, cos, sina, sinb, qn_w, kn_w):
    bsz, s, d = xs.shape
    tm = TOKEN_BLOCK
    tok = lambda b, j: (b, j, 0)
    head_tok = lambda b, j: (b, 0, j, 0)
    full2 = lambda b, j: (0, 0)
    rope_spec = pl.BlockSpec((tm, HEAD_DIM), lambda b, j: (j, 0))
    return pl.pallas_call(
        _inproj_kernel,
        out_shape=(jax.ShapeDtypeStruct((bsz, ATTN_HEADS, tm, HEAD_DIM), BF16),
                   jax.ShapeDtypeStruct((bsz, ATTN_HEADS, s - tm, HEAD_DIM), BF16),
                   jax.ShapeDtypeStruct((bsz, ATTN_KV_HEADS, s, HEAD_DIM), BF16),
                   jax.ShapeDtypeStruct((bsz, ATTN_KV_HEADS, s, 2 * HEAD_DIM), BF16),
                   jax.ShapeDtypeStruct((bsz, s, 3 * DN_W), F32),
                   jax.ShapeDtypeStruct((bsz, s, DN_W), F32),
                   jax.ShapeDtypeStruct((bsz, s, LANES), F32)),
        grid=(bsz, s // tm),
        in_specs=[pl.BlockSpec((1, tm, d), tok),
                  pl.BlockSpec((1, 1, 1, mod.shape[-1]), _mod_index),
                  pl.BlockSpec((1, d), full2),
                  pl.BlockSpec((d, IN_MAIN_W), full2),
                  pl.BlockSpec((d, LANES), full2),
                  rope_spec, rope_spec, rope_spec,
                  pl.BlockSpec((1, HEAD_DIM), full2),
                  pl.BlockSpec((1, HEAD_DIM), full2)],
        out_specs=(pl.BlockSpec((1, ATTN_HEADS, tm, HEAD_DIM), lambda b, j: (b, 0, 0, 0)),
                   pl.BlockSpec((1, ATTN_HEADS, tm, HEAD_DIM), lambda b, j: (b, 0, jnp.maximum(j - 1, 0), 0)),
                   pl.BlockSpec((1, ATTN_KV_HEADS, tm, HEAD_DIM), head_tok),
                   pl.BlockSpec((1, ATTN_KV_HEADS, tm, 2 * HEAD_DIM), head_tok),
                   pl.BlockSpec((1, tm, 3 * DN_W), tok),
                   pl.BlockSpec((1, tm, DN_W), tok),
                   pl.BlockSpec((1, tm, LANES), tok)),
        compiler_params=_params(("parallel", "arbitrary")),
        name="in_projection",
    )(xs, mod, norm_w.reshape(1, d), w_main, w_gate, cos, sina, sinb,
      qn_w.reshape(1, HEAD_DIM), kn_w.reshape(1, HEAD_DIM))


def _attn_kernel(q_ref, k_ref, v_ref, o_ref, m_sc, acc_sc):
    ki = pl.program_id(2)
    nk = pl.num_programs(2)

    @pl.when(ki == 0)
    def _():
        m_sc[...] = jnp.full(m_sc.shape, NEG_BIG, F32)
        acc_sc[...] = jnp.zeros(acc_sc.shape, F32)

    tq = q_ref.shape[2]
    chains = [(hd, pl.ds(r0, ATTN_CHAIN_ROWS)) for hd in range(ATTN_HEADS)
              for r0 in range(0, tq, ATTN_CHAIN_ROWS)]
    scores = [lax.dot_general(q_ref[0, hd, rows, :], k_ref[0, hd // ATTN_GROUP], (((1,), (1,)), ((), ())),
                              preferred_element_type=F32) for hd, rows in chains]
    for (hd, rows), s in zip(chains, scores):
        m_prev = m_sc[hd, rows, :]
        m_new = jnp.maximum(m_prev, jnp.max(s, axis=-1, keepdims=True))
        p = jnp.exp2(s - m_new)
        alpha = jnp.exp2(m_prev - m_new)
        acc_sc[hd, rows, :] = alpha * acc_sc[hd, rows, :] + jnp.dot(
            p.astype(BF16), v_ref[0, hd // ATTN_GROUP], preferred_element_type=F32)
        m_sc[hd, rows, :] = m_new

    @pl.when(ki == nk - 1)
    def _():
        for hd in range(ATTN_HEADS):
            o_ref[0, :, hd * HEAD_DIM:(hd + 1) * HEAD_DIM] = (
                acc_sc[hd, :, 0:HEAD_DIM] / acc_sc[hd, :, HEAD_DIM:2 * HEAD_DIM]).astype(o_ref.dtype)


def _attention_call(q, k, v, tq, n_keys, tk, name):
    bsz = q.shape[0]
    n_q_blocks = q.shape[2] // tq
    return pl.pallas_call(
        _attn_kernel,
        out_shape=jax.ShapeDtypeStruct((bsz, n_q_blocks * tq, ATTN_Q_W), BF16),
        grid=(bsz, n_q_blocks, n_keys // tk),
        in_specs=[pl.BlockSpec((1, ATTN_HEADS, tq, HEAD_DIM), lambda b, i, j: (b, 0, i, 0)),
                  pl.BlockSpec((1, ATTN_KV_HEADS, tk, HEAD_DIM), lambda b, i, j: (b, 0, j, 0)),
                  pl.BlockSpec((1, ATTN_KV_HEADS, tk, 2 * HEAD_DIM), lambda b, i, j: (b, 0, j, 0))],
        out_specs=pl.BlockSpec((1, tq, ATTN_Q_W), lambda b, i, j: (b, i, 0)),
        scratch_shapes=[pltpu.VMEM((ATTN_HEADS, tq, 1), F32),
                        pltpu.VMEM((ATTN_HEADS, tq, 2 * HEAD_DIM), F32)],
        compiler_params=_params(("parallel", "parallel", "arbitrary")),
        name=name,
    )(q, k, v)


def attention(q_ctx, q_lat, k, v):
    ctx_len, n_lat, s = q_ctx.shape[2], q_lat.shape[2], k.shape[2]
    assert ctx_len % ATTN_CHAIN_ROWS == 0 and n_lat % ATTN_CHAIN_ROWS == 0
    tk = ATTN_KEY_BLOCK if s % ATTN_KEY_BLOCK == 0 else ATTN_CHAIN_ROWS
    tq = ATTN_Q_BLOCK if n_lat % ATTN_Q_BLOCK == 0 else ATTN_CHAIN_ROWS
    attn_ctx = _attention_call(q_ctx, k, v, ctx_len, ctx_len, ctx_len, "attention_ctx")
    attn_lat = _attention_call(q_lat, k, v, tq, s, tk, "attention")
    return attn_ctx, attn_lat


def _dn_prep_kernel(main_ref, prev_ref, next_ref, ba_ref, cw_ref, gp_ref,
                    w_out, u_out, qg_out, kdt_out, qk_out, dl_out, ext_sc, *, ctx_chunks):
    j = pl.program_id(1)
    nj = pl.num_programs(1)
    c = DN_CHUNK
    has_prev = (j != 0) & (j != ctx_chunks)
    has_next = (j != ctx_chunks - 1) & (j != nj - 1)
    ext_sc[0:SUBLANES] = jnp.where(has_prev, prev_ref[0], 0.0)
    ext_sc[SUBLANES:SUBLANES + c] = main_ref[0]
    ext_sc[SUBLANES + c:2 * SUBLANES + c] = jnp.where(has_next, next_ref[0], 0.0)
    y = ext_sc[SUBLANES - CONV_PAD:SUBLANES - CONV_PAD + c] * cw_ref[0:1]
    for t in range(1, CONV_K):
        y = y + ext_sc[SUBLANES - CONV_PAD + t:SUBLANES - CONV_PAD + t + c] * cw_ref[t:t + 1]
    y = y * jax.nn.sigmoid(y)

    ba = ba_ref[0]
    beta_all = jax.nn.sigmoid(ba)
    g_all = -jnp.exp(gp_ref[0:1]) * jax.nn.softplus(ba + gp_ref[1:2])
    row = lax.broadcasted_iota(jnp.int32, (c, c), 0)
    col = lax.broadcasted_iota(jnp.int32, (c, c), 1)
    lower = (row >= col).astype(BF16)
    g_hi = g_all.astype(BF16)
    g_r1 = g_all - g_hi.astype(F32)
    g_mid = g_r1.astype(BF16)
    g_lo = (g_r1 - g_mid.astype(F32)).astype(BF16)
    dotf = functools.partial(jnp.dot, preferred_element_type=F32)
    prefix = dotf(lower, g_hi) + (dotf(lower, g_mid) + dotf(lower, g_lo))
    total = prefix[c - 1:c]
    gc = (prefix, total - prefix + g_all)
    gct = (gc[0].T, gc[1].T)
    row2 = lax.broadcasted_iota(jnp.int32, (2 * c, 2 * c), 0)
    col2 = lax.broadcasted_iota(jnp.int32, (2 * c, 2 * c), 1)
    eye2 = (row2 == col2).astype(F32)
    zero = jnp.zeros((c, c), F32)

    heads = range(DN_HEADS)
    a2, rhs2 = [], []
    for hd in heads:
        q = y[:, hd * HEAD_DIM:(hd + 1) * HEAD_DIM]
        k = y[:, DN_W + hd * HEAD_DIM:DN_W + (hd + 1) * HEAD_DIM]
        v = y[:, 2 * DN_W + hd * HEAD_DIM:2 * DN_W + (hd + 1) * HEAD_DIM]
        q = q * lax.rsqrt(jnp.sum(q * q, axis=-1, keepdims=True) + EPS) * (HEAD_DIM ** -0.5)
        k = k * lax.rsqrt(jnp.sum(k * k, axis=-1, keepdims=True) + EPS)
        kk = _bf16_dot_nt(k, k)
        qk = _bf16_dot_nt(q, k)
        a_dir, rhs_dir = [], []
        for dr in range(2):
            cb = dr * DN_HEADS + hd
            cg = 2 * DN_HEADS + cb
            beta = beta_all[:, cb:cb + 1]
            gcol = gc[dr][:, cg:cg + 1]
            grow = gct[dr][cg:cg + 1, :]
            incl = (row >= col) if dr == 0 else (row <= col)
            strict = (row > col) if dr == 0 else (row < col)
            decay = jnp.exp(jnp.where(incl, gcol - grow, NEG_BIG))
            a_dir.append(jnp.where(strict, kk * beta * decay, 0.0))
            eg = jnp.exp(gcol)
            rhs_dir.append(jnp.concatenate([v * beta, k * (beta * eg)], axis=1))
            qg_out[0, dr, hd] = (q * eg).astype(BF16)
            tot = total[:, cg:cg + 1]
            kdt_out[0, dr, hd] = (k * jnp.exp(tot - gcol)).T.astype(BF16)
            qk_out[0, dr, hd] = (qk * decay).astype(BF16)
            dl_out[0, dr, hd, 0] = jnp.broadcast_to(jnp.exp(tot), (1, LANES))
        a2.append(jnp.concatenate([jnp.concatenate([a_dir[0], zero], axis=1),
                                   jnp.concatenate([zero, a_dir[1]], axis=1)], axis=0))
        rhs2.append(jnp.concatenate(rhs_dir, axis=0))
    base = DN_BASE_BLOCK
    base_mask = (row2 // base) == (col2 // base)
    a_base = [jnp.where(base_mask, a2[hd], 0.0) for hd in heads]
    x = [eye2 - a_base[hd] for hd in heads]
    pw = [_split_dot(a_base[hd], a_base[hd]) for hd in heads]
    for it in range(int(math.log2(base)) - 1):
        x = [x[hd] + _split_dot(x[hd], pw[hd]) for hd in heads]
        if it < int(math.log2(base)) - 2:
            pw = [_split_dot(pw[hd], pw[hd]) for hd in heads]
    blk = base
    while blk < c:
        sibling = ((row2 // (2 * blk)) == (col2 // (2 * blk))) & ((row2 // blk) != (col2 // blk))
        fold = [_bf16_dot(x[hd], jnp.where(sibling, a2[hd], 0.0)) for hd in heads]
        x = [x[hd] - _bf16_dot(fold[hd], x[hd]) for hd in heads]
        blk *= 2
    for hd in heads:
        uw = _bf16_dot(x[hd], rhs2[hd])
        for dr in range(2):
            u_out[0, dr, hd] = uw[dr * c:(dr + 1) * c, 0:HEAD_DIM]
            w_out[0, dr, hd] = uw[dr * c:(dr + 1) * c, HEAD_DIM:2 * HEAD_DIM].astype(BF16)


def dn_prepare(dnqkv, ba, conv_w, gate_par, ctx_len):
    bsz, s, wdt = dnqkv.shape
    c = DN_CHUNK
    nc = s // c
    rows8 = s // SUBLANES
    per = c // SUBLANES
    chain = lambda b, j: (b, 0, 0, j, 0)
    return pl.pallas_call(
        functools.partial(_dn_prep_kernel, ctx_chunks=ctx_len // c),
        out_shape=(jax.ShapeDtypeStruct((bsz, 2, DN_HEADS, s, HEAD_DIM), BF16),
                   jax.ShapeDtypeStruct((bsz, 2, DN_HEADS, s, HEAD_DIM), F32),
                   jax.ShapeDtypeStruct((bsz, 2, DN_HEADS, s, HEAD_DIM), BF16),
                   jax.ShapeDtypeStruct((bsz, 2, DN_HEADS, HEAD_DIM, s), BF16),
                   jax.ShapeDtypeStruct((bsz, 2, DN_HEADS, s, c), BF16),
                   jax.ShapeDtypeStruct((bsz, 2, DN_HEADS, nc, 1, LANES), F32)),
        grid=(bsz, nc),
        in_specs=[pl.BlockSpec((1, c, wdt), lambda b, j: (b, j, 0)),
                  pl.BlockSpec((1, SUBLANES, wdt), lambda b, j: (b, jnp.maximum(j * per - 1, 0), 0)),
                  pl.BlockSpec((1, SUBLANES, wdt), lambda b, j: (b, jnp.minimum((j + 1) * per, rows8 - 1), 0)),
                  pl.BlockSpec((1, c, LANES), lambda b, j: (b, j, 0)),
                  pl.BlockSpec((SUBLANES, wdt), lambda b, j: (0, 0)),
                  pl.BlockSpec((SUBLANES, LANES), lambda b, j: (0, 0))],
        out_specs=(pl.BlockSpec((1, 2, DN_HEADS, c, HEAD_DIM), chain),
                   pl.BlockSpec((1, 2, DN_HEADS, c, HEAD_DIM), chain),
                   pl.BlockSpec((1, 2, DN_HEADS, c, HEAD_DIM), chain),
                   pl.BlockSpec((1, 2, DN_HEADS, HEAD_DIM, c), lambda b, j: (b, 0, 0, 0, j)),
                   pl.BlockSpec((1, 2, DN_HEADS, c, c), chain),
                   pl.BlockSpec((1, 2, DN_HEADS, 1, 1, LANES), lambda b, j: (b, 0, 0, j, 0, 0))),
        scratch_shapes=[pltpu.VMEM((c + 2 * SUBLANES, wdt), F32)],
        compiler_params=_params(("parallel", "parallel")),
        name="dn_prepare",
    )(dnqkv, dnqkv, dnqkv, ba, conv_w, gate_par)


def _dn_scan_kernel(*refs, bsz):
    ins = refs[:12]
    of_ref, ob_ref, s_sc = refs[12:]
    n = pl.program_id(0)

    @pl.when(n == 0)
    def _():
        s_sc[...] = jnp.zeros(s_sc.shape, F32)

    chains = [(dr, b, hd) for dr in range(2) for b in range(bsz) for hd in range(DN_HEADS)]
    dotf = functools.partial(jnp.dot, preferred_element_type=F32)

    def inp(dr, k):
        return ins[dr * 6 + k]

    state = [s_sc[ci] for ci in range(len(chains))]
    r = [dotf(jnp.concatenate([inp(dr, 0)[b, 0, hd], inp(dr, 2)[b, 0, hd]], axis=0), state[ci].astype(BF16))
         for ci, (dr, b, hd) in enumerate(chains)]
    v_new = [(inp(dr, 1)[b, 0, hd] - r[ci][0:DN_CHUNK]).astype(BF16) for ci, (dr, b, hd) in enumerate(chains)]
    intra = [dotf(inp(dr, 4)[b, 0, hd], v_new[ci]) for ci, (dr, b, hd) in enumerate(chains)]
    upd = [dotf(inp(dr, 3)[b, 0, hd], v_new[ci]) for ci, (dr, b, hd) in enumerate(chains)]
    for ci, (dr, b, hd) in enumerate(chains):
        o_ref = of_ref if dr == 0 else ob_ref
        o_ref[b, :, hd * HEAD_DIM:(hd + 1) * HEAD_DIM] = r[ci][DN_CHUNK:] + intra[ci]
        s_sc[ci] = state[ci] * inp(dr, 5)[b, 0, hd, 0] + upd[ci]


def dn_scan(w, u, qg, kdt, qk, dl, ctx_len):
    bsz, _, _, s, _ = w.shape
    c = DN_CHUNK
    nc = s // c
    cc = ctx_len // c

    def bwd_chunk(n):
        return jnp.where(n < cc, cc - 1 - n, nc - 1 - (n - cc))

    in_specs, args = [], []
    for dr in range(2):
        pos = (lambda n: n) if dr == 0 else bwd_chunk
        tokm = lambda n, dr=dr, pos=pos: (0, dr, 0, pos(n), 0)
        for arr in (w, u, qg):
            in_specs.append(pl.BlockSpec((bsz, 1, DN_HEADS, c, HEAD_DIM), tokm))
            args.append(arr)
        in_specs.append(pl.BlockSpec((bsz, 1, DN_HEADS, HEAD_DIM, c), lambda n, dr=dr, pos=pos: (0, dr, 0, 0, pos(n))))
        args.append(kdt)
        in_specs.append(pl.BlockSpec((bsz, 1, DN_HEADS, c, c), tokm))
        args.append(qk)
        in_specs.append(pl.BlockSpec((bsz, 1, DN_HEADS, 1, 1, LANES), lambda n, dr=dr, pos=pos: (0, dr, 0, pos(n), 0, 0)))
        args.append(dl)
    return pl.pallas_call(
        functools.partial(_dn_scan_kernel, bsz=bsz),
        out_shape=(jax.ShapeDtypeStruct((bsz, s, DN_W), F32), jax.ShapeDtypeStruct((bsz, s, DN_W), F32)),
        grid=(nc,),
        in_specs=in_specs,
        out_specs=(pl.BlockSpec((bsz, c, DN_W), lambda n: (0, n, 0)),
                   pl.BlockSpec((bsz, c, DN_W), lambda n: (0, bwd_chunk(n), 0))),
        scratch_shapes=[pltpu.VMEM((2 * bsz * DN_HEADS, HEAD_DIM, HEAD_DIM), F32)],
        compiler_params=_params(("arbitrary",)),
        name="dn_scan",
    )(*args)


def _outproj_kernel(x_ref, mod_ref, attn_ctx_ref, attn_lat_ref, of_ref, ob_ref, gate_ref, dnw_ref, wo_ref,
                    n2w_ref, x_out, h2_out):
    d = x_ref.shape[-1]
    o = of_ref[0] + ob_ref[0]
    gate = gate_ref[0]
    parts = [jnp.where(pl.program_id(1) == 0, attn_ctx_ref[0], attn_lat_ref[0])]
    for hd in range(DN_HEADS):
        sl = slice(hd * HEAD_DIM, (hd + 1) * HEAD_DIM)
        g = gate[:, sl]
        parts.append((_head_rmsnorm(o[:, sl], dnw_ref[...]) * (g * jax.nn.sigmoid(g))).astype(BF16))
    mix = jnp.concatenate(parts, axis=1)
    y = jnp.dot(mix, wo_ref[...], preferred_element_type=F32)
    x = x_ref[0] + mod_ref[0, 0, :, 2 * d:3 * d] * y
    x_out[0] = x
    h2_out[0] = _modulated_norm(x, n2w_ref[...], mod_ref[0, 0, :, 3 * d:4 * d],
                                mod_ref[0, 0, :, 4 * d:5 * d]).astype(BF16)


def out_projection(xs, mod, attn_ctx, attn_lat, o_f, o_b, gate, dn_norm_w, w_out, norm2_w):
    bsz, s, d = xs.shape
    tm = TOKEN_BLOCK
    assert attn_ctx.shape[1] == tm
    tok = lambda b, j: (b, j, 0)
    full2 = lambda b, j: (0, 0)
    return pl.pallas_call(
        _outproj_kernel,
        out_shape=(jax.ShapeDtypeStruct((bsz, s, d), F32), jax.ShapeDtypeStruct((bsz, s, d), BF16)),
        grid=(bsz, s // tm),
        in_specs=[pl.BlockSpec((1, tm, d), tok),
                  pl.BlockSpec((1, 1, 1, mod.shape[-1]), _mod_index),
                  pl.BlockSpec((1, tm, ATTN_Q_W), lambda b, j: (b, 0, 0)),
                  pl.BlockSpec((1, tm, ATTN_Q_W), lambda b, j: (b, jnp.maximum(j - 1, 0), 0)),
                  pl.BlockSpec((1, tm, DN_W), tok),
                  pl.BlockSpec((1, tm, DN_W), tok),
                  pl.BlockSpec((1, tm, DN_W), tok),
                  pl.BlockSpec((1, HEAD_DIM), full2),
                  pl.BlockSpec(w_out.shape, full2),
                  pl.BlockSpec((1, d), full2)],
        out_specs=(pl.BlockSpec((1, tm, d), tok), pl.BlockSpec((1, tm, d), tok)),
        compiler_params=_params(("parallel", "parallel")),
        name="out_projection",
    )(xs, mod, attn_ctx, attn_lat, o_f, o_b, gate, dn_norm_w.reshape(1, HEAD_DIM), w_out,
      norm2_w.reshape(1, d))


def _peer_score_kernel(h2_ref, wq_ref, sk_ref, st_out):
    q = jnp.dot(h2_ref[0], wq_ref[...], preferred_element_type=F32).astype(BF16)
    for hp in range(2 * PEER_HEADS):
        st_out[0, hp] = lax.dot_general(sk_ref[hp], q[:, hp * PEER_HALF:(hp + 1) * PEER_HALF],
                                        (((1,), (1,)), ((), ())), preferred_element_type=F32)


def peer_scores(h2, wq, subkeys):
    bsz, s, d = h2.shape
    tm = TOKEN_BLOCK
    nhp = 2 * PEER_HEADS
    return pl.pallas_call(
        _peer_score_kernel,
        out_shape=jax.ShapeDtypeStruct((bsz, nhp, N_KEYS, s), F32),
        grid=(bsz, s // tm),
        in_specs=[pl.BlockSpec((1, tm, d), lambda b, j: (b, j, 0)),
                  pl.BlockSpec(wq.shape, lambda b, j: (0, 0)),
                  pl.BlockSpec(subkeys.shape, lambda b, j: (0, 0, 0))],
        out_specs=pl.BlockSpec((1, nhp, N_KEYS, tm), lambda b, j: (b, 0, 0, j)),
        compiler_params=_params(("parallel", "parallel")),
        name="peer_scores",
    )(h2, wq, subkeys)


def _sorting_network(n):
    pairs = []
    p = 1
    while p < n:
        k = p
        while k >= 1:
            for j in range(k % p, n - k, 2 * k):
                for i in range(min(k, n - j - k)):
                    if (i + j) // (2 * p) == (i + j + k) // (2 * p):
                        pairs.append((i + j, i + j + k))
            k //= 2
        p *= 2
    return pairs


def _sorted_top(s, k, with_rank):
    n_tiles = s.shape[0] // SUBLANES
    v = [s[i * SUBLANES:(i + 1) * SUBLANES] for i in range(n_tiles)]
    for lo, hi in _sorting_network(n_tiles):
        v[lo], v[hi] = jnp.maximum(v[lo], v[hi]), jnp.minimum(v[lo], v[hi])
    out = []
    for r in range(k):
        m = jnp.max(v[0], axis=0, keepdims=True)
        out.append(m)
        hit = v[0] == m
        for i in range(min(n_tiles, k - 1 - r)):
            v[i] = jnp.where(hit, v[i + 1] if i + 1 < n_tiles else NEG_BIG, v[i])
    rank = None
    if with_rank:
        rank = jnp.full(s.shape, float(k), F32)
        for r in reversed(range(k)):
            rank = jnp.where(s >= out[r], float(r), rank)
    return out, rank


def _paired_bf16_words(x):
    w = pltpu.bitcast(x.astype(BF16).astype(F32), jnp.uint32)
    return w | (w >> 16)


def _peer_topk_kernel(st_ref, cnt_out, e0_out, rank_out, e1_out):
    nt = PEER_TOPK + 1
    tops = ([], [])
    for hd in range(PEER_HEADS):
        a, _ = _sorted_top(st_ref[0, 2 * hd], nt, False)
        b, rank = _sorted_top(st_ref[0, 2 * hd + 1], nt, True)
        rank_out[0, hd] = rank.astype(BF16)
        tops[0].append(a)
        tops[1].append(b)
    a8 = [jnp.concatenate([tops[0][hd][r] for hd in range(PEER_HEADS)], axis=0) for r in range(nt)]
    b8 = [jnp.concatenate([tops[1][hd][r] for hd in range(PEER_HEADS)], axis=0) for r in range(nt)]
    cand = [a8[i] + b8[j] for i in range(nt) for j in range(nt) if (i + 1) * (j + 1) <= nt]
    top = []
    for _ in range(nt):
        m = functools.reduce(jnp.maximum, cand)
        top.append(m)
        cand = [jnp.where(t == m, NEG_BIG, t) for t in cand]
    tau8 = 0.5 * (top[PEER_TOPK - 1] + top[PEER_TOPK])
    smax = a8[0] + b8[0]
    z8 = jnp.exp(top[0] - smax)
    for t in top[1:PEER_TOPK]:
        z8 = z8 + jnp.exp(t - smax)
    rz8 = 1.0 / z8
    for hd in range(PEER_HEADS):
        s0 = st_ref[0, 2 * hd]
        s1 = st_ref[0, 2 * hd + 1]
        thr = tau8[hd:hd + 1] - s0
        cnt = jnp.zeros(s0.shape, F32)
        for r in range(nt):
            cnt = jnp.where(tops[1][hd][r] > thr, float(r + 1), cnt)
        cnt_out[0, hd] = _paired_bf16_words(cnt)
        e0_out[0, hd] = _paired_bf16_words(jnp.exp(s0 - tops[0][hd][0]))
        e1_out[0, hd] = (jnp.exp(s1 - tops[1][hd][0]) * (rz8[hd:hd + 1] * GELU_GATE_SCALE)).astype(BF16)


def peer_topk(st):
    bsz, nhp, nk, s = st.shape
    tl = LANES
    spec = pl.BlockSpec((1, PEER_HEADS, nk, tl), lambda b, j: (b, 0, 0, j))
    words = jax.ShapeDtypeStruct((bsz, PEER_HEADS, nk, s), jnp.uint32)
    halfs = jax.ShapeDtypeStruct((bsz, PEER_HEADS, nk, s), BF16)
    return pl.pallas_call(
        _peer_topk_kernel,
        out_shape=(words, words, halfs, halfs),
        grid=(bsz, s // tl),
        in_specs=[pl.BlockSpec((1, nhp, nk, tl), lambda b, j: (b, 0, 0, j))],
        out_specs=(spec, spec, spec, spec),
        compiler_params=_params(("parallel", "parallel")),
        name="peer_topk",
    )(st)


def _peer_expert_kernel(x_ref, mod_ref, h2_ref, u_ref, vt_ref, cnt_ref, e0_ref, rank_ref, e1_ref,
                        x_out, acc_sc, *, ctx_len):
    tok_block = pl.program_id(1)
    ec = pl.program_id(2)
    n_ec = pl.num_programs(2)
    d = x_ref.shape[-1]
    tb = x_ref.shape[1]
    pk = 2 * SUBLANES

    @pl.when(ec == 0)
    def _():
        acc_sc[...] = jnp.zeros(acc_sc.shape, F32)

    def row_tile(ref, hd, ii):
        return pltpu.bitcast(jnp.broadcast_to(ref[0, hd, ii:ii + 1, :], (SUBLANES, tb)), BF16)

    h2 = h2_ref[0]
    pair = 2 * N_KEYS
    n_pairs = PEER_I_PER_STEP // 2

    def activations(p):
        return lax.dot_general(u_ref[p * pair:(p + 1) * pair, :], h2, (((1,), (1,)), ((), ())),
                               preferred_element_type=F32)

    def gate_weights(p):
        tiles = []
        for ii in (2 * p, 2 * p + 1):
            wt = [jnp.zeros((pk, tb), BF16) for _ in range(N_KEYS // pk)]
            for hd in range(PEER_HEADS):
                cnt = row_tile(cnt_ref, hd, ii)
                e0 = row_tile(e0_ref, hd, ii)
                for rt in range(N_KEYS // pk):
                    rows = slice(rt * pk, (rt + 1) * pk)
                    sel = jnp.where(rank_ref[0, hd, rows, :] < cnt, e1_ref[0, hd, rows, :],
                                    jnp.zeros((), BF16))
                    wt[rt] = wt[rt] + sel * e0
            tiles += wt
        return tiles

    def gated(at2, tiles):
        g = []
        for rt, wt in enumerate(tiles):
            a = at2[rt * pk:(rt + 1) * pk]
            act = a * (1.0 + lax.erf(a))
            g.append(act.astype(BF16) * wt)
        return jnp.concatenate(g, axis=0)

    g = [gated(activations(p), gate_weights(p)) for p in range(n_pairs)]
    acc_sc[...] += jnp.dot(vt_ref[...], jnp.concatenate(g, axis=0), preferred_element_type=F32)

    @pl.when(ec == n_ec - 1)
    def _():
        tok = tok_block * tb + lax.broadcasted_iota(jnp.int32, (tb, 1), 0)
        g2 = jnp.where(tok < ctx_len, mod_ref[0, 0, :, 5 * d:6 * d], mod_ref[0, 1, :, 5 * d:6 * d])
        x_out[0] = x_ref[0] + g2 * acc_sc[...].T


def peer_experts(xs, mod, h2, u_tab, vt_tab, cnt, e0, rank, e1, ctx_len):
    bsz, s, d = xs.shape
    tb = PEER_TOKEN_BLOCK if s % PEER_TOKEN_BLOCK == 0 else TOKEN_BLOCK
    ech = PEER_I_PER_STEP * N_KEYS
    n_exp = u_tab.shape[0]
    tok = lambda b, j, e: (b, j, 0)
    per_tok = lambda b, j, e: (b, 0, 0, j)
    per_i = lambda b, j, e: (b, 0, e, j)
    once = dict(pipeline_mode=pl.Buffered(1))
    return pl.pallas_call(
        functools.partial(_peer_expert_kernel, ctx_len=ctx_len),
        out_shape=jax.ShapeDtypeStruct((bsz, s, d), F32),
        grid=(bsz, s // tb, n_exp // ech),
        in_specs=[pl.BlockSpec((1, tb, d), tok, **once),
                  pl.BlockSpec((1, 2, 1, mod.shape[-1]), lambda b, j, e: (b, 0, 0, 0)),
                  pl.BlockSpec((1, tb, d), tok, **once),
                  pl.BlockSpec((ech, d), lambda b, j, e: (e, 0)),
                  pl.BlockSpec((d, ech), lambda b, j, e: (0, e)),
                  pl.BlockSpec((1, PEER_HEADS, PEER_I_PER_STEP, tb), per_i),
                  pl.BlockSpec((1, PEER_HEADS, PEER_I_PER_STEP, tb), per_i),
                  pl.BlockSpec((1, PEER_HEADS, N_KEYS, tb), per_tok, **once),
                  pl.BlockSpec((1, PEER_HEADS, N_KEYS, tb), per_tok, **once)],
        out_specs=pl.BlockSpec((1, tb, d), tok, **once),
        scratch_shapes=[pltpu.VMEM((d, tb), F32)],
        compiler_params=_params(("parallel", "parallel", "arbitrary")),
        name="peer_experts",
    )(xs, mod, h2, u_tab, vt_tab, cnt, e0, rank, e1)


def _rope_tables(ctx_len, n_lat):
    rows = n_lat // GRID_W
    row = jnp.repeat(jnp.arange(rows, dtype=F32), GRID_W)
    col = jnp.tile(jnp.arange(GRID_W, dtype=F32), rows)
    axis_dim = HEAD_DIM // 2
    inv_freq = ROPE_THETA ** (-jnp.arange(0, axis_dim, 2, dtype=F32) / axis_dim)
    ang_r = row[:, None] * inv_freq[None, :]
    ang_c = col[:, None] * inv_freq[None, :]
    ang = jnp.concatenate([ang_r, ang_r, ang_c, ang_c], axis=-1)
    cos, sin = jnp.cos(ang), jnp.sin(ang)
    first = (jnp.arange(HEAD_DIM) % (HEAD_DIM // 2)) < (HEAD_DIM // 4)
    sina = jnp.where(first, -sin, 0.0)
    sinb = jnp.where(first, 0.0, sin)
    pad = lambda t, v: jnp.concatenate([jnp.full((ctx_len, HEAD_DIM), v, F32), t], axis=0)
    return pad(cos, 1.0), pad(sina, 0.0), pad(sinb, 0.0)


def kernel(x, c, ctx, c_ctx, ada_w, ada_b, norm1_w, norm2_w, w_in, attn_qnorm_w, attn_knorm_w, dn_conv_w,
           dn_A_log, dn_dt_bias, dn_norm_w, w_out, peer_wq, peer_subkeys, peer_u, peer_v):
    bsz, n_lat, d = x.shape
    ctx_len = ctx.shape[1]
    depth = ada_w.shape[0]
    assert ctx_len == TOKEN_BLOCK and n_lat % TOKEN_BLOCK == 0 and bsz + 1 <= SUBLANES
    assert w_in.shape[-1] == IN_MAIN_W + N_GATE_COLS

    xs = jnp.concatenate([ctx, x], axis=1)
    cos, sina, sinb = _rope_tables(ctx_len, n_lat)

    cc = jnp.zeros((SUBLANES, d), F32).at[:bsz].set(c).at[bsz].set(c_ctx)
    mod_all = ada_modulation(cc, ada_w, ada_b)
    mod_ctx = jnp.broadcast_to(mod_all[:, bsz][:, None], (depth, bsz, 6 * d))
    mod = jnp.stack([mod_ctx, mod_all[:, :bsz]], axis=2)[:, :, :, None, :]

    for l in range(depth):
        w_main = w_in[l, :, :IN_MAIN_W].astype(BF16)
        w_gate = jnp.pad(w_in[l, :, IN_MAIN_W:], ((0, 0), (0, LANES - N_GATE_COLS))).astype(BF16)
        q_ctx, q_lat, k, v, dnqkv, gate, ba = in_projection(xs, mod[l], norm1_w[l], w_main, w_gate, cos, sina,
                                                            sinb, attn_qnorm_w[l], attn_knorm_w[l])
        attn_ctx, attn_lat = attention(q_ctx, q_lat, k, v)
        conv_w = jnp.pad(dn_conv_w[l], ((0, SUBLANES - CONV_K), (0, 0)))
        gate_par = jnp.zeros((SUBLANES, LANES), F32)
        gate_par = gate_par.at[0, 2 * DN_HEADS:4 * DN_HEADS].set(dn_A_log[l].reshape(-1))
        gate_par = gate_par.at[1, 2 * DN_HEADS:4 * DN_HEADS].set(dn_dt_bias[l].reshape(-1))
        o_f, o_b = dn_scan(*dn_prepare(dnqkv, ba, conv_w, gate_par, ctx_len), ctx_len)
        xs, h2 = out_projection(xs, mod[l], attn_ctx, attn_lat, o_f, o_b, gate, dn_norm_w[l],
                                w_out[l].astype(BF16), norm2_w[l])
        sk = peer_subkeys[l].reshape(2 * PEER_HEADS, N_KEYS, PEER_HALF).astype(BF16)
        st = peer_scores(h2, peer_wq[l].astype(BF16), sk)
        cnt, e0, rank, e1 = peer_topk(st)
        xs = peer_experts(xs, mod[l], h2, (peer_u[l] * GELU_GATE_SCALE).astype(BF16), peer_v[l].T.astype(BF16), cnt, e0, rank, e1,
                          ctx_len)
    return xs[:, ctx_len:]
```

```python
import functools
import math

import jax
import jax.numpy as jnp
from jax import lax
from jax.experimental import pallas as pl
from jax.experimental.pallas import tpu as pltpu

F32 = jnp.float32
BF16 = jnp.bfloat16
HIGHEST = lax.Precision.HIGHEST

HEAD_DIM = 128
ATTN_HEADS = 4
ATTN_KV_HEADS = 2
ATTN_GROUP = ATTN_HEADS // ATTN_KV_HEADS
DN_HEADS = 4
ATTN_Q_W = ATTN_HEADS * HEAD_DIM
ATTN_KV_W = ATTN_KV_HEADS * HEAD_DIM
DN_W = DN_HEADS * HEAD_DIM
IN_MAIN_W = ATTN_Q_W + 2 * ATTN_KV_W + 3 * DN_W + DN_W
N_GATE_COLS = 4 * DN_HEADS
ROPE_THETA = 10000.0
GRID_W = 64
CONV_K = 5
CONV_PAD = CONV_K // 2
PEER_HEADS = 8
PEER_HALF = 128
N_KEYS = 128
PEER_TOPK = 16
EPS = 1e-6
NEG_BIG = -1e30
GELU_GATE_SCALE = 2.0 ** -0.5

LANES = 128
SUBLANES = 8
MXU_TILE = 256
TOKEN_BLOCK = 256
DN_CHUNK = 128
DN_BASE_BLOCK = 4
ATTN_KEY_BLOCK = 1280
ATTN_Q_BLOCK = 1024
ATTN_CHAIN_ROWS = 256
PEER_TOKEN_BLOCK = 1280
PEER_I_PER_STEP = 8
VMEM_LIMIT = 56 * 1024 * 1024


def _params(sem):
    return pltpu.CompilerParams(dimension_semantics=sem, vmem_limit_bytes=VMEM_LIMIT)


def _bf16_dot(a, b):
    return jnp.dot(a.astype(BF16), b.astype(BF16), preferred_element_type=F32)


def _split_dot(a, b):
    a_hi = a.astype(BF16)
    b_hi = b.astype(BF16)
    a_lo = (a - a_hi.astype(F32)).astype(BF16)
    b_lo = (b - b_hi.astype(F32)).astype(BF16)
    dot = functools.partial(jnp.dot, preferred_element_type=F32)
    return dot(a_hi, b_hi) + (dot(a_hi, b_lo) + dot(a_lo, b_hi))


def _bf16_dot_nt(a, b):
    return lax.dot_general(a.astype(BF16), b.astype(BF16), (((1,), (1,)), ((), ())),
                           preferred_element_type=F32)


def _ada_kernel(c_ref, w_ref, b_ref, o_ref):
    c = c_ref[...]
    a = c * jax.nn.sigmoid(c)
    o_ref[0] = jnp.dot(a, w_ref[0], preferred_element_type=F32, precision=HIGHEST) + b_ref[0]


def ada_modulation(cc, ada_w, ada_b):
    depth, d, n = ada_w.shape
    tn = 1536
    return pl.pallas_call(
        _ada_kernel,
        out_shape=jax.ShapeDtypeStruct((depth, SUBLANES, n), F32),
        grid=(depth, n // tn),
        in_specs=[pl.BlockSpec((SUBLANES, d), lambda l, j: (0, 0)),
                  pl.BlockSpec((1, d, tn), lambda l, j: (l, 0, j)),
                  pl.BlockSpec((1, 1, tn), lambda l, j: (l, 0, j))],
        out_specs=pl.BlockSpec((1, SUBLANES, tn), lambda l, j: (l, 0, j)),
        compiler_params=_params(("parallel", "parallel")),
        name="ada_modulation",
    )(cc, ada_w, ada_b.reshape(depth, 1, n))


def _mod_index(b, j):
    return (b, jnp.minimum(j, 1), 0, 0)


def _modulated_norm(x, nw, shift, scale):
    ms = jnp.mean(x * x, axis=-1, keepdims=True)
    y = x * lax.rsqrt(ms + EPS) * nw
    return y * (1.0 + scale) + shift


def _head_rmsnorm(x, w):
    return x * lax.rsqrt(jnp.mean(x * x, axis=-1, keepdims=True) + EPS) * w


def _inproj_kernel(x_ref, mod_ref, nw_ref, wm_ref, wg_ref, cos_ref, sina_ref, sinb_ref, qnw_ref, knw_ref,
                   q_out, k_out, v_out, dn_out, gate_out, ba_out):
    d = x_ref.shape[-1]
    x = x_ref[0]
    h = _modulated_norm(x, nw_ref[...], mod_ref[0, 0, :, 0:d], mod_ref[0, 0, :, d:2 * d]).astype(BF16)
    p = jnp.dot(h, wm_ref[...], preferred_element_type=F32)
    ba_out[0] = jnp.dot(h, wg_ref[...], preferred_element_type=F32)
    cos = cos_ref[...]
    sina = sina_ref[...]
    sinb = sinb_ref[...]

    def rope(t):
        return (t * cos + pltpu.roll(t, HEAD_DIM - HEAD_DIM // 4, 1) * sina
                + pltpu.roll(t, HEAD_DIM // 4, 1) * sinb)

    scale = HEAD_DIM ** -0.5 * math.log2(math.e)
    for hd in range(ATTN_HEADS):
        qh = _head_rmsnorm(p[:, hd * HEAD_DIM:(hd + 1) * HEAD_DIM], qnw_ref[...])
        q_out[0, hd] = (rope(qh) * scale).astype(BF16)
    off = ATTN_Q_W
    for hd in range(ATTN_KV_HEADS):
        kh = _head_rmsnorm(p[:, off + hd * HEAD_DIM:off + (hd + 1) * HEAD_DIM], knw_ref[...])
        k_out[0, hd] = rope(kh).astype(BF16)
    off += ATTN_KV_W
    for hd in range(ATTN_KV_HEADS):
        v_out[0, hd, :, 0:HEAD_DIM] = p[:, off + hd * HEAD_DIM:off + (hd + 1) * HEAD_DIM].astype(BF16)
        v_out[0, hd, :, HEAD_DIM:2 * HEAD_DIM] = jnp.ones((x.shape[0], HEAD_DIM), BF16)
    off += ATTN_KV_W
    dn_out[0] = p[:, off:off + 3 * DN_W]
    off += 3 * DN_W
    gate_out[0] = p[:, off:off + DN_W]


def in_projection(xs, mod, norm_w, w_main, w_gate, cos, sina, sinb, qn_w, kn_w, q_lead_blocks):
    bsz, s, d = xs.shape
    tm = TOKEN_BLOCK
    tok = lambda b, j: (b, j, 0)
    head_tok = lambda b, j: (b, 0, j, 0)
    full2 = lambda b, j: (0, 0)
    rope_spec = pl.BlockSpec((tm, HEAD_DIM), lambda b, j: (j, 0))
    return pl.pallas_call(
        _inproj_kernel,
        out_shape=(jax.ShapeDtypeStruct((bsz, ATTN_HEADS, q_lead_blocks * tm + s, HEAD_DIM), BF16),
                   jax.ShapeDtypeStruct((bsz, ATTN_KV_HEADS, s, HEAD_DIM), BF16),
                   jax.ShapeDtypeStruct((bsz, ATTN_KV_HEADS, s, 2 * HEAD_DIM), BF16),
                   jax.ShapeDtypeStruct((bsz, s, 3 * DN_W), F32),
                   jax.ShapeDtypeStruct((bsz, s, DN_W), F32),
                   jax.ShapeDtypeStruct((bsz, s, LANES), F32)),
        grid=(bsz, s // tm),
        in_specs=[pl.BlockSpec((1, tm, d), tok),
                  pl.BlockSpec((1, 1, 1, mod.shape[-1]), _mod_index),
                  pl.BlockSpec((1, d), full2),
                  pl.BlockSpec((d, IN_MAIN_W), full2),
                  pl.BlockSpec((d, LANES), full2),
                  rope_spec, rope_spec, rope_spec,
                  pl.BlockSpec((1, HEAD_DIM), full2),
                  pl.BlockSpec((1, HEAD_DIM), full2)],
        out_specs=(pl.BlockSpec((1, ATTN_HEADS, tm, HEAD_DIM), lambda b, j: (b, 0, j + q_lead_blocks, 0)),
                   pl.BlockSpec((1, ATTN_KV_HEADS, tm, HEAD_DIM), head_tok),
                   pl.BlockSpec((1, ATTN_KV_HEADS, tm, 2 * HEAD_DIM), head_tok),
                   pl.BlockSpec((1, tm, 3 * DN_W), tok),
                   pl.BlockSpec((1, tm, DN_W), tok),
                   pl.BlockSpec((1, tm, LANES), tok)),
        compiler_params=_params(("parallel", "parallel")),
        name="in_projection",
    )(xs, mod, norm_w.reshape(1, d), w_main, w_gate, cos, sina, sinb,
      qn_w.reshape(1, HEAD_DIM), kn_w.reshape(1, HEAD_DIM))


def _attn_kernel(q_ref, k_ref, v_ref, o_ref, m_sc, acc_sc):
    ki = pl.program_id(2)
    nk = pl.num_programs(2)

    @pl.when(ki == 0)
    def _():
        m_sc[...] = jnp.full(m_sc.shape, NEG_BIG, F32)
        acc_sc[...] = jnp.zeros(acc_sc.shape, F32)

    tq = q_ref.shape[2]
    chains = [(hd, pl.ds(r0, ATTN_CHAIN_ROWS)) for hd in range(ATTN_HEADS)
              for r0 in range(0, tq, ATTN_CHAIN_ROWS)]
    scores = [lax.dot_general(q_ref[0, hd, rows, :], k_ref[0, hd // ATTN_GROUP], (((1,), (1,)), ((), ())),
                              preferred_element_type=F32) for hd, rows in chains]
    for (hd, rows), s in zip(chains, scores):
        m_prev = m_sc[hd, rows, :]
        m_new = jnp.maximum(m_prev, jnp.max(s, axis=-1, keepdims=True))
        p = jnp.exp2(s - m_new)
        alpha = jnp.exp2(m_prev - m_new)
        acc_sc[hd, rows, :] = alpha * acc_sc[hd, rows, :] + jnp.dot(
            p.astype(BF16), v_ref[0, hd // ATTN_GROUP], preferred_element_type=F32)
        m_sc[hd, rows, :] = m_new

    @pl.when(ki == nk - 1)
    def _():
        for hd in range(ATTN_HEADS):
            o_ref[0, :, hd * HEAD_DIM:(hd + 1) * HEAD_DIM] = (
                acc_sc[hd, :, 0:HEAD_DIM] / acc_sc[hd, :, HEAD_DIM:2 * HEAD_DIM]).astype(o_ref.dtype)


def _attention_call(q, k, v, q_row0, n_q, tq, n_keys, tk, name):
    bsz = q.shape[0]
    n_q_blocks = n_q // tq
    q_block0 = q_row0 // tq
    assert q_block0 * tq == q_row0
    return pl.pallas_call(
        _attn_kernel,
        out_shape=jax.ShapeDtypeStruct((bsz, n_q, ATTN_Q_W), BF16),
        grid=(bsz, n_q_blocks, n_keys // tk),
        in_specs=[pl.BlockSpec((1, ATTN_HEADS, tq, HEAD_DIM), lambda b, i, j: (b, 0, i + q_block0, 0)),
                  pl.BlockSpec((1, ATTN_KV_HEADS, tk, HEAD_DIM), lambda b, i, j: (b, 0, j, 0)),
                  pl.BlockSpec((1, ATTN_KV_HEADS, tk, 2 * HEAD_DIM), lambda b, i, j: (b, 0, j, 0))],
        out_specs=pl.BlockSpec((1, tq, ATTN_Q_W), lambda b, i, j: (b, i, 0)),
        scratch_shapes=[pltpu.VMEM((ATTN_HEADS, tq, 1), F32),
                        pltpu.VMEM((ATTN_HEADS, tq, 2 * HEAD_DIM), F32)],
        compiler_params=_params(("parallel", "parallel", "arbitrary")),
        name=name,
    )(q, k, v)


def latent_query_block(ctx_len, n_lat):
    return ATTN_Q_BLOCK if n_lat % ATTN_Q_BLOCK == 0 and ATTN_Q_BLOCK % ctx_len == 0 else ATTN_CHAIN_ROWS


def attention(q, k, v, ctx_len):
    s = k.shape[2]
    n_lat = s - ctx_len
    lead = q.shape[2] - s
    assert ctx_len % ATTN_CHAIN_ROWS == 0 and n_lat % ATTN_CHAIN_ROWS == 0
    tk = ATTN_KEY_BLOCK if s % ATTN_KEY_BLOCK == 0 else ATTN_CHAIN_ROWS
    tq = latent_query_block(ctx_len, n_lat)
    attn_ctx = _attention_call(q, k, v, lead, ctx_len, ctx_len, ctx_len, ctx_len, "attention_ctx")
    attn_lat = _attention_call(q, k, v, lead + ctx_len, n_lat, tq, s, tk, "attention")
    return attn_ctx, attn_lat


def _dn_prep_kernel(main_ref, prev_ref, next_ref, ba_ref, cw_ref, gp_ref,
                    w_out, u_out, qg_out, kdt_out, qk_out, dl_out, ext_sc, *, ctx_chunks):
    j = pl.program_id(1)
    nj = pl.num_programs(1)
    c = DN_CHUNK
    has_prev = (j != 0) & (j != ctx_chunks)
    has_next = (j != ctx_chunks - 1) & (j != nj - 1)
    ext_sc[0:SUBLANES] = jnp.where(has_prev, prev_ref[0], 0.0)
    ext_sc[SUBLANES:SUBLANES + c] = main_ref[0]
    ext_sc[SUBLANES + c:2 * SUBLANES + c] = jnp.where(has_next, next_ref[0], 0.0)
    y = ext_sc[SUBLANES - CONV_PAD:SUBLANES - CONV_PAD + c] * cw_ref[0:1]
    for t in range(1, CONV_K):
        y = y + ext_sc[SUBLANES - CONV_PAD + t:SUBLANES - CONV_PAD + t + c] * cw_ref[t:t + 1]
    y = y * jax.nn.sigmoid(y)

    ba = ba_ref[0]
    beta_all = jax.nn.sigmoid(ba)
    g_all = -jnp.exp(gp_ref[0:1]) * jax.nn.softplus(ba + gp_ref[1:2])
    row = lax.broadcasted_iota(jnp.int32, (c, c), 0)
    col = lax.broadcasted_iota(jnp.int32, (c, c), 1)
    lower = (row >= col).astype(BF16)
    g_hi = g_all.astype(BF16)
    g_r1 = g_all - g_hi.astype(F32)
    g_mid = g_r1.astype(BF16)
    g_lo = (g_r1 - g_mid.astype(F32)).astype(BF16)
    dotf = functools.partial(jnp.dot, preferred_element_type=F32)
    prefix = dotf(lower, g_hi) + (dotf(lower, g_mid) + dotf(lower, g_lo))
    total = prefix[c - 1:c]
    gc = (prefix, total - prefix + g_all)
    gct = (gc[0].T, gc[1].T)
    row2 = lax.broadcasted_iota(jnp.int32, (2 * c, 2 * c), 0)
    col2 = lax.broadcasted_iota(jnp.int32, (2 * c, 2 * c), 1)
    eye2 = (row2 == col2).astype(F32)
    zero = jnp.zeros((c, c), F32)

    heads = range(DN_HEADS)
    a2, rhs2 = [], []
    for hd in heads:
        q = y[:, hd * HEAD_DIM:(hd + 1) * HEAD_DIM]
        k = y[:, DN_W + hd * HEAD_DIM:DN_W + (hd + 1) * HEAD_DIM]
        v = y[:, 2 * DN_W + hd * HEAD_DIM:2 * DN_W + (hd + 1) * HEAD_DIM]
        q = q * lax.rsqrt(jnp.sum(q * q, axis=-1, keepdims=True) + EPS) * (HEAD_DIM ** -0.5)
        k = k * lax.rsqrt(jnp.sum(k * k, axis=-1, keepdims=True) + EPS)
        kk = _bf16_dot_nt(k, k)
        qk = _bf16_dot_nt(q, k)
        a_dir, rhs_dir = [], []
        for dr in range(2):
            cb = dr * DN_HEADS + hd
            cg = 2 * DN_HEADS + cb
            beta = beta_all[:, cb:cb + 1]
            gcol = gc[dr][:, cg:cg + 1]
            grow = gct[dr][cg:cg + 1, :]
            incl = (row >= col) if dr == 0 else (row <= col)
            strict = (row > col) if dr == 0 else (row < col)
            decay = jnp.exp(jnp.where(incl, gcol - grow, NEG_BIG))
            a_dir.append(jnp.where(strict, kk * beta * decay, 0.0))
            eg = jnp.exp(gcol)
            rhs_dir.append(jnp.concatenate([v * beta, k * (beta * eg)], axis=1))
            qg_out[0, dr, hd] = (q * eg).astype(BF16)
            tot = total[:, cg:cg + 1]
            kdt_out[0, dr, hd] = (k * jnp.exp(tot - gcol)).T.astype(BF16)
            qk_out[0, dr, hd] = (qk * decay).astype(BF16)
            dl_out[0, dr, hd, 0] = jnp.broadcast_to(jnp.exp(tot), (1, LANES))
        a2.append(jnp.concatenate([jnp.concatenate([a_dir[0], zero], axis=1),
                                   jnp.concatenate([zero, a_dir[1]], axis=1)], axis=0))
        rhs2.append(jnp.concatenate(rhs_dir, axis=0))
    base = DN_BASE_BLOCK
    base_mask = (row2 // base) == (col2 // base)
    a_base = [jnp.where(base_mask, a2[hd], 0.0) for hd in heads]
    x = [eye2 - a_base[hd] for hd in heads]
    pw = [_split_dot(a_base[hd], a_base[hd]) for hd in heads]
    for it in range(int(math.log2(base)) - 1):
        x = [x[hd] + _split_dot(x[hd], pw[hd]) for hd in heads]
        if it < int(math.log2(base)) - 2:
            pw = [_split_dot(pw[hd], pw[hd]) for hd in heads]
    blk = base
    while blk < c:
        sibling = ((row2 // (2 * blk)) == (col2 // (2 * blk))) & ((row2 // blk) != (col2 // blk))
        fold = [_bf16_dot(x[hd], jnp.where(sibling, a2[hd], 0.0)) for hd in heads]
        x = [x[hd] - _bf16_dot(fold[hd], x[hd]) for hd in heads]
        blk *= 2
    for hd in heads:
        uw = _bf16_dot(x[hd], rhs2[hd])
        for dr in range(2):
            u_out[0, dr, hd] = uw[dr * c:(dr + 1) * c, 0:HEAD_DIM]
            w_out[0, dr, hd] = uw[dr * c:(dr + 1) * c, HEAD_DIM:2 * HEAD_DIM].astype(BF16)


def dn_prepare(dnqkv, ba, conv_w, gate_par, ctx_len):
    bsz, s, wdt = dnqkv.shape
    c = DN_CHUNK
    nc = s // c
    rows8 = s // SUBLANES
    per = c // SUBLANES
    chain = lambda b, j: (b, 0, 0, j, 0)
    return pl.pallas_call(
        functools.partial(_dn_prep_kernel, ctx_chunks=ctx_len // c),
        out_shape=(jax.ShapeDtypeStruct((bsz, 2, DN_HEADS, s, HEAD_DIM), BF16),
                   jax.ShapeDtypeStruct((bsz, 2, DN_HEADS, s, HEAD_DIM), F32),
                   jax.ShapeDtypeStruct((bsz, 2, DN_HEADS, s, HEAD_DIM), BF16),
                   jax.ShapeDtypeStruct((bsz, 2, DN_HEADS, HEAD_DIM, s), BF16),
                   jax.ShapeDtypeStruct((bsz, 2, DN_HEADS, s, c), BF16),
                   jax.ShapeDtypeStruct((bsz, 2, DN_HEADS, nc, 1, LANES), F32)),
        grid=(bsz, nc),
        in_specs=[pl.BlockSpec((1, c, wdt), lambda b, j: (b, j, 0)),
                  pl.BlockSpec((1, SUBLANES, wdt), lambda b, j: (b, jnp.maximum(j * per - 1, 0), 0)),
                  pl.BlockSpec((1, SUBLANES, wdt), lambda b, j: (b, jnp.minimum((j + 1) * per, rows8 - 1), 0)),
                  pl.BlockSpec((1, c, LANES), lambda b, j: (b, j, 0)),
                  pl.BlockSpec((SUBLANES, wdt), lambda b, j: (0, 0)),
                  pl.BlockSpec((SUBLANES, LANES), lambda b, j: (0, 0))],
        out_specs=(pl.BlockSpec((1, 2, DN_HEADS, c, HEAD_DIM), chain),
                   pl.BlockSpec((1, 2, DN_HEADS, c, HEAD_DIM), chain),
                   pl.BlockSpec((1, 2, DN_HEADS, c, HEAD_DIM), chain),
                   pl.BlockSpec((1, 2, DN_HEADS, HEAD_DIM, c), lambda b, j: (b, 0, 0, 0, j)),
                   pl.BlockSpec((1, 2, DN_HEADS, c, c), chain),
                   pl.BlockSpec((1, 2, DN_HEADS, 1, 1, LANES), lambda b, j: (b, 0, 0, j, 0, 0))),
        scratch_shapes=[pltpu.VMEM((c + 2 * SUBLANES, wdt), F32)],
        compiler_params=_params(("parallel", "parallel")),
        name="dn_prepare",
    )(dnqkv, dnqkv, dnqkv, ba, conv_w, gate_par)


def _dn_scan_kernel(*refs, bsz):
    ins = refs[:12]
    of_ref, ob_ref, s_sc = refs[12:]
    n = pl.program_id(0)

    @pl.when(n == 0)
    def _():
        s_sc[...] = jnp.zeros(s_sc.shape, F32)

    chains = [(dr, b, hd) for dr in range(2) for b in range(bsz) for hd in range(DN_HEADS)]
    dotf = functools.partial(jnp.dot, preferred_element_type=F32)

    def inp(dr, k):
        return ins[dr * 6 + k]

    state = [s_sc[ci] for ci in range(len(chains))]
    r = [dotf(jnp.concatenate([inp(dr, 0)[b, 0, hd], inp(dr, 2)[b, 0, hd]], axis=0), state[ci].astype(BF16))
         for ci, (dr, b, hd) in enumerate(chains)]
    v_new = [(inp(dr, 1)[b, 0, hd] - r[ci][0:DN_CHUNK]).astype(BF16) for ci, (dr, b, hd) in enumerate(chains)]
    intra = [dotf(inp(dr, 4)[b, 0, hd], v_new[ci]) for ci, (dr, b, hd) in enumerate(chains)]
    upd = [dotf(inp(dr, 3)[b, 0, hd], v_new[ci]) for ci, (dr, b, hd) in enumerate(chains)]
    for ci, (dr, b, hd) in enumerate(chains):
        o_ref = of_ref if dr == 0 else ob_ref
        o_ref[b, :, hd * HEAD_DIM:(hd + 1) * HEAD_DIM] = r[ci][DN_CHUNK:] + intra[ci]
        s_sc[ci] = state[ci] * inp(dr, 5)[b, 0, hd, 0] + upd[ci]


def dn_scan(w, u, qg, kdt, qk, dl, ctx_len):
    bsz, _, _, s, _ = w.shape
    c = DN_CHUNK
    nc = s // c
    cc = ctx_len // c

    def bwd_chunk(n):
        return jnp.where(n < cc, cc - 1 - n, nc - 1 - (n - cc))

    in_specs, args = [], []
    for dr in range(2):
        pos = (lambda n: n) if dr == 0 else bwd_chunk
        tokm = lambda n, dr=dr, pos=pos: (0, dr, 0, pos(n), 0)
        for arr in (w, u, qg):
            in_specs.append(pl.BlockSpec((bsz, 1, DN_HEADS, c, HEAD_DIM), tokm))
            args.append(arr)
        in_specs.append(pl.BlockSpec((bsz, 1, DN_HEADS, HEAD_DIM, c), lambda n, dr=dr, pos=pos: (0, dr, 0, 0, pos(n))))
        args.append(kdt)
        in_specs.append(pl.BlockSpec((bsz, 1, DN_HEADS, c, c), tokm))
        args.append(qk)
        in_specs.append(pl.BlockSpec((bsz, 1, DN_HEADS, 1, 1, LANES), lambda n, dr=dr, pos=pos: (0, dr, 0, pos(n), 0, 0)))
        args.append(dl)
    return pl.pallas_call(
        functools.partial(_dn_scan_kernel, bsz=bsz),
        out_shape=(jax.ShapeDtypeStruct((bsz, s, DN_W), F32), jax.ShapeDtypeStruct((bsz, s, DN_W), F32)),
        grid=(nc,),
        in_specs=in_specs,
        out_specs=(pl.BlockSpec((bsz, c, DN_W), lambda n: (0, n, 0)),
                   pl.BlockSpec((bsz, c, DN_W), lambda n: (0, bwd_chunk(n), 0))),
        scratch_shapes=[pltpu.VMEM((2 * bsz * DN_HEADS, HEAD_DIM, HEAD_DIM), F32)],
        compiler_params=_params(("arbitrary",)),
        name="dn_scan",
    )(*args)


def _outproj_kernel(x_ref, mod_ref, attn_ctx_ref, attn_lat_ref, of_ref, ob_ref, gate_ref, dnw_ref, wo_ref,
                    n2w_ref, x_out, h2_out):
    d = x_ref.shape[-1]
    o = of_ref[0] + ob_ref[0]
    gate = gate_ref[0]
    parts = [jnp.where(pl.program_id(1) == 0, attn_ctx_ref[0], attn_lat_ref[0])]
    for hd in range(DN_HEADS):
        sl = slice(hd * HEAD_DIM, (hd + 1) * HEAD_DIM)
        g = gate[:, sl]
        parts.append((_head_rmsnorm(o[:, sl], dnw_ref[...]) * (g * jax.nn.sigmoid(g))).astype(BF16))
    mix = jnp.concatenate(parts, axis=1)
    y = jnp.dot(mix, wo_ref[...], preferred_element_type=F32)
    x = x_ref[0] + mod_ref[0, 0, :, 2 * d:3 * d] * y
    x_out[0] = x
    h2_out[0] = _modulated_norm(x, n2w_ref[...], mod_ref[0, 0, :, 3 * d:4 * d],
                                mod_ref[0, 0, :, 4 * d:5 * d]).astype(BF16)


def out_projection(xs, mod, attn_ctx, attn_lat, o_f, o_b, gate, dn_norm_w, w_out, norm2_w):
    bsz, s, d = xs.shape
    tm = TOKEN_BLOCK
    assert attn_ctx.shape[1] == tm
    tok = lambda b, j: (b, j, 0)
    full2 = lambda b, j: (0, 0)
    return pl.pallas_call(
        _outproj_kernel,
        out_shape=(jax.ShapeDtypeStruct((bsz, s, d), F32), jax.ShapeDtypeStruct((bsz, s, d), BF16)),
        grid=(bsz, s // tm),
        in_specs=[pl.BlockSpec((1, tm, d), tok),
                  pl.BlockSpec((1, 1, 1, mod.shape[-1]), _mod_index),
                  pl.BlockSpec((1, tm, ATTN_Q_W), lambda b, j: (b, 0, 0)),
                  pl.BlockSpec((1, tm, ATTN_Q_W), lambda b, j: (b, jnp.maximum(j - 1, 0), 0)),
                  pl.BlockSpec((1, tm, DN_W), tok),
                  pl.BlockSpec((1, tm, DN_W), tok),
                  pl.BlockSpec((1, tm, DN_W), tok),
                  pl.BlockSpec((1, HEAD_DIM), full2),
                  pl.BlockSpec(w_out.shape, full2),
                  pl.BlockSpec((1, d), full2)],
        out_specs=(pl.BlockSpec((1, tm, d), tok), pl.BlockSpec((1, tm, d), tok)),
        compiler_params=_params(("parallel", "parallel")),
        name="out_projection",
    )(xs, mod, attn_ctx, attn_lat, o_f, o_b, gate, dn_norm_w.reshape(1, HEAD_DIM), w_out,
      norm2_w.reshape(1, d))


def _peer_score_kernel(h2_ref, wq_ref, sk_ref, st_out):
    q = jnp.dot(h2_ref[0], wq_ref[...], preferred_element_type=F32).astype(BF16)
    for hp in range(2 * PEER_HEADS):
        st_out[0, hp] = lax.dot_general(sk_ref[hp], q[:, hp * PEER_HALF:(hp + 1) * PEER_HALF],
                                        (((1,), (1,)), ((), ())), preferred_element_type=F32)


def peer_scores(h2, wq, subkeys):
    bsz, s, d = h2.shape
    tm = TOKEN_BLOCK
    nhp = 2 * PEER_HEADS
    return pl.pallas_call(
        _peer_score_kernel,
        out_shape=jax.ShapeDtypeStruct((bsz, nhp, N_KEYS, s), F32),
        grid=(bsz, s // tm),
        in_specs=[pl.BlockSpec((1, tm, d), lambda b, j: (b, j, 0)),
                  pl.BlockSpec(wq.shape, lambda b, j: (0, 0)),
                  pl.BlockSpec(subkeys.shape, lambda b, j: (0, 0, 0))],
        out_specs=pl.BlockSpec((1, nhp, N_KEYS, tm), lambda b, j: (b, 0, 0, j)),
        compiler_params=_params(("parallel", "parallel")),
        name="peer_scores",
    )(h2, wq, subkeys)


def _sorting_network(n):
    pairs = []
    p = 1
    while p < n:
        k = p
        while k >= 1:
            for j in range(k % p, n - k, 2 * k):
                for i in range(min(k, n - j - k)):
                    if (i + j) // (2 * p) == (i + j + k) // (2 * p):
                        pairs.append((i + j, i + j + k))
            k //= 2
        p *= 2
    return pairs


def _sorted_top(s, k, with_rank):
    n_tiles = s.shape[0] // SUBLANES
    v = [s[i * SUBLANES:(i + 1) * SUBLANES] for i in range(n_tiles)]
    for lo, hi in _sorting_network(n_tiles):
        v[lo], v[hi] = jnp.maximum(v[lo], v[hi]), jnp.minimum(v[lo], v[hi])
    out = []
    for r in range(k):
        m = jnp.max(v[0], axis=0, keepdims=True)
        out.append(m)
        hit = v[0] == m
        for i in range(min(n_tiles, k - 1 - r)):
            v[i] = jnp.where(hit, v[i + 1] if i + 1 < n_tiles else NEG_BIG, v[i])
    rank = None
    if with_rank:
        rank = jnp.full(s.shape, float(k), F32)
        for r in reversed(range(k)):
            rank = jnp.where(s >= out[r], float(r), rank)
    return out, rank


def _paired_bf16_words(x):
    w = pltpu.bitcast(x.astype(BF16).astype(F32), jnp.uint32)
    return w | (w >> 16)


def _peer_topk_kernel(st_ref, cnt_out, e0_out, rank_out, e1_out):
    nt = PEER_TOPK + 1
    tops = ([], [])
    for hd in range(PEER_HEADS):
        a, _ = _sorted_top(st_ref[0, 2 * hd], nt, False)
        b, rank = _sorted_top(st_ref[0, 2 * hd + 1], nt, True)
        rank_out[0, hd] = rank.astype(BF16)
        tops[0].append(a)
        tops[1].append(b)
    a8 = [jnp.concatenate([tops[0][hd][r] for hd in range(PEER_HEADS)], axis=0) for r in range(nt)]
    b8 = [jnp.concatenate([tops[1][hd][r] for hd in range(PEER_HEADS)], axis=0) for r in range(nt)]
    cand = [a8[i] + b8[j] for i in range(nt) for j in range(nt) if (i + 1) * (j + 1) <= nt]
    top = []
    for _ in range(nt):
        m = functools.reduce(jnp.maximum, cand)
        top.append(m)
        cand = [jnp.where(t == m, NEG_BIG, t) for t in cand]
    tau8 = 0.5 * (top[PEER_TOPK - 1] + top[PEER_TOPK])
    smax = a8[0] + b8[0]
    z8 = jnp.exp(top[0] - smax)
    for t in top[1:PEER_TOPK]:
        z8 = z8 + jnp.exp(t - smax)
    rz8 = 1.0 / z8
    for hd in range(PEER_HEADS):
        s0 = st_ref[0, 2 * hd]
        s1 = st_ref[0, 2 * hd + 1]
        thr = tau8[hd:hd + 1] - s0
        cnt = jnp.zeros(s0.shape, F32)
        for r in range(nt):
            cnt = jnp.where(tops[1][hd][r] > thr, float(r + 1), cnt)
        cnt_out[0, hd] = _paired_bf16_words(cnt)
        e0_out[0, hd] = _paired_bf16_words(jnp.exp(s0 - tops[0][hd][0]))
        e1_out[0, hd] = (jnp.exp(s1 - tops[1][hd][0]) * (rz8[hd:hd + 1] * GELU_GATE_SCALE)).astype(BF16)


def peer_topk(st):
    bsz, nhp, nk, s = st.shape
    tl = LANES
    spec = pl.BlockSpec((1, PEER_HEADS, nk, tl), lambda b, j: (b, 0, 0, j))
    words = jax.ShapeDtypeStruct((bsz, PEER_HEADS, nk, s), jnp.uint32)
    halfs = jax.ShapeDtypeStruct((bsz, PEER_HEADS, nk, s), BF16)
    return pl.pallas_call(
        _peer_topk_kernel,
        out_shape=(words, words, halfs, halfs),
        grid=(bsz, s // tl),
        in_specs=[pl.BlockSpec((1, nhp, nk, tl), lambda b, j: (b, 0, 0, j))],
        out_specs=(spec, spec, spec, spec),
        compiler_params=_params(("parallel", "parallel")),
        name="peer_topk",
    )(st)


def _peer_expert_kernel(x_ref, mod_ref, h2_ref, u_ref, vt_ref, cnt_ref, e0_ref, rank_ref, e1_ref,
                        x_out, acc_sc, *, ctx_len):
    tok_block = pl.program_id(1)
    ec = pl.program_id(2)
    n_ec = pl.num_programs(2)
    d = x_ref.shape[-1]
    tb = x_ref.shape[1]
    pk = 2 * SUBLANES

    @pl.when(ec == 0)
    def _():
        acc_sc[...] = jnp.zeros(acc_sc.shape, F32)

    def row_tile(ref, hd, ii):
        return pltpu.bitcast(jnp.broadcast_to(ref[0, hd, ii:ii + 1, :], (SUBLANES, tb)), BF16)

    h2 = h2_ref[0]
    pair = 2 * N_KEYS
    n_pairs = PEER_I_PER_STEP // 2

    def activations(p):
        return lax.dot_general(u_ref[p * pair:(p + 1) * pair, :], h2, (((1,), (1,)), ((), ())),
                               preferred_element_type=F32)

    def gate_weights(p):
        tiles = []
        for ii in (2 * p, 2 * p + 1):
            wt = [jnp.zeros((pk, tb), BF16) for _ in range(N_KEYS // pk)]
            for hd in range(PEER_HEADS):
                cnt = row_tile(cnt_ref, hd, ii)
                e0 = row_tile(e0_ref, hd, ii)
                for rt in range(N_KEYS // pk):
                    rows = slice(rt * pk, (rt + 1) * pk)
                    sel = jnp.where(rank_ref[0, hd, rows, :] < cnt, e1_ref[0, hd, rows, :],
                                    jnp.zeros((), BF16))
                    wt[rt] = wt[rt] + sel * e0
            tiles += wt
        return tiles

    def gated(at2, tiles):
        g = []
        for rt, wt in enumerate(tiles):
            a = at2[rt * pk:(rt + 1) * pk]
            act = a * (1.0 + lax.erf(a))
            g.append(act.astype(BF16) * wt)
        return jnp.concatenate(g, axis=0)

    g = [gated(activations(p), gate_weights(p)) for p in range(n_pairs)]
    acc_sc[...] += jnp.dot(vt_ref[...], jnp.concatenate(g, axis=0), preferred_element_type=F32)

    @pl.when(ec == n_ec - 1)
    def _():
        tok = tok_block * tb + lax.broadcasted_iota(jnp.int32, (tb, 1), 0)
        g2 = jnp.where(tok < ctx_len, mod_ref[0, 0, :, 5 * d:6 * d], mod_ref[0, 1, :, 5 * d:6 * d])
        x_out[0] = x_ref[0] + g2 * acc_sc[...].T


def peer_experts(xs, mod, h2, u_tab, vt_tab, cnt, e0, rank, e1, ctx_len):
    bsz, s, d = xs.shape
    tb = PEER_TOKEN_BLOCK if s % PEER_TOKEN_BLOCK == 0 else TOKEN_BLOCK
    ech = PEER_I_PER_STEP * N_KEYS
    n_exp = u_tab.shape[0]
    tok = lambda b, j, e: (b, j, 0)
    per_tok = lambda b, j, e: (b, 0, 0, j)
    per_i = lambda b, j, e: (b, 0, e, j)
    once = dict(pipeline_mode=pl.Buffered(1))
    return pl.pallas_call(
        functools.partial(_peer_expert_kernel, ctx_len=ctx_len),
        out_shape=jax.ShapeDtypeStruct((bsz, s, d), F32),
        grid=(bsz, s // tb, n_exp // ech),
        in_specs=[pl.BlockSpec((1, tb, d), tok, **once),
                  pl.BlockSpec((1, 2, 1, mod.shape[-1]), lambda b, j, e: (b, 0, 0, 0)),
                  pl.BlockSpec((1, tb, d), tok, **once),
                  pl.BlockSpec((ech, d), lambda b, j, e: (e, 0)),
                  pl.BlockSpec((d, ech), lambda b, j, e: (0, e)),
                  pl.BlockSpec((1, PEER_HEADS, PEER_I_PER_STEP, tb), per_i),
                  pl.BlockSpec((1, PEER_HEADS, PEER_I_PER_STEP, tb), per_i),
                  pl.BlockSpec((1, PEER_HEADS, N_KEYS, tb), per_tok, **once),
                  pl.BlockSpec((1, PEER_HEADS, N_KEYS, tb), per_tok, **once)],
        out_specs=pl.BlockSpec((1, tb, d), tok, **once),
        scratch_shapes=[pltpu.VMEM((d, tb), F32)],
        compiler_params=_params(("parallel", "parallel", "arbitrary")),
        name="peer_experts",
    )(xs, mod, h2, u_tab, vt_tab, cnt, e0, rank, e1)


def _rope_tables(ctx_len, n_lat):
    rows = n_lat // GRID_W
    row = jnp.repeat(jnp.arange(rows, dtype=F32), GRID_W)
    col = jnp.tile(jnp.arange(GRID_W, dtype=F32), rows)
    axis_dim = HEAD_DIM // 2
    inv_freq = ROPE_THETA ** (-jnp.arange(0, axis_dim, 2, dtype=F32) / axis_dim)
    ang_r = row[:, None] * inv_freq[None, :]
    ang_c = col[:, None] * inv_freq[None, :]
    ang = jnp.concatenate([ang_r, ang_r, ang_c, ang_c], axis=-1)
    cos, sin = jnp.cos(ang), jnp.sin(ang)
    first = (jnp.arange(HEAD_DIM) % (HEAD_DIM // 2)) < (HEAD_DIM // 4)
    sina = jnp.where(first, -sin, 0.0)
    sinb = jnp.where(first, 0.0, sin)
    pad = lambda t, v: jnp.concatenate([jnp.full((ctx_len, HEAD_DIM), v, F32), t], axis=0)
    return pad(cos, 1.0), pad(sina, 0.0), pad(sinb, 0.0)


def kernel(x, c, ctx, c_ctx, ada_w, ada_b, norm1_w, norm2_w, w_in, attn_qnorm_w, attn_knorm_w, dn_conv_w,
           dn_A_log, dn_dt_bias, dn_norm_w, w_out, peer_wq, peer_subkeys, peer_u, peer_v):
    bsz, n_lat, d = x.shape
    ctx_len = ctx.shape[1]
    depth = ada_w.shape[0]
    assert ctx_len == TOKEN_BLOCK and n_lat % TOKEN_BLOCK == 0 and bsz + 1 <= SUBLANES
    assert w_in.shape[-1] == IN_MAIN_W + N_GATE_COLS

    xs = jnp.concatenate([ctx, x], axis=1)
    cos, sina, sinb = _rope_tables(ctx_len, n_lat)
    q_lead_blocks = (-ctx_len % latent_query_block(ctx_len, n_lat)) // TOKEN_BLOCK

    cc = jnp.zeros((SUBLANES, d), F32).at[:bsz].set(c).at[bsz].set(c_ctx)
    mod_all = ada_modulation(cc, ada_w, ada_b)
    mod_ctx = jnp.broadcast_to(mod_all[:, bsz][:, None], (depth, bsz, 6 * d))
    mod = jnp.stack([mod_ctx, mod_all[:, :bsz]], axis=2)[:, :, :, None, :]

    for l in range(depth):
        w_main = w_in[l, :, :IN_MAIN_W].astype(BF16)
        w_gate = jnp.pad(w_in[l, :, IN_MAIN_W:], ((0, 0), (0, LANES - N_GATE_COLS))).astype(BF16)
        q, k, v, dnqkv, gate, ba = in_projection(xs, mod[l], norm1_w[l], w_main, w_gate, cos, sina, sinb,
                                                 attn_qnorm_w[l], attn_knorm_w[l], q_lead_blocks)
        attn_ctx, attn_lat = attention(q, k, v, ctx_len)
        conv_w = jnp.pad(dn_conv_w[l], ((0, SUBLANES - CONV_K), (0, 0)))
        gate_par = jnp.zeros((SUBLANES, LANES), F32)
        gate_par = gate_par.at[0, 2 * DN_HEADS:4 * DN_HEADS].set(dn_A_log[l].reshape(-1))
        gate_par = gate_par.at[1, 2 * DN_HEADS:4 * DN_HEADS].set(dn_dt_bias[l].reshape(-1))
        o_f, o_b = dn_scan(*dn_prepare(dnqkv, ba, conv_w, gate_par, ctx_len), ctx_len)
        xs, h2 = out_projection(xs, mod[l], attn_ctx, attn_lat, o_f, o_b, gate, dn_norm_w[l],
                                w_out[l].astype(BF16), norm2_w[l])
        sk = peer_subkeys[l].reshape(2 * PEER_HEADS, N_KEYS, PEER_HALF).astype(BF16)
        st = peer_scores(h2, peer_wq[l].astype(BF16), sk)
        cnt, e0, rank, e1 = peer_topk(st)
        xs = peer_experts(xs, mod[l], h2, (peer_u[l] * GELU_GATE_SCALE).astype(BF16), peer_v[l].T.astype(BF16), cnt, e0, rank, e1,
                          ctx_len)
    return xs[:, ctx_len:]
```

```python
import functools
import math

import jax
import jax.numpy as jnp
from jax import lax
from jax.experimental import pallas as pl
from jax.experimental.pallas import tpu as pltpu

F32 = jnp.float32
BF16 = jnp.bfloat16
HIGHEST = lax.Precision.HIGHEST

HEAD_DIM = 128
ATTN_HEADS = 4
ATTN_KV_HEADS = 2
ATTN_GROUP = ATTN_HEADS // ATTN_KV_HEADS
DN_HEADS = 4
ATTN_Q_W = ATTN_HEADS * HEAD_DIM
ATTN_KV_W = ATTN_KV_HEADS * HEAD_DIM
DN_W = DN_HEADS * HEAD_DIM
IN_MAIN_W = ATTN_Q_W + 2 * ATTN_KV_W + 3 * DN_W + DN_W
N_GATE_COLS = 4 * DN_HEADS
ROPE_THETA = 10000.0
GRID_W = 64
CONV_K = 5
CONV_PAD = CONV_K // 2
PEER_HEADS = 8
PEER_HALF = 128
N_KEYS = 128
PEER_TOPK = 16
EPS = 1e-6
NEG_BIG = -1e30
GELU_GATE_SCALE = 2.0 ** -0.5

LANES = 128
SUBLANES = 8
MXU_TILE = 256
TOKEN_BLOCK = 256
DN_CHUNK = 128
DN_BASE_BLOCK = 2
ATTN_KEY_BLOCK = 1280
ATTN_Q_BLOCK = 1024
ATTN_CHAIN_ROWS = 256
PEER_TOKEN_BLOCK = 1280
PEER_I_PER_STEP = 8
VMEM_LIMIT = 56 * 1024 * 1024


def _params(sem):
    return pltpu.CompilerParams(dimension_semantics=sem, vmem_limit_bytes=VMEM_LIMIT)


def _bf16_dot(a, b):
    return jnp.dot(a.astype(BF16), b.astype(BF16), preferred_element_type=F32)


def _split_dot(a, b):
    a_hi = a.astype(BF16)
    b_hi = b.astype(BF16)
    a_lo = (a - a_hi.astype(F32)).astype(BF16)
    b_lo = (b - b_hi.astype(F32)).astype(BF16)
    dot = functools.partial(jnp.dot, preferred_element_type=F32)
    return dot(a_hi, b_hi) + (dot(a_hi, b_lo) + dot(a_lo, b_hi))


def _bf16_dot_nt(a, b):
    return lax.dot_general(a.astype(BF16), b.astype(BF16), (((1,), (1,)), ((), ())),
                           preferred_element_type=F32)


def _ada_kernel(c_ref, w_ref, b_ref, o_ref):
    c = c_ref[...]
    a = c * jax.nn.sigmoid(c)
    o_ref[0] = jnp.dot(a, w_ref[0], preferred_element_type=F32, precision=HIGHEST) + b_ref[0]


def ada_modulation(cc, ada_w, ada_b):
    depth, d, n = ada_w.shape
    tn = 1536
    return pl.pallas_call(
        _ada_kernel,
        out_shape=jax.ShapeDtypeStruct((depth, SUBLANES, n), F32),
        grid=(depth, n // tn),
        in_specs=[pl.BlockSpec((SUBLANES, d), lambda l, j: (0, 0)),
                  pl.BlockSpec((1, d, tn), lambda l, j: (l, 0, j)),
                  pl.BlockSpec((1, 1, tn), lambda l, j: (l, 0, j))],
        out_specs=pl.BlockSpec((1, SUBLANES, tn), lambda l, j: (l, 0, j)),
        compiler_params=_params(("parallel", "parallel")),
        name="ada_modulation",
    )(cc, ada_w, ada_b.reshape(depth, 1, n))


def _mod_index(b, j):
    return (b, jnp.minimum(j, 1), 0, 0)


def _modulated_norm(x, nw, shift, scale):
    ms = jnp.mean(x * x, axis=-1, keepdims=True)
    y = x * lax.rsqrt(ms + EPS) * nw
    return y * (1.0 + scale) + shift


def _head_rmsnorm(x, w):
    return x * lax.rsqrt(jnp.mean(x * x, axis=-1, keepdims=True) + EPS) * w


def _inproj_kernel(x_ref, mod_ref, nw_ref, wm_ref, wg_ref, cos_ref, sina_ref, sinb_ref, qnw_ref, knw_ref,
                   q_out, k_out, v_out, dn_out, gate_out, ba_out):
    d = x_ref.shape[-1]
    x = x_ref[0]
    h = _modulated_norm(x, nw_ref[...], mod_ref[0, 0, :, 0:d], mod_ref[0, 0, :, d:2 * d]).astype(BF16)
    p = jnp.dot(h, wm_ref[...], preferred_element_type=F32)
    ba_out[0] = jnp.dot(h, wg_ref[...], preferred_element_type=F32)
    cos = cos_ref[...]
    sina = sina_ref[...]
    sinb = sinb_ref[...]

    def rope(t):
        return (t * cos + pltpu.roll(t, HEAD_DIM - HEAD_DIM // 4, 1) * sina
                + pltpu.roll(t, HEAD_DIM // 4, 1) * sinb)

    scale = HEAD_DIM ** -0.5 * math.log2(math.e)
    for hd in range(ATTN_HEADS):
        qh = _head_rmsnorm(p[:, hd * HEAD_DIM:(hd + 1) * HEAD_DIM], qnw_ref[...])
        q_out[0, hd] = (rope(qh) * scale).astype(BF16)
    off = ATTN_Q_W
    for hd in range(ATTN_KV_HEADS):
        kh = _head_rmsnorm(p[:, off + hd * HEAD_DIM:off + (hd + 1) * HEAD_DIM], knw_ref[...])
        k_out[0, hd] = rope(kh).astype(BF16)
    off += ATTN_KV_W
    for hd in range(ATTN_KV_HEADS):
        v_out[0, hd, :, 0:HEAD_DIM] = p[:, off + hd * HEAD_DIM:off + (hd + 1) * HEAD_DIM].astype(BF16)
        v_out[0, hd, :, HEAD_DIM:2 * HEAD_DIM] = jnp.ones((x.shape[0], HEAD_DIM), BF16)
    off += ATTN_KV_W
    dn_out[0] = p[:, off:off + 3 * DN_W]
    off += 3 * DN_W
    gate_out[0] = p[:, off:off + DN_W]


def in_projection(xs, mod, norm_w, w_main, w_gate, cos, sina, sinb, qn_w, kn_w, q_lead_blocks):
    bsz, s, d = xs.shape
    tm = TOKEN_BLOCK
    tok = lambda b, j: (b, j, 0)
    head_tok = lambda b, j: (b, 0, j, 0)
    full2 = lambda b, j: (0, 0)
    rope_spec = pl.BlockSpec((tm, HEAD_DIM), lambda b, j: (j, 0))
    return pl.pallas_call(
        _inproj_kernel,
        out_shape=(jax.ShapeDtypeStruct((bsz, ATTN_HEADS, q_lead_blocks * tm + s, HEAD_DIM), BF16),
                   jax.ShapeDtypeStruct((bsz, ATTN_KV_HEADS, s, HEAD_DIM), BF16),
                   jax.ShapeDtypeStruct((bsz, ATTN_KV_HEADS, s, 2 * HEAD_DIM), BF16),
                   jax.ShapeDtypeStruct((bsz, s, 3 * DN_W), F32),
                   jax.ShapeDtypeStruct((bsz, s, DN_W), F32),
                   jax.ShapeDtypeStruct((bsz, s, LANES), F32)),
        grid=(bsz, s // tm),
        in_specs=[pl.BlockSpec((1, tm, d), tok),
                  pl.BlockSpec((1, 1, 1, mod.shape[-1]), _mod_index),
                  pl.BlockSpec((1, d), full2),
                  pl.BlockSpec((d, IN_MAIN_W), full2),
                  pl.BlockSpec((d, LANES), full2),
                  rope_spec, rope_spec, rope_spec,
                  pl.BlockSpec((1, HEAD_DIM), full2),
                  pl.BlockSpec((1, HEAD_DIM), full2)],
        out_specs=(pl.BlockSpec((1, ATTN_HEADS, tm, HEAD_DIM), lambda b, j: (b, 0, j + q_lead_blocks, 0)),
                   pl.BlockSpec((1, ATTN_KV_HEADS, tm, HEAD_DIM), head_tok),
                   pl.BlockSpec((1, ATTN_KV_HEADS, tm, 2 * HEAD_DIM), head_tok),
                   pl.BlockSpec((1, tm, 3 * DN_W), tok),
                   pl.BlockSpec((1, tm, DN_W), tok),
                   pl.BlockSpec((1, tm, LANES), tok)),
        compiler_params=_params(("parallel", "parallel")),
        name="in_projection",
    )(xs, mod, norm_w.reshape(1, d), w_main, w_gate, cos, sina, sinb,
      qn_w.reshape(1, HEAD_DIM), kn_w.reshape(1, HEAD_DIM))


def _attn_kernel(q_ref, k_ref, v_ref, o_ref, m_sc, acc_sc):
    ki = pl.program_id(2)
    nk = pl.num_programs(2)

    @pl.when(ki == 0)
    def _():
        m_sc[...] = jnp.full(m_sc.shape, NEG_BIG, F32)
        acc_sc[...] = jnp.zeros(acc_sc.shape, F32)

    tq = q_ref.shape[2]
    chains = [(hd, pl.ds(r0, ATTN_CHAIN_ROWS)) for hd in range(ATTN_HEADS)
              for r0 in range(0, tq, ATTN_CHAIN_ROWS)]
    scores = [lax.dot_general(q_ref[0, hd, rows, :], k_ref[0, hd // ATTN_GROUP], (((1,), (1,)), ((), ())),
                              preferred_element_type=F32) for hd, rows in chains]
    for (hd, rows), s in zip(chains, scores):
        m_prev = m_sc[hd, rows, :]
        m_new = jnp.maximum(m_prev, jnp.max(s, axis=-1, keepdims=True))
        p = jnp.exp2(s - m_new)
        alpha = jnp.exp2(m_prev - m_new)
        acc_sc[hd, rows, :] = alpha * acc_sc[hd, rows, :] + jnp.dot(
            p.astype(BF16), v_ref[0, hd // ATTN_GROUP], preferred_element_type=F32)
        m_sc[hd, rows, :] = m_new

    @pl.when(ki == nk - 1)
    def _():
        for hd in range(ATTN_HEADS):
            o_ref[0, :, hd * HEAD_DIM:(hd + 1) * HEAD_DIM] = (
                acc_sc[hd, :, 0:HEAD_DIM] / acc_sc[hd, :, HEAD_DIM:2 * HEAD_DIM]).astype(o_ref.dtype)


def _attention_call(q, k, v, q_row0, n_q, tq, n_keys, tk, name):
    bsz = q.shape[0]
    n_q_blocks = n_q // tq
    q_block0 = q_row0 // tq
    assert q_block0 * tq == q_row0
    return pl.pallas_call(
        _attn_kernel,
        out_shape=jax.ShapeDtypeStruct((bsz, n_q, ATTN_Q_W), BF16),
        grid=(bsz, n_q_blocks, n_keys // tk),
        in_specs=[pl.BlockSpec((1, ATTN_HEADS, tq, HEAD_DIM), lambda b, i, j: (b, 0, i + q_block0, 0)),
                  pl.BlockSpec((1, ATTN_KV_HEADS, tk, HEAD_DIM), lambda b, i, j: (b, 0, j, 0)),
                  pl.BlockSpec((1, ATTN_KV_HEADS, tk, 2 * HEAD_DIM), lambda b, i, j: (b, 0, j, 0))],
        out_specs=pl.BlockSpec((1, tq, ATTN_Q_W), lambda b, i, j: (b, i, 0)),
        scratch_shapes=[pltpu.VMEM((ATTN_HEADS, tq, 1), F32),
                        pltpu.VMEM((ATTN_HEADS, tq, 2 * HEAD_DIM), F32)],
        compiler_params=_params(("parallel", "parallel", "arbitrary")),
        name=name,
    )(q, k, v)


def latent_query_block(ctx_len, n_lat):
    return ATTN_Q_BLOCK if n_lat % ATTN_Q_BLOCK == 0 and ATTN_Q_BLOCK % ctx_len == 0 else ATTN_CHAIN_ROWS


def attention(q, k, v, ctx_len):
    s = k.shape[2]
    n_lat = s - ctx_len
    lead = q.shape[2] - s
    assert ctx_len % ATTN_CHAIN_ROWS == 0 and n_lat % ATTN_CHAIN_ROWS == 0
    tk = ATTN_KEY_BLOCK if s % ATTN_KEY_BLOCK == 0 else ATTN_CHAIN_ROWS
    tq = latent_query_block(ctx_len, n_lat)
    attn_ctx = _attention_call(q, k, v, lead, ctx_len, ctx_len, ctx_len, ctx_len, "attention_ctx")
    attn_lat = _attention_call(q, k, v, lead + ctx_len, n_lat, tq, s, tk, "attention")
    return attn_ctx, attn_lat


def _dn_prep_kernel(main_ref, prev_ref, next_ref, ba_ref, cw_ref, gp_ref,
                    w_out, u_out, qg_out, kdt_out, qk_out, dl_out, ext_sc, *, ctx_chunks):
    j = pl.program_id(1)
    nj = pl.num_programs(1)
    c = DN_CHUNK
    has_prev = (j != 0) & (j != ctx_chunks)
    has_next = (j != ctx_chunks - 1) & (j != nj - 1)
    ext_sc[0:SUBLANES] = jnp.where(has_prev, prev_ref[0], 0.0)
    ext_sc[SUBLANES:SUBLANES + c] = main_ref[0]
    ext_sc[SUBLANES + c:2 * SUBLANES + c] = jnp.where(has_next, next_ref[0], 0.0)
    y = ext_sc[SUBLANES - CONV_PAD:SUBLANES - CONV_PAD + c] * cw_ref[0:1]
    for t in range(1, CONV_K):
        y = y + ext_sc[SUBLANES - CONV_PAD + t:SUBLANES - CONV_PAD + t + c] * cw_ref[t:t + 1]
    y = y * jax.nn.sigmoid(y)

    ba = ba_ref[0]
    beta_all = jax.nn.sigmoid(ba)
    g_all = -jnp.exp(gp_ref[0:1]) * jax.nn.softplus(ba + gp_ref[1:2])
    row = lax.broadcasted_iota(jnp.int32, (c, c), 0)
    col = lax.broadcasted_iota(jnp.int32, (c, c), 1)
    lower = (row >= col).astype(BF16)
    g_hi = g_all.astype(BF16)
    g_r1 = g_all - g_hi.astype(F32)
    g_mid = g_r1.astype(BF16)
    g_lo = (g_r1 - g_mid.astype(F32)).astype(BF16)
    dotf = functools.partial(jnp.dot, preferred_element_type=F32)
    prefix = dotf(lower, g_hi) + (dotf(lower, g_mid) + dotf(lower, g_lo))
    total = prefix[c - 1:c]
    gc = (prefix, total - prefix + g_all)
    gct = (gc[0].T, gc[1].T)
    row2 = lax.broadcasted_iota(jnp.int32, (2 * c, 2 * c), 0)
    col2 = lax.broadcasted_iota(jnp.int32, (2 * c, 2 * c), 1)
    eye2 = (row2 == col2).astype(F32)
    zero = jnp.zeros((c, c), F32)

    heads = range(DN_HEADS)
    a2, rhs2 = [], []
    for hd in heads:
        q = y[:, hd * HEAD_DIM:(hd + 1) * HEAD_DIM]
        k = y[:, DN_W + hd * HEAD_DIM:DN_W + (hd + 1) * HEAD_DIM]
        v = y[:, 2 * DN_W + hd * HEAD_DIM:2 * DN_W + (hd + 1) * HEAD_DIM]
        q = q * lax.rsqrt(jnp.sum(q * q, axis=-1, keepdims=True) + EPS) * (HEAD_DIM ** -0.5)
        k = k * lax.rsqrt(jnp.sum(k * k, axis=-1, keepdims=True) + EPS)
        kk = _bf16_dot_nt(k, k)
        qk = _bf16_dot_nt(q, k)
        a_dir, rhs_dir = [], []
        for dr in range(2):
            cb = dr * DN_HEADS + hd
            cg = 2 * DN_HEADS + cb
            beta = beta_all[:, cb:cb + 1]
            gcol = gc[dr][:, cg:cg + 1]
            grow = gct[dr][cg:cg + 1, :]
            incl = (row >= col) if dr == 0 else (row <= col)
            strict = (row > col) if dr == 0 else (row < col)
            decay = jnp.exp(jnp.where(incl, gcol - grow, NEG_BIG))
            a_dir.append(jnp.where(strict, kk * beta * decay, 0.0))
            eg = jnp.exp(gcol)
            rhs_dir.append(jnp.concatenate([v * beta, k * (beta * eg)], axis=1))
            qg_out[0, dr, hd] = (q * eg).astype(BF16)
            tot = total[:, cg:cg + 1]
            kdt_out[0, dr, hd] = (k * jnp.exp(tot - gcol)).T.astype(BF16)
            qk_out[0, dr, hd] = (qk * decay).astype(BF16)
            dl_out[0, dr, hd, 0] = jnp.broadcast_to(jnp.exp(tot), (1, LANES))
        a2.append(jnp.concatenate([jnp.concatenate([a_dir[0], zero], axis=1),
                                   jnp.concatenate([zero, a_dir[1]], axis=1)], axis=0))
        rhs2.append(jnp.concatenate(rhs_dir, axis=0))
    base = DN_BASE_BLOCK
    base_mask = (row2 // base) == (col2 // base)
    a_base = [jnp.where(base_mask, a2[hd], 0.0) for hd in heads]
    x = [eye2 - a_base[hd] for hd in heads]
    if base > 2:
        pw = [_split_dot(a_base[hd], a_base[hd]) for hd in heads]
    for it in range(int(math.log2(base)) - 1):
        x = [x[hd] + _split_dot(x[hd], pw[hd]) for hd in heads]
        if it < int(math.log2(base)) - 2:
            pw = [_split_dot(pw[hd], pw[hd]) for hd in heads]
    blk = base
    while blk < c:
        sibling = ((row2 // (2 * blk)) == (col2 // (2 * blk))) & ((row2 // blk) != (col2 // blk))
        fold = [_bf16_dot(x[hd], jnp.where(sibling, a2[hd], 0.0)) for hd in heads]
        x = [x[hd] - _bf16_dot(fold[hd], x[hd]) for hd in heads]
        blk *= 2
    for hd in heads:
        uw = _bf16_dot(x[hd], rhs2[hd])
        for dr in range(2):
            u_out[0, dr, hd] = uw[dr * c:(dr + 1) * c, 0:HEAD_DIM]
            w_out[0, dr, hd] = uw[dr * c:(dr + 1) * c, HEAD_DIM:2 * HEAD_DIM].astype(BF16)


def dn_prepare(dnqkv, ba, conv_w, gate_par, ctx_len):
    bsz, s, wdt = dnqkv.shape
    c = DN_CHUNK
    nc = s // c
    rows8 = s // SUBLANES
    per = c // SUBLANES
    chain = lambda b, j: (b, 0, 0, j, 0)
    return pl.pallas_call(
        functools.partial(_dn_prep_kernel, ctx_chunks=ctx_len // c),
        out_shape=(jax.ShapeDtypeStruct((bsz, 2, DN_HEADS, s, HEAD_DIM), BF16),
                   jax.ShapeDtypeStruct((bsz, 2, DN_HEADS, s, HEAD_DIM), F32),
                   jax.ShapeDtypeStruct((bsz, 2, DN_HEADS, s, HEAD_DIM), BF16),
                   jax.ShapeDtypeStruct((bsz, 2, DN_HEADS, HEAD_DIM, s), BF16),
                   jax.ShapeDtypeStruct((bsz, 2, DN_HEADS, s, c), BF16),
                   jax.ShapeDtypeStruct((bsz, 2, DN_HEADS, nc, 1, LANES), F32)),
        grid=(bsz, nc),
        in_specs=[pl.BlockSpec((1, c, wdt), lambda b, j: (b, j, 0)),
                  pl.BlockSpec((1, SUBLANES, wdt), lambda b, j: (b, jnp.maximum(j * per - 1, 0), 0)),
                  pl.BlockSpec((1, SUBLANES, wdt), lambda b, j: (b, jnp.minimum((j + 1) * per, rows8 - 1), 0)),
                  pl.BlockSpec((1, c, LANES), lambda b, j: (b, j, 0)),
                  pl.BlockSpec((SUBLANES, wdt), lambda b, j: (0, 0)),
                  pl.BlockSpec((SUBLANES, LANES), lambda b, j: (0, 0))],
        out_specs=(pl.BlockSpec((1, 2, DN_HEADS, c, HEAD_DIM), chain),
                   pl.BlockSpec((1, 2, DN_HEADS, c, HEAD_DIM), chain),
                   pl.BlockSpec((1, 2, DN_HEADS, c, HEAD_DIM), chain),
                   pl.BlockSpec((1, 2, DN_HEADS, HEAD_DIM, c), lambda b, j: (b, 0, 0, 0, j)),
                   pl.BlockSpec((1, 2, DN_HEADS, c, c), chain),
                   pl.BlockSpec((1, 2, DN_HEADS, 1, 1, LANES), lambda b, j: (b, 0, 0, j, 0, 0))),
        scratch_shapes=[pltpu.VMEM((c + 2 * SUBLANES, wdt), F32)],
        compiler_params=_params(("parallel", "parallel")),
        name="dn_prepare",
    )(dnqkv, dnqkv, dnqkv, ba, conv_w, gate_par)


def _dn_scan_kernel(*refs, bsz):
    ins = refs[:12]
    of_ref, ob_ref, s_sc = refs[12:]
    n = pl.program_id(0)

    @pl.when(n == 0)
    def _():
        s_sc[...] = jnp.zeros(s_sc.shape, F32)

    chains = [(dr, b, hd) for dr in range(2) for b in range(bsz) for hd in range(DN_HEADS)]
    dotf = functools.partial(jnp.dot, preferred_element_type=F32)

    def inp(dr, k):
        return ins[dr * 6 + k]

    state = [s_sc[ci] for ci in range(len(chains))]
    r = [dotf(jnp.concatenate([inp(dr, 0)[b, 0, hd], inp(dr, 2)[b, 0, hd]], axis=0), state[ci].astype(BF16))
         for ci, (dr, b, hd) in enumerate(chains)]
    v_new = [(inp(dr, 1)[b, 0, hd] - r[ci][0:DN_CHUNK]).astype(BF16) for ci, (dr, b, hd) in enumerate(chains)]
    intra = [dotf(inp(dr, 4)[b, 0, hd], v_new[ci]) for ci, (dr, b, hd) in enumerate(chains)]
    upd = [dotf(inp(dr, 3)[b, 0, hd], v_new[ci]) for ci, (dr, b, hd) in enumerate(chains)]
    for ci, (dr, b, hd) in enumerate(chains):
        o_ref = of_ref if dr == 0 else ob_ref
        o_ref[b, :, hd * HEAD_DIM:(hd + 1) * HEAD_DIM] = r[ci][DN_CHUNK:] + intra[ci]
        s_sc[ci] = state[ci] * inp(dr, 5)[b, 0, hd, 0] + upd[ci]


def dn_scan(w, u, qg, kdt, qk, dl, ctx_len):
    bsz, _, _, s, _ = w.shape
    c = DN_CHUNK
    nc = s // c
    cc = ctx_len // c

    def bwd_chunk(n):
        return jnp.where(n < cc, cc - 1 - n, nc - 1 - (n - cc))

    in_specs, args = [], []
    for dr in range(2):
        pos = (lambda n: n) if dr == 0 else bwd_chunk
        tokm = lambda n, dr=dr, pos=pos: (0, dr, 0, pos(n), 0)
        for arr in (w, u, qg):
            in_specs.append(pl.BlockSpec((bsz, 1, DN_HEADS, c, HEAD_DIM), tokm))
            args.append(arr)
        in_specs.append(pl.BlockSpec((bsz, 1, DN_HEADS, HEAD_DIM, c), lambda n, dr=dr, pos=pos: (0, dr, 0, 0, pos(n))))
        args.append(kdt)
        in_specs.append(pl.BlockSpec((bsz, 1, DN_HEADS, c, c), tokm))
        args.append(qk)
        in_specs.append(pl.BlockSpec((bsz, 1, DN_HEADS, 1, 1, LANES), lambda n, dr=dr, pos=pos: (0, dr, 0, pos(n), 0, 0)))
        args.append(dl)
    return pl.pallas_call(
        functools.partial(_dn_scan_kernel, bsz=bsz),
        out_shape=(jax.ShapeDtypeStruct((bsz, s, DN_W), F32), jax.ShapeDtypeStruct((bsz, s, DN_W), F32)),
        grid=(nc,),
        in_specs=in_specs,
        out_specs=(pl.BlockSpec((bsz, c, DN_W), lambda n: (0, n, 0)),
                   pl.BlockSpec((bsz, c, DN_W), lambda n: (0, bwd_chunk(n), 0))),
        scratch_shapes=[pltpu.VMEM((2 * bsz * DN_HEADS, HEAD_DIM, HEAD_DIM), F32)],
        compiler_params=_params(("arbitrary",)),
        name="dn_scan",
    )(*args)


def _outproj_kernel(x_ref, mod_ref, attn_ctx_ref, attn_lat_ref, of_ref, ob_ref, gate_ref, dnw_ref, wo_ref,
                    n2w_ref, x_out, h2_out):
    d = x_ref.shape[-1]
    o = of_ref[0] + ob_ref[0]
    gate = gate_ref[0]
    parts = [jnp.where(pl.program_id(1) == 0, attn_ctx_ref[0], attn_lat_ref[0])]
    for hd in range(DN_HEADS):
        sl = slice(hd * HEAD_DIM, (hd + 1) * HEAD_DIM)
        g = gate[:, sl]
        parts.append((_head_rmsnorm(o[:, sl], dnw_ref[...]) * (g * jax.nn.sigmoid(g))).astype(BF16))
    mix = jnp.concatenate(parts, axis=1)
    y = jnp.dot(mix, wo_ref[...], preferred_element_type=F32)
    x = x_ref[0] + mod_ref[0, 0, :, 2 * d:3 * d] * y
    x_out[0] = x
    h2_out[0] = _modulated_norm(x, n2w_ref[...], mod_ref[0, 0, :, 3 * d:4 * d],
                                mod_ref[0, 0, :, 4 * d:5 * d]).astype(BF16)


def out_projection(xs, mod, attn_ctx, attn_lat, o_f, o_b, gate, dn_norm_w, w_out, norm2_w):
    bsz, s, d = xs.shape
    tm = TOKEN_BLOCK
    assert attn_ctx.shape[1] == tm
    tok = lambda b, j: (b, j, 0)
    full2 = lambda b, j: (0, 0)
    return pl.pallas_call(
        _outproj_kernel,
        out_shape=(jax.ShapeDtypeStruct((bsz, s, d), F32), jax.ShapeDtypeStruct((bsz, s, d), BF16)),
        grid=(bsz, s // tm),
        in_specs=[pl.BlockSpec((1, tm, d), tok),
                  pl.BlockSpec((1, 1, 1, mod.shape[-1]), _mod_index),
                  pl.BlockSpec((1, tm, ATTN_Q_W), lambda b, j: (b, 0, 0)),
                  pl.BlockSpec((1, tm, ATTN_Q_W), lambda b, j: (b, jnp.maximum(j - 1, 0), 0)),
                  pl.BlockSpec((1, tm, DN_W), tok),
                  pl.BlockSpec((1, tm, DN_W), tok),
                  pl.BlockSpec((1, tm, DN_W), tok),
                  pl.BlockSpec((1, HEAD_DIM), full2),
                  pl.BlockSpec(w_out.shape, full2),
                  pl.BlockSpec((1, d), full2)],
        out_specs=(pl.BlockSpec((1, tm, d), tok), pl.BlockSpec((1, tm, d), tok)),
        compiler_params=_params(("parallel", "parallel")),
        name="out_projection",
    )(xs, mod, attn_ctx, attn_lat, o_f, o_b, gate, dn_norm_w.reshape(1, HEAD_DIM), w_out,
      norm2_w.reshape(1, d))


def _peer_score_kernel(h2_ref, wq_ref, sk_ref, st_out):
    q = jnp.dot(h2_ref[0], wq_ref[...], preferred_element_type=F32).astype(BF16)
    for hp in range(2 * PEER_HEADS):
        st_out[0, hp] = lax.dot_general(sk_ref[hp], q[:, hp * PEER_HALF:(hp + 1) * PEER_HALF],
                                        (((1,), (1,)), ((), ())), preferred_element_type=F32)


def peer_scores(h2, wq, subkeys):
    bsz, s, d = h2.shape
    tm = TOKEN_BLOCK
    nhp = 2 * PEER_HEADS
    return pl.pallas_call(
        _peer_score_kernel,
        out_shape=jax.ShapeDtypeStruct((bsz, nhp, N_KEYS, s), F32),
        grid=(bsz, s // tm),
        in_specs=[pl.BlockSpec((1, tm, d), lambda b, j: (b, j, 0)),
                  pl.BlockSpec(wq.shape, lambda b, j: (0, 0)),
                  pl.BlockSpec(subkeys.shape, lambda b, j: (0, 0, 0))],
        out_specs=pl.BlockSpec((1, nhp, N_KEYS, tm), lambda b, j: (b, 0, 0, j)),
        compiler_params=_params(("parallel", "parallel")),
        name="peer_scores",
    )(h2, wq, subkeys)


def _sorting_network(n):
    pairs = []
    p = 1
    while p < n:
        k = p
        while k >= 1:
            for j in range(k % p, n - k, 2 * k):
                for i in range(min(k, n - j - k)):
                    if (i + j) // (2 * p) == (i + j + k) // (2 * p):
                        pairs.append((i + j, i + j + k))
            k //= 2
        p *= 2
    return pairs


def _sorted_top(s, k, with_rank):
    n_tiles = s.shape[0] // SUBLANES
    v = [s[i * SUBLANES:(i + 1) * SUBLANES] for i in range(n_tiles)]
    for lo, hi in _sorting_network(n_tiles):
        v[lo], v[hi] = jnp.maximum(v[lo], v[hi]), jnp.minimum(v[lo], v[hi])
    out = []
    for r in range(k):
        m = jnp.max(v[0], axis=0, keepdims=True)
        out.append(m)
        hit = v[0] == m
        for i in range(min(n_tiles, k - 1 - r)):
            v[i] = jnp.where(hit, v[i + 1] if i + 1 < n_tiles else NEG_BIG, v[i])
    rank = None
    if with_rank:
        rank = jnp.full(s.shape, float(k), F32)
        for r in reversed(range(k)):
            rank = jnp.where(s >= out[r], float(r), rank)
    return out, rank


def _peer_topk_kernel(st_ref, cnt_out, e0_out, rank_out, e1_out):
    nt = PEER_TOPK + 1
    tops = ([], [])
    for hd in range(PEER_HEADS):
        a, _ = _sorted_top(st_ref[0, 2 * hd], nt, False)
        b, rank = _sorted_top(st_ref[0, 2 * hd + 1], nt, True)
        rank_out[0, hd] = rank.astype(BF16)
        tops[0].append(a)
        tops[1].append(b)
    a8 = [jnp.concatenate([tops[0][hd][r] for hd in range(PEER_HEADS)], axis=0) for r in range(nt)]
    b8 = [jnp.concatenate([tops[1][hd][r] for hd in range(PEER_HEADS)], axis=0) for r in range(nt)]
    cand = [a8[i] + b8[j] for i in range(nt) for j in range(nt) if (i + 1) * (j + 1) <= nt]
    top = []
    for _ in range(nt):
        m = functools.reduce(jnp.maximum, cand)
        top.append(m)
        cand = [jnp.where(t == m, NEG_BIG, t) for t in cand]
    tau8 = 0.5 * (top[PEER_TOPK - 1] + top[PEER_TOPK])
    smax = a8[0] + b8[0]
    z8 = jnp.exp(top[0] - smax)
    for t in top[1:PEER_TOPK]:
        z8 = z8 + jnp.exp(t - smax)
    rz8 = 1.0 / z8
    for hd in range(PEER_HEADS):
        s0 = st_ref[0, 2 * hd]
        s1 = st_ref[0, 2 * hd + 1]
        thr = tau8[hd:hd + 1] - s0
        cnt = jnp.zeros(s0.shape, F32)
        for r in range(nt):
            cnt = jnp.where(tops[1][hd][r] > thr, float(r + 1), cnt)
        cnt_out[0, hd] = cnt
        e0_out[0, hd] = jnp.exp(s0 - tops[0][hd][0])
        e1_out[0, hd] = (jnp.exp(s1 - tops[1][hd][0]) * (rz8[hd:hd + 1] * GELU_GATE_SCALE)).astype(BF16)


def peer_topk(st):
    bsz, nhp, nk, s = st.shape
    tl = LANES
    spec = pl.BlockSpec((1, PEER_HEADS, nk, tl), lambda b, j: (b, 0, 0, j))
    words = jax.ShapeDtypeStruct((bsz, PEER_HEADS, nk, s), F32)
    halfs = jax.ShapeDtypeStruct((bsz, PEER_HEADS, nk, s), BF16)
    return pl.pallas_call(
        _peer_topk_kernel,
        out_shape=(words, words, halfs, halfs),
        grid=(bsz, s // tl),
        in_specs=[pl.BlockSpec((1, nhp, nk, tl), lambda b, j: (b, 0, 0, j))],
        out_specs=(spec, spec, spec, spec),
        compiler_params=_params(("parallel", "parallel")),
        name="peer_topk",
    )(st)


def _peer_expert_kernel(x_ref, mod_ref, h2_ref, u_ref, vt_ref, cnt_ref, e0_ref, rank_ref, e1_ref,
                        x_out, acc_sc, *, ctx_len):
    tok_block = pl.program_id(1)
    ec = pl.program_id(2)
    n_ec = pl.num_programs(2)
    d = x_ref.shape[-1]
    tb = x_ref.shape[1]
    pk = 2 * SUBLANES

    @pl.when(ec == 0)
    def _():
        acc_sc[...] = jnp.zeros(acc_sc.shape, F32)

    def row_tile(ref, hd, ii):
        return jnp.broadcast_to(ref[0, hd, ii:ii + 1, :], (pk, tb)).astype(BF16)

    h2 = h2_ref[0]
    pair = 2 * N_KEYS
    n_pairs = PEER_I_PER_STEP // 2

    def activations(p):
        return lax.dot_general(u_ref[p * pair:(p + 1) * pair, :], h2, (((1,), (1,)), ((), ())),
                               preferred_element_type=F32)

    def gate_weights(p):
        tiles = []
        for ii in (2 * p, 2 * p + 1):
            wt = [jnp.zeros((pk, tb), BF16) for _ in range(N_KEYS // pk)]
            for hd in range(PEER_HEADS):
                cnt = row_tile(cnt_ref, hd, ii)
                e0 = row_tile(e0_ref, hd, ii)
                for rt in range(N_KEYS // pk):
                    rows = slice(rt * pk, (rt + 1) * pk)
                    sel = jnp.where(rank_ref[0, hd, rows, :] < cnt, e1_ref[0, hd, rows, :],
                                    jnp.zeros((), BF16))
                    wt[rt] = wt[rt] + sel * e0
            tiles += wt
        return tiles

    def gated(at2, tiles):
        g = []
        for rt, wt in enumerate(tiles):
            a = at2[rt * pk:(rt + 1) * pk]
            act = a * (1.0 + lax.erf(a))
            g.append(act.astype(BF16) * wt)
        return jnp.concatenate(g, axis=0)

    g = [gated(activations(p), gate_weights(p)) for p in range(n_pairs)]
    acc_sc[...] += jnp.dot(vt_ref[...], jnp.concatenate(g, axis=0), preferred_element_type=F32)

    @pl.when(ec == n_ec - 1)
    def _():
        tok = tok_block * tb + lax.broadcasted_iota(jnp.int32, (tb, 1), 0)
        g2 = jnp.where(tok < ctx_len, mod_ref[0, 0, :, 5 * d:6 * d], mod_ref[0, 1, :, 5 * d:6 * d])
        x_out[0] = x_ref[0] + g2 * acc_sc[...].T


def peer_experts(xs, mod, h2, u_tab, vt_tab, cnt, e0, rank, e1, ctx_len):
    bsz, s, d = xs.shape
    tb = PEER_TOKEN_BLOCK if s % PEER_TOKEN_BLOCK == 0 else TOKEN_BLOCK
    ech = PEER_I_PER_STEP * N_KEYS
    n_exp = u_tab.shape[0]
    tok = lambda b, j, e: (b, j, 0)
    per_tok = lambda b, j, e: (b, 0, 0, j)
    per_i = lambda b, j, e: (b, 0, e, j)
    once = dict(pipeline_mode=pl.Buffered(1))
    return pl.pallas_call(
        functools.partial(_peer_expert_kernel, ctx_len=ctx_len),
        out_shape=jax.ShapeDtypeStruct((bsz, s, d), F32),
        grid=(bsz, s // tb, n_exp // ech),
        in_specs=[pl.BlockSpec((1, tb, d), tok, **once),
                  pl.BlockSpec((1, 2, 1, mod.shape[-1]), lambda b, j, e: (b, 0, 0, 0)),
                  pl.BlockSpec((1, tb, d), tok, **once),
                  pl.BlockSpec((ech, d), lambda b, j, e: (e, 0)),
                  pl.BlockSpec((d, ech), lambda b, j, e: (0, e)),
                  pl.BlockSpec((1, PEER_HEADS, PEER_I_PER_STEP, tb), per_i),
                  pl.BlockSpec((1, PEER_HEADS, PEER_I_PER_STEP, tb), per_i),
                  pl.BlockSpec((1, PEER_HEADS, N_KEYS, tb), per_tok, **once),
                  pl.BlockSpec((1, PEER_HEADS, N_KEYS, tb), per_tok, **once)],
        out_specs=pl.BlockSpec((1, tb, d), tok, **once),
        scratch_shapes=[pltpu.VMEM((d, tb), F32)],
        compiler_params=_params(("parallel", "parallel", "arbitrary")),
        name="peer_experts",
    )(xs, mod, h2, u_tab, vt_tab, cnt, e0, rank, e1)


def _rope_tables(ctx_len, n_lat):
    rows = n_lat // GRID_W
    row = jnp.repeat(jnp.arange(rows, dtype=F32), GRID_W)
    col = jnp.tile(jnp.arange(GRID_W, dtype=F32), rows)
    axis_dim = HEAD_DIM // 2
    inv_freq = ROPE_THETA ** (-jnp.arange(0, axis_dim, 2, dtype=F32) / axis_dim)
    ang_r = row[:, None] * inv_freq[None, :]
    ang_c = col[:, None] * inv_freq[None, :]
    ang = jnp.concatenate([ang_r, ang_r, ang_c, ang_c], axis=-1)
    cos, sin = jnp.cos(ang), jnp.sin(ang)
    first = (jnp.arange(HEAD_DIM) % (HEAD_DIM // 2)) < (HEAD_DIM // 4)
    sina = jnp.where(first, -sin, 0.0)
    sinb = jnp.where(first, 0.0, sin)
    pad = lambda t, v: jnp.concatenate([jnp.full((ctx_len, HEAD_DIM), v, F32), t], axis=0)
    return pad(cos, 1.0), pad(sina, 0.0), pad(sinb, 0.0)


def kernel(x, c, ctx, c_ctx, ada_w, ada_b, norm1_w, norm2_w, w_in, attn_qnorm_w, attn_knorm_w, dn_conv_w,
           dn_A_log, dn_dt_bias, dn_norm_w, w_out, peer_wq, peer_subkeys, peer_u, peer_v):
    bsz, n_lat, d = x.shape
    ctx_len = ctx.shape[1]
    depth = ada_w.shape[0]
    assert ctx_len == TOKEN_BLOCK and n_lat % TOKEN_BLOCK == 0 and bsz + 1 <= SUBLANES
    assert w_in.shape[-1] == IN_MAIN_W + N_GATE_COLS

    xs = jnp.concatenate([ctx, x], axis=1)
    cos, sina, sinb = _rope_tables(ctx_len, n_lat)
    q_lead_blocks = (-ctx_len % latent_query_block(ctx_len, n_lat)) // TOKEN_BLOCK

    cc = jnp.zeros((SUBLANES, d), F32).at[:bsz].set(c).at[bsz].set(c_ctx)
    mod_all = ada_modulation(cc, ada_w, ada_b)
    mod_ctx = jnp.broadcast_to(mod_all[:, bsz][:, None], (depth, bsz, 6 * d))
    mod = jnp.stack([mod_ctx, mod_all[:, :bsz]], axis=2)[:, :, :, None, :]

    for l in range(depth):
        w_main = w_in[l, :, :IN_MAIN_W].astype(BF16)
        w_gate = jnp.pad(w_in[l, :, IN_MAIN_W:], ((0, 0), (0, LANES - N_GATE_COLS))).astype(BF16)
        q, k, v, dnqkv, gate, ba = in_projection(xs, mod[l], norm1_w[l], w_main, w_gate, cos, sina, sinb,
                                                 attn_qnorm_w[l], attn_knorm_w[l], q_lead_blocks)
        attn_ctx, attn_lat = attention(q, k, v, ctx_len)
        conv_w = jnp.pad(dn_conv_w[l], ((0, SUBLANES - CONV_K), (0, 0)))
        gate_par = jnp.zeros((SUBLANES, LANES), F32)
        gate_par = gate_par.at[0, 2 * DN_HEADS:4 * DN_HEADS].set(dn_A_log[l].reshape(-1))
        gate_par = gate_par.at[1, 2 * DN_HEADS:4 * DN_HEADS].set(dn_dt_bias[l].reshape(-1))
        o_f, o_b = dn_scan(*dn_prepare(dnqkv, ba, conv_w, gate_par, ctx_len), ctx_len)
        xs, h2 = out_projection(xs, mod[l], attn_ctx, attn_lat, o_f, o_b, gate, dn_norm_w[l],
                                w_out[l].astype(BF16), norm2_w[l])
        sk = peer_subkeys[l].reshape(2 * PEER_HEADS, N_KEYS, PEER_HALF).astype(BF16)
        st = peer_scores(h2, peer_wq[l].astype(BF16), sk)
        cnt, e0, rank, e1 = peer_topk(st)
        xs = peer_experts(xs, mod[l], h2, (peer_u[l] * GELU_GATE_SCALE).astype(BF16), peer_v[l].T.astype(BF16), cnt, e0, rank, e1,
                          ctx_len)
    return xs[:, ctx_len:]
```

```python
import functools
import math

import jax
import jax.numpy as jnp
from jax import lax
from jax.experimental import pallas as pl
from jax.experimental.pallas import tpu as pltpu

F32 = jnp.float32
BF16 = jnp.bfloat16
HIGHEST = lax.Precision.HIGHEST

HEAD_DIM = 128
ATTN_HEADS = 4
ATTN_KV_HEADS = 2
ATTN_GROUP = ATTN_HEADS // ATTN_KV_HEADS
DN_HEADS = 4
ATTN_Q_W = ATTN_HEADS * HEAD_DIM
ATTN_KV_W = ATTN_KV_HEADS * HEAD_DIM
DN_W = DN_HEADS * HEAD_DIM
IN_MAIN_W = ATTN_Q_W + 2 * ATTN_KV_W + 3 * DN_W + DN_W
N_GATE_COLS = 4 * DN_HEADS
ROPE_THETA = 10000.0
GRID_W = 64
CONV_K = 5
CONV_PAD = CONV_K // 2
PEER_HEADS = 8
PEER_HALF = 128
N_KEYS = 128
PEER_TOPK = 16
EPS = 1e-6
NEG_BIG = -1e30
GELU_GATE_SCALE = 2.0 ** -0.5

LANES = 128
SUBLANES = 8
MXU_TILE = 256
TOKEN_BLOCK = 256
DN_CHUNK = 128
DN_BASE_BLOCK = 2
ATTN_KEY_BLOCK = 3328
ATTN_Q_BLOCK = 512
ATTN_CHAIN_ROWS = 256
PEER_TOKEN_BLOCK = 1280
PEER_I_PER_STEP = 8
VMEM_LIMIT = 56 * 1024 * 1024


def _params(sem):
    return pltpu.CompilerParams(dimension_semantics=sem, vmem_limit_bytes=VMEM_LIMIT)


def _bf16_dot(a, b):
    return jnp.dot(a.astype(BF16), b.astype(BF16), preferred_element_type=F32)


def _split_dot(a, b):
    a_hi = a.astype(BF16)
    b_hi = b.astype(BF16)
    a_lo = (a - a_hi.astype(F32)).astype(BF16)
    b_lo = (b - b_hi.astype(F32)).astype(BF16)
    dot = functools.partial(jnp.dot, preferred_element_type=F32)
    return dot(a_hi, b_hi) + (dot(a_hi, b_lo) + dot(a_lo, b_hi))


def _bf16_dot_nt(a, b):
    return lax.dot_general(a.astype(BF16), b.astype(BF16), (((1,), (1,)), ((), ())),
                           preferred_element_type=F32)


def _ada_kernel(c_ref, w_ref, b_ref, o_ref):
    c = c_ref[...]
    a = c * jax.nn.sigmoid(c)
    o_ref[0] = jnp.dot(a, w_ref[0], preferred_element_type=F32, precision=HIGHEST) + b_ref[0]


def ada_modulation(cc, ada_w, ada_b):
    depth, d, n = ada_w.shape
    tn = 1536
    return pl.pallas_call(
        _ada_kernel,
        out_shape=jax.ShapeDtypeStruct((depth, SUBLANES, n), F32),
        grid=(depth, n // tn),
        in_specs=[pl.BlockSpec((SUBLANES, d), lambda l, j: (0, 0)),
                  pl.BlockSpec((1, d, tn), lambda l, j: (l, 0, j)),
                  pl.BlockSpec((1, 1, tn), lambda l, j: (l, 0, j))],
        out_specs=pl.BlockSpec((1, SUBLANES, tn), lambda l, j: (l, 0, j)),
        compiler_params=_params(("parallel", "parallel")),
        name="ada_modulation",
    )(cc, ada_w, ada_b.reshape(depth, 1, n))


def _mod_index(b, j):
    return (b, jnp.minimum(j, 1), 0, 0)


def _modulated_norm(x, nw, shift, scale):
    ms = jnp.mean(x * x, axis=-1, keepdims=True)
    y = x * lax.rsqrt(ms + EPS) * nw
    return y * (1.0 + scale) + shift


def _head_rmsnorm(x, w):
    return x * lax.rsqrt(jnp.mean(x * x, axis=-1, keepdims=True) + EPS) * w


def _inproj_kernel(x_ref, mod_ref, nw_ref, wm_ref, wg_ref, cos_ref, sina_ref, sinb_ref, qnw_ref, knw_ref,
                   q_out, k_out, v_out, dn_out, gate_out, ba_out):
    d = x_ref.shape[-1]
    x = x_ref[0]
    h = _modulated_norm(x, nw_ref[...], mod_ref[0, 0, :, 0:d], mod_ref[0, 0, :, d:2 * d]).astype(BF16)
    p = jnp.dot(h, wm_ref[...], preferred_element_type=F32)
    ba_out[0] = jnp.dot(h, wg_ref[...], preferred_element_type=F32)
    cos = cos_ref[...]
    sina = sina_ref[...]
    sinb = sinb_ref[...]

    def rope(t):
        return (t * cos + pltpu.roll(t, HEAD_DIM - HEAD_DIM // 4, 1) * sina
                + pltpu.roll(t, HEAD_DIM // 4, 1) * sinb)

    scale = HEAD_DIM ** -0.5 * math.log2(math.e)
    for hd in range(ATTN_HEADS):
        qh = _head_rmsnorm(p[:, hd * HEAD_DIM:(hd + 1) * HEAD_DIM], qnw_ref[...])
        q_out[0, hd] = (rope(qh) * scale).astype(BF16)
    off = ATTN_Q_W
    for hd in range(ATTN_KV_HEADS):
        kh = _head_rmsnorm(p[:, off + hd * HEAD_DIM:off + (hd + 1) * HEAD_DIM], knw_ref[...])
        k_out[0, hd] = rope(kh).astype(BF16)
    off += ATTN_KV_W
    for hd in range(ATTN_KV_HEADS):
        v_out[0, hd, :, 0:HEAD_DIM] = p[:, off + hd * HEAD_DIM:off + (hd + 1) * HEAD_DIM].astype(BF16)
        v_out[0, hd, :, HEAD_DIM:2 * HEAD_DIM] = jnp.ones((x.shape[0], HEAD_DIM), BF16)
    off += ATTN_KV_W
    dn_out[0] = p[:, off:off + 3 * DN_W]
    off += 3 * DN_W
    gate_out[0] = p[:, off:off + DN_W]


def in_projection(xs, mod, norm_w, w_main, w_gate, cos, sina, sinb, qn_w, kn_w, q_lead_blocks):
    bsz, s, d = xs.shape
    tm = TOKEN_BLOCK
    tok = lambda b, j: (b, j, 0)
    head_tok = lambda b, j: (b, 0, j, 0)
    full2 = lambda b, j: (0, 0)
    rope_spec = pl.BlockSpec((tm, HEAD_DIM), lambda b, j: (j, 0))
    return pl.pallas_call(
        _inproj_kernel,
        out_shape=(jax.ShapeDtypeStruct((bsz, ATTN_HEADS, q_lead_blocks * tm + s, HEAD_DIM), BF16),
                   jax.ShapeDtypeStruct((bsz, ATTN_KV_HEADS, s, HEAD_DIM), BF16),
                   jax.ShapeDtypeStruct((bsz, ATTN_KV_HEADS, s, 2 * HEAD_DIM), BF16),
                   jax.ShapeDtypeStruct((bsz, s, 3 * DN_W), F32),
                   jax.ShapeDtypeStruct((bsz, s, DN_W), F32),
                   jax.ShapeDtypeStruct((bsz, s, LANES), F32)),
        grid=(bsz, s // tm),
        in_specs=[pl.BlockSpec((1, tm, d), tok),
                  pl.BlockSpec((1, 1, 1, mod.shape[-1]), _mod_index),
                  pl.BlockSpec((1, d), full2),
                  pl.BlockSpec((d, IN_MAIN_W), full2),
                  pl.BlockSpec((d, LANES), full2),
                  rope_spec, rope_spec, rope_spec,
                  pl.BlockSpec((1, HEAD_DIM), full2),
                  pl.BlockSpec((1, HEAD_DIM), full2)],
        out_specs=(pl.BlockSpec((1, ATTN_HEADS, tm, HEAD_DIM), lambda b, j: (b, 0, j + q_lead_blocks, 0)),
                   pl.BlockSpec((1, ATTN_KV_HEADS, tm, HEAD_DIM), head_tok),
                   pl.BlockSpec((1, ATTN_KV_HEADS, tm, 2 * HEAD_DIM), head_tok),
                   pl.BlockSpec((1, tm, 3 * DN_W), tok),
                   pl.BlockSpec((1, tm, DN_W), tok),
                   pl.BlockSpec((1, tm, LANES), tok)),
        compiler_params=_params(("parallel", "parallel")),
        name="in_projection",
    )(xs, mod, norm_w.reshape(1, d), w_main, w_gate, cos, sina, sinb,
      qn_w.reshape(1, HEAD_DIM), kn_w.reshape(1, HEAD_DIM))


def _attn_kernel(q_ref, k_ref, v_ref, o_ref, m_sc, acc_sc):
    ki = pl.program_id(2)
    nk = pl.num_programs(2)

    @pl.when(ki == 0)
    def _():
        m_sc[...] = jnp.full(m_sc.shape, NEG_BIG, F32)
        acc_sc[...] = jnp.zeros(acc_sc.shape, F32)

    tq = q_ref.shape[2]
    chains = [(hd, pl.ds(r0, ATTN_CHAIN_ROWS)) for hd in range(ATTN_HEADS)
              for r0 in range(0, tq, ATTN_CHAIN_ROWS)]
    scores = [lax.dot_general(q_ref[0, hd, rows, :], k_ref[0, hd // ATTN_GROUP], (((1,), (1,)), ((), ())),
                              preferred_element_type=F32) for hd, rows in chains]
    for (hd, rows), s in zip(chains, scores):
        m_prev = m_sc[hd, rows, :]
        m_new = jnp.maximum(m_prev, jnp.max(s, axis=-1, keepdims=True))
        p = jnp.exp2(s - m_new)
        alpha = jnp.exp2(m_prev - m_new)
        acc_sc[hd, rows, :] = alpha * acc_sc[hd, rows, :] + jnp.dot(
            p.astype(BF16), v_ref[0, hd // ATTN_GROUP], preferred_element_type=F32)
        m_sc[hd, rows, :] = m_new

    @pl.when(ki == nk - 1)
    def _():
        for hd in range(ATTN_HEADS):
            o_ref[0, :, hd * HEAD_DIM:(hd + 1) * HEAD_DIM] = (
                acc_sc[hd, :, 0:HEAD_DIM] / acc_sc[hd, :, HEAD_DIM:2 * HEAD_DIM]).astype(o_ref.dtype)


def _attention_call(q, k, v, q_row0, n_q, tq, n_keys, tk, name):
    bsz = q.shape[0]
    n_q_blocks = n_q // tq
    q_block0 = q_row0 // tq
    assert q_block0 * tq == q_row0
    return pl.pallas_call(
        _attn_kernel,
        out_shape=jax.ShapeDtypeStruct((bsz, n_q, ATTN_Q_W), BF16),
        grid=(bsz, n_q_blocks, n_keys // tk),
        in_specs=[pl.BlockSpec((1, ATTN_HEADS, tq, HEAD_DIM), lambda b, i, j: (b, 0, i + q_block0, 0)),
                  pl.BlockSpec((1, ATTN_KV_HEADS, tk, HEAD_DIM), lambda b, i, j: (b, 0, j, 0)),
                  pl.BlockSpec((1, ATTN_KV_HEADS, tk, 2 * HEAD_DIM), lambda b, i, j: (b, 0, j, 0))],
        out_specs=pl.BlockSpec((1, tq, ATTN_Q_W), lambda b, i, j: (b, i, 0)),
        scratch_shapes=[pltpu.VMEM((ATTN_HEADS, tq, 1), F32),
                        pltpu.VMEM((ATTN_HEADS, tq, 2 * HEAD_DIM), F32)],
        compiler_params=_params(("parallel", "parallel", "arbitrary")),
        name=name,
    )(q, k, v)


def latent_query_block(ctx_len, n_lat):
    return ATTN_Q_BLOCK if n_lat % ATTN_Q_BLOCK == 0 and ATTN_Q_BLOCK % ctx_len == 0 else ATTN_CHAIN_ROWS


def attention(q, k, v, ctx_len):
    s = k.shape[2]
    n_lat = s - ctx_len
    lead = q.shape[2] - s
    assert ctx_len % ATTN_CHAIN_ROWS == 0 and n_lat % ATTN_CHAIN_ROWS == 0
    tk = ATTN_KEY_BLOCK if s % ATTN_KEY_BLOCK == 0 else ATTN_CHAIN_ROWS
    tq = latent_query_block(ctx_len, n_lat)
    attn_ctx = _attention_call(q, k, v, lead, ctx_len, ctx_len, ctx_len, ctx_len, "attention_ctx")
    attn_lat = _attention_call(q, k, v, lead + ctx_len, n_lat, tq, s, tk, "attention")
    return attn_ctx, attn_lat


def _dn_prep_kernel(main_ref, prev_ref, next_ref, ba_ref, cw_ref, gp_ref,
                    w_out, u_out, qg_out, kdt_out, qk_out, dl_out, ext_sc, *, ctx_chunks):
    j = pl.program_id(1)
    nj = pl.num_programs(1)
    c = DN_CHUNK
    has_prev = (j != 0) & (j != ctx_chunks)
    has_next = (j != ctx_chunks - 1) & (j != nj - 1)
    ext_sc[0:SUBLANES] = jnp.where(has_prev, prev_ref[0], 0.0)
    ext_sc[SUBLANES:SUBLANES + c] = main_ref[0]
    ext_sc[SUBLANES + c:2 * SUBLANES + c] = jnp.where(has_next, next_ref[0], 0.0)
    y = ext_sc[SUBLANES - CONV_PAD:SUBLANES - CONV_PAD + c] * cw_ref[0:1]
    for t in range(1, CONV_K):
        y = y + ext_sc[SUBLANES - CONV_PAD + t:SUBLANES - CONV_PAD + t + c] * cw_ref[t:t + 1]
    y = y * jax.nn.sigmoid(y)

    ba = ba_ref[0]
    beta_all = jax.nn.sigmoid(ba)
    g_all = -jnp.exp(gp_ref[0:1]) * jax.nn.softplus(ba + gp_ref[1:2])
    row = lax.broadcasted_iota(jnp.int32, (c, c), 0)
    col = lax.broadcasted_iota(jnp.int32, (c, c), 1)
    lower = (row >= col).astype(BF16)
    g_hi = g_all.astype(BF16)
    g_r1 = g_all - g_hi.astype(F32)
    g_mid = g_r1.astype(BF16)
    g_lo = (g_r1 - g_mid.astype(F32)).astype(BF16)
    dotf = functools.partial(jnp.dot, preferred_element_type=F32)
    prefix = dotf(lower, g_hi) + (dotf(lower, g_mid) + dotf(lower, g_lo))
    total = prefix[c - 1:c]
    gc = (prefix, total - prefix + g_all)
    gct = (gc[0].T, gc[1].T)
    row2 = lax.broadcasted_iota(jnp.int32, (2 * c, 2 * c), 0)
    col2 = lax.broadcasted_iota(jnp.int32, (2 * c, 2 * c), 1)
    eye2 = (row2 == col2).astype(F32)
    zero = jnp.zeros((c, c), F32)

    heads = range(DN_HEADS)
    a2, rhs2 = [], []
    for hd in heads:
        q = y[:, hd * HEAD_DIM:(hd + 1) * HEAD_DIM]
        k = y[:, DN_W + hd * HEAD_DIM:DN_W + (hd + 1) * HEAD_DIM]
        v = y[:, 2 * DN_W + hd * HEAD_DIM:2 * DN_W + (hd + 1) * HEAD_DIM]
        q = q * lax.rsqrt(jnp.sum(q * q, axis=-1, keepdims=True) + EPS) * (HEAD_DIM ** -0.5)
        k = k * lax.rsqrt(jnp.sum(k * k, axis=-1, keepdims=True) + EPS)
        kk = _bf16_dot_nt(k, k)
        qk = _bf16_dot_nt(q, k)
        a_dir, rhs_dir = [], []
        for dr in range(2):
            cb = dr * DN_HEADS + hd
            cg = 2 * DN_HEADS + cb
            beta = beta_all[:, cb:cb + 1]
            gcol = gc[dr][:, cg:cg + 1]
            grow = gct[dr][cg:cg + 1, :]
            incl = (row >= col) if dr == 0 else (row <= col)
            strict = (row > col) if dr == 0 else (row < col)
            decay = jnp.exp(jnp.where(incl, gcol - grow, NEG_BIG))
            a_dir.append(jnp.where(strict, kk * beta * decay, 0.0))
            eg = jnp.exp(gcol)
            rhs_dir.append(jnp.concatenate([v * beta, k * (beta * eg)], axis=1))
            qg_out[0, dr, hd] = (q * eg).astype(BF16)
            tot = total[:, cg:cg + 1]
            kdt_out[0, dr, hd] = (k * jnp.exp(tot - gcol)).T.astype(BF16)
            qk_out[0, dr, hd] = (qk * decay).astype(BF16)
            dl_out[0, dr, hd, 0] = jnp.broadcast_to(jnp.exp(tot), (1, LANES))
        a2.append(jnp.concatenate([jnp.concatenate([a_dir[0], zero], axis=1),
                                   jnp.concatenate([zero, a_dir[1]], axis=1)], axis=0))
        rhs2.append(jnp.concatenate(rhs_dir, axis=0))
    base = DN_BASE_BLOCK
    base_mask = (row2 // base) == (col2 // base)
    a_base = [jnp.where(base_mask, a2[hd], 0.0) for hd in heads]
    x = [eye2 - a_base[hd] for hd in heads]
    if base > 2:
        pw = [_split_dot(a_base[hd], a_base[hd]) for hd in heads]
    for it in range(int(math.log2(base)) - 1):
        x = [x[hd] + _split_dot(x[hd], pw[hd]) for hd in heads]
        if it < int(math.log2(base)) - 2:
            pw = [_split_dot(pw[hd], pw[hd]) for hd in heads]
    blk = base
    while blk < c:
        sibling = ((row2 // (2 * blk)) == (col2 // (2 * blk))) & ((row2 // blk) != (col2 // blk))
        fold = [_bf16_dot(x[hd], jnp.where(sibling, a2[hd], 0.0)) for hd in heads]
        x = [x[hd] - _bf16_dot(fold[hd], x[hd]) for hd in heads]
        blk *= 2
    for hd in heads:
        uw = _bf16_dot(x[hd], rhs2[hd])
        for dr in range(2):
            u_out[0, dr, hd] = uw[dr * c:(dr + 1) * c, 0:HEAD_DIM]
            w_out[0, dr, hd] = uw[dr * c:(dr + 1) * c, HEAD_DIM:2 * HEAD_DIM].astype(BF16)


def dn_prepare(dnqkv, ba, conv_w, gate_par, ctx_len):
    bsz, s, wdt = dnqkv.shape
    c = DN_CHUNK
    nc = s // c
    rows8 = s // SUBLANES
    per = c // SUBLANES
    chain = lambda b, j: (b, 0, 0, j, 0)
    return pl.pallas_call(
        functools.partial(_dn_prep_kernel, ctx_chunks=ctx_len // c),
        out_shape=(jax.ShapeDtypeStruct((bsz, 2, DN_HEADS, s, HEAD_DIM), BF16),
                   jax.ShapeDtypeStruct((bsz, 2, DN_HEADS, s, HEAD_DIM), F32),
                   jax.ShapeDtypeStruct((bsz, 2, DN_HEADS, s, HEAD_DIM), BF16),
                   jax.ShapeDtypeStruct((bsz, 2, DN_HEADS, HEAD_DIM, s), BF16),
                   jax.ShapeDtypeStruct((bsz, 2, DN_HEADS, s, c), BF16),
                   jax.ShapeDtypeStruct((bsz, 2, DN_HEADS, nc, 1, LANES), F32)),
        grid=(bsz, nc),
        in_specs=[pl.BlockSpec((1, c, wdt), lambda b, j: (b, j, 0)),
                  pl.BlockSpec((1, SUBLANES, wdt), lambda b, j: (b, jnp.maximum(j * per - 1, 0), 0)),
                  pl.BlockSpec((1, SUBLANES, wdt), lambda b, j: (b, jnp.minimum((j + 1) * per, rows8 - 1), 0)),
                  pl.BlockSpec((1, c, LANES), lambda b, j: (b, j, 0)),
                  pl.BlockSpec((SUBLANES, wdt), lambda b, j: (0, 0)),
                  pl.BlockSpec((SUBLANES, LANES), lambda b, j: (0, 0))],
        out_specs=(pl.BlockSpec((1, 2, DN_HEADS, c, HEAD_DIM), chain),
                   pl.BlockSpec((1, 2, DN_HEADS, c, HEAD_DIM), chain),
                   pl.BlockSpec((1, 2, DN_HEADS, c, HEAD_DIM), chain),
                   pl.BlockSpec((1, 2, DN_HEADS, HEAD_DIM, c), lambda b, j: (b, 0, 0, 0, j)),
                   pl.BlockSpec((1, 2, DN_HEADS, c, c), chain),
                   pl.BlockSpec((1, 2, DN_HEADS, 1, 1, LANES), lambda b, j: (b, 0, 0, j, 0, 0))),
        scratch_shapes=[pltpu.VMEM((c + 2 * SUBLANES, wdt), F32)],
        compiler_params=_params(("parallel", "parallel")),
        name="dn_prepare",
    )(dnqkv, dnqkv, dnqkv, ba, conv_w, gate_par)


def _dn_scan_kernel(*refs, bsz):
    ins = refs[:12]
    of_ref, ob_ref, s_sc = refs[12:]
    n = pl.program_id(0)

    @pl.when(n == 0)
    def _():
        s_sc[...] = jnp.zeros(s_sc.shape, F32)

    chains = [(dr, b, hd) for dr in range(2) for b in range(bsz) for hd in range(DN_HEADS)]
    dotf = functools.partial(jnp.dot, preferred_element_type=F32)

    def inp(dr, k):
        return ins[dr * 6 + k]

    state = [s_sc[ci] for ci in range(len(chains))]
    r = [dotf(jnp.concatenate([inp(dr, 0)[b, 0, hd], inp(dr, 2)[b, 0, hd]], axis=0), state[ci].astype(BF16))
         for ci, (dr, b, hd) in enumerate(chains)]
    v_new = [(inp(dr, 1)[b, 0, hd] - r[ci][0:DN_CHUNK]).astype(BF16) for ci, (dr, b, hd) in enumerate(chains)]
    intra = [dotf(inp(dr, 4)[b, 0, hd], v_new[ci]) for ci, (dr, b, hd) in enumerate(chains)]
    upd = [dotf(inp(dr, 3)[b, 0, hd], v_new[ci]) for ci, (dr, b, hd) in enumerate(chains)]
    for ci, (dr, b, hd) in enumerate(chains):
        o_ref = of_ref if dr == 0 else ob_ref
        o_ref[b, :, hd * HEAD_DIM:(hd + 1) * HEAD_DIM] = r[ci][DN_CHUNK:] + intra[ci]
        s_sc[ci] = state[ci] * inp(dr, 5)[b, 0, hd, 0] + upd[ci]


def dn_scan(w, u, qg, kdt, qk, dl, ctx_len):
    bsz, _, _, s, _ = w.shape
    c = DN_CHUNK
    nc = s // c
    cc = ctx_len // c

    def bwd_chunk(n):
        return jnp.where(n < cc, cc - 1 - n, nc - 1 - (n - cc))

    in_specs, args = [], []
    for dr in range(2):
        pos = (lambda n: n) if dr == 0 else bwd_chunk
        tokm = lambda n, dr=dr, pos=pos: (0, dr, 0, pos(n), 0)
        for arr in (w, u, qg):
            in_specs.append(pl.BlockSpec((bsz, 1, DN_HEADS, c, HEAD_DIM), tokm))
            args.append(arr)
        in_specs.append(pl.BlockSpec((bsz, 1, DN_HEADS, HEAD_DIM, c), lambda n, dr=dr, pos=pos: (0, dr, 0, 0, pos(n))))
        args.append(kdt)
        in_specs.append(pl.BlockSpec((bsz, 1, DN_HEADS, c, c), tokm))
        args.append(qk)
        in_specs.append(pl.BlockSpec((bsz, 1, DN_HEADS, 1, 1, LANES), lambda n, dr=dr, pos=pos: (0, dr, 0, pos(n), 0, 0)))
        args.append(dl)
    return pl.pallas_call(
        functools.partial(_dn_scan_kernel, bsz=bsz),
        out_shape=(jax.ShapeDtypeStruct((bsz, s, DN_W), F32), jax.ShapeDtypeStruct((bsz, s, DN_W), F32)),
        grid=(nc,),
        in_specs=in_specs,
        out_specs=(pl.BlockSpec((bsz, c, DN_W), lambda n: (0, n, 0)),
                   pl.BlockSpec((bsz, c, DN_W), lambda n: (0, bwd_chunk(n), 0))),
        scratch_shapes=[pltpu.VMEM((2 * bsz * DN_HEADS, HEAD_DIM, HEAD_DIM), F32)],
        compiler_params=_params(("arbitrary",)),
        name="dn_scan",
    )(*args)


def _outproj_kernel(x_ref, mod_ref, attn_ctx_ref, attn_lat_ref, of_ref, ob_ref, gate_ref, dnw_ref, wo_ref,
                    n2w_ref, x_out, h2_out):
    d = x_ref.shape[-1]
    o = of_ref[0] + ob_ref[0]
    gate = gate_ref[0]
    parts = [jnp.where(pl.program_id(1) == 0, attn_ctx_ref[0], attn_lat_ref[0])]
    for hd in range(DN_HEADS):
        sl = slice(hd * HEAD_DIM, (hd + 1) * HEAD_DIM)
        g = gate[:, sl]
        parts.append((_head_rmsnorm(o[:, sl], dnw_ref[...]) * (g * jax.nn.sigmoid(g))).astype(BF16))
    mix = jnp.concatenate(parts, axis=1)
    y = jnp.dot(mix, wo_ref[...], preferred_element_type=F32)
    x = x_ref[0] + mod_ref[0, 0, :, 2 * d:3 * d] * y
    x_out[0] = x
    h2_out[0] = _modulated_norm(x, n2w_ref[...], mod_ref[0, 0, :, 3 * d:4 * d],
                                mod_ref[0, 0, :, 4 * d:5 * d]).astype(BF16)


def out_projection(xs, mod, attn_ctx, attn_lat, o_f, o_b, gate, dn_norm_w, w_out, norm2_w):
    bsz, s, d = xs.shape
    tm = TOKEN_BLOCK
    assert attn_ctx.shape[1] == tm
    tok = lambda b, j: (b, j, 0)
    full2 = lambda b, j: (0, 0)
    return pl.pallas_call(
        _outproj_kernel,
        out_shape=(jax.ShapeDtypeStruct((bsz, s, d), F32), jax.ShapeDtypeStruct((bsz, s, d), BF16)),
        grid=(bsz, s // tm),
        in_specs=[pl.BlockSpec((1, tm, d), tok),
                  pl.BlockSpec((1, 1, 1, mod.shape[-1]), _mod_index),
                  pl.BlockSpec((1, tm, ATTN_Q_W), lambda b, j: (b, 0, 0)),
                  pl.BlockSpec((1, tm, ATTN_Q_W), lambda b, j: (b, jnp.maximum(j - 1, 0), 0)),
                  pl.BlockSpec((1, tm, DN_W), tok),
                  pl.BlockSpec((1, tm, DN_W), tok),
                  pl.BlockSpec((1, tm, DN_W), tok),
                  pl.BlockSpec((1, HEAD_DIM), full2),
                  pl.BlockSpec(w_out.shape, full2),
                  pl.BlockSpec((1, d), full2)],
        out_specs=(pl.BlockSpec((1, tm, d), tok), pl.BlockSpec((1, tm, d), tok)),
        compiler_params=_params(("parallel", "parallel")),
        name="out_projection",
    )(xs, mod, attn_ctx, attn_lat, o_f, o_b, gate, dn_norm_w.reshape(1, HEAD_DIM), w_out,
      norm2_w.reshape(1, d))


def _peer_score_kernel(h2_ref, wq_ref, sk_ref, st_out):
    q = jnp.dot(h2_ref[0], wq_ref[...], preferred_element_type=F32).astype(BF16)
    for hp in range(2 * PEER_HEADS):
        st_out[0, hp] = lax.dot_general(sk_ref[hp], q[:, hp * PEER_HALF:(hp + 1) * PEER_HALF],
                                        (((1,), (1,)), ((), ())), preferred_element_type=F32)


def peer_scores(h2, wq, subkeys):
    bsz, s, d = h2.shape
    tm = TOKEN_BLOCK
    nhp = 2 * PEER_HEADS
    return pl.pallas_call(
        _peer_score_kernel,
        out_shape=jax.ShapeDtypeStruct((bsz, nhp, N_KEYS, s), F32),
        grid=(bsz, s // tm),
        in_specs=[pl.BlockSpec((1, tm, d), lambda b, j: (b, j, 0)),
                  pl.BlockSpec(wq.shape, lambda b, j: (0, 0)),
                  pl.BlockSpec(subkeys.shape, lambda b, j: (0, 0, 0))],
        out_specs=pl.BlockSpec((1, nhp, N_KEYS, tm), lambda b, j: (b, 0, 0, j)),
        compiler_params=_params(("parallel", "parallel")),
        name="peer_scores",
    )(h2, wq, subkeys)


def _sorting_network(n):
    pairs = []
    p = 1
    while p < n:
        k = p
        while k >= 1:
            for j in range(k % p, n - k, 2 * k):
                for i in range(min(k, n - j - k)):
                    if (i + j) // (2 * p) == (i + j + k) // (2 * p):
                        pairs.append((i + j, i + j + k))
            k //= 2
        p *= 2
    return pairs


def _sorted_top(s, k, with_rank):
    n_tiles = s.shape[0] // SUBLANES
    v = [s[i * SUBLANES:(i + 1) * SUBLANES] for i in range(n_tiles)]
    for lo, hi in _sorting_network(n_tiles):
        v[lo], v[hi] = jnp.maximum(v[lo], v[hi]), jnp.minimum(v[lo], v[hi])
    out = []
    for r in range(k):
        m = jnp.max(v[0], axis=0, keepdims=True)
        out.append(m)
        hit = v[0] == m
        for i in range(min(n_tiles, k - 1 - r)):
            v[i] = jnp.where(hit, v[i + 1] if i + 1 < n_tiles else NEG_BIG, v[i])
    rank = None
    if with_rank:
        rank = jnp.full(s.shape, float(k), F32)
        for r in reversed(range(k)):
            rank = jnp.where(s >= out[r], float(r), rank)
    return out, rank


def _peer_topk_kernel(st_ref, cnt_out, e0_out, rank_out, e1_out):
    nt = PEER_TOPK + 1
    tops = ([], [])
    for hd in range(PEER_HEADS):
        a, _ = _sorted_top(st_ref[0, 2 * hd], nt, False)
        b, rank = _sorted_top(st_ref[0, 2 * hd + 1], nt, True)
        rank_out[0, hd] = rank.astype(BF16)
        tops[0].append(a)
        tops[1].append(b)
    a8 = [jnp.concatenate([tops[0][hd][r] for hd in range(PEER_HEADS)], axis=0) for r in range(nt)]
    b8 = [jnp.concatenate([tops[1][hd][r] for hd in range(PEER_HEADS)], axis=0) for r in range(nt)]
    cand = [a8[i] + b8[j] for i in range(nt) for j in range(nt) if (i + 1) * (j + 1) <= nt]
    top = []
    for _ in range(nt):
        m = functools.reduce(jnp.maximum, cand)
        top.append(m)
        cand = [jnp.where(t == m, NEG_BIG, t) for t in cand]
    tau8 = 0.5 * (top[PEER_TOPK - 1] + top[PEER_TOPK])
    smax = a8[0] + b8[0]
    z8 = jnp.exp(top[0] - smax)
    for t in top[1:PEER_TOPK]:
        z8 = z8 + jnp.exp(t - smax)
    rz8 = 1.0 / z8
    for hd in range(PEER_HEADS):
        s0 = st_ref[0, 2 * hd]
        s1 = st_ref[0, 2 * hd + 1]
        thr = tau8[hd:hd + 1] - s0
        cnt = jnp.zeros(s0.shape, F32)
        for r in range(nt):
            cnt = jnp.where(tops[1][hd][r] > thr, float(r + 1), cnt)
        cnt_out[0, hd] = cnt
        e0_out[0, hd] = jnp.exp(s0 - tops[0][hd][0])
        e1_out[0, hd] = (jnp.exp(s1 - tops[1][hd][0]) * (rz8[hd:hd + 1] * GELU_GATE_SCALE)).astype(BF16)


def peer_topk(st):
    bsz, nhp, nk, s = st.shape
    tl = LANES
    spec = pl.BlockSpec((1, PEER_HEADS, nk, tl), lambda b, j: (b, 0, 0, j))
    words = jax.ShapeDtypeStruct((bsz, PEER_HEADS, nk, s), F32)
    halfs = jax.ShapeDtypeStruct((bsz, PEER_HEADS, nk, s), BF16)
    return pl.pallas_call(
        _peer_topk_kernel,
        out_shape=(words, words, halfs, halfs),
        grid=(bsz, s // tl),
        in_specs=[pl.BlockSpec((1, nhp, nk, tl), lambda b, j: (b, 0, 0, j))],
        out_specs=(spec, spec, spec, spec),
        compiler_params=_params(("parallel", "parallel")),
        name="peer_topk",
    )(st)


def _peer_expert_kernel(x_ref, mod_ref, h2_ref, u_ref, vt_ref, cnt_ref, e0_ref, rank_ref, e1_ref,
                        x_out, acc_sc, *, ctx_len):
    tok_block = pl.program_id(1)
    ec = pl.program_id(2)
    n_ec = pl.num_programs(2)
    d = x_ref.shape[-1]
    tb = x_ref.shape[1]
    pk = 2 * SUBLANES

    @pl.when(ec == 0)
    def _():
        acc_sc[...] = jnp.zeros(acc_sc.shape, F32)

    def row_tile(ref, hd, ii):
        return jnp.broadcast_to(ref[0, hd, ii:ii + 1, :], (pk, tb)).astype(BF16)

    h2 = h2_ref[0]
    pair = 2 * N_KEYS
    n_pairs = PEER_I_PER_STEP // 2

    def activations(p):
        return lax.dot_general(u_ref[p * pair:(p + 1) * pair, :], h2, (((1,), (1,)), ((), ())),
                               preferred_element_type=F32)

    def gate_weights(p):
        tiles = []
        for ii in (2 * p, 2 * p + 1):
            wt = [jnp.zeros((pk, tb), BF16) for _ in range(N_KEYS // pk)]
            for hd in range(PEER_HEADS):
                cnt = row_tile(cnt_ref, hd, ii)
                e0 = row_tile(e0_ref, hd, ii)
                for rt in range(N_KEYS // pk):
                    rows = slice(rt * pk, (rt + 1) * pk)
                    sel = jnp.where(rank_ref[0, hd, rows, :] < cnt, e1_ref[0, hd, rows, :],
                                    jnp.zeros((), BF16))
                    wt[rt] = wt[rt] + sel * e0
            tiles += wt
        return tiles

    def gated(at2, tiles):
        g = []
        for rt, wt in enumerate(tiles):
            a = at2[rt * pk:(rt + 1) * pk]
            act = a * (1.0 + lax.erf(a))
            g.append(act.astype(BF16) * wt)
        return jnp.concatenate(g, axis=0)

    g = [gated(activations(p), gate_weights(p)) for p in range(n_pairs)]
    acc_sc[...] += jnp.dot(vt_ref[...], jnp.concatenate(g, axis=0), preferred_element_type=F32)

    @pl.when(ec == n_ec - 1)
    def _():
        tok = tok_block * tb + lax.broadcasted_iota(jnp.int32, (tb, 1), 0)
        g2 = jnp.where(tok < ctx_len, mod_ref[0, 0, :, 5 * d:6 * d], mod_ref[0, 1, :, 5 * d:6 * d])
        x_out[0] = x_ref[0] + g2 * acc_sc[...].T


def peer_experts(xs, mod, h2, u_tab, vt_tab, cnt, e0, rank, e1, ctx_len):
    bsz, s, d = xs.shape
    tb = PEER_TOKEN_BLOCK if s % PEER_TOKEN_BLOCK == 0 else TOKEN_BLOCK
    ech = PEER_I_PER_STEP * N_KEYS
    n_exp = u_tab.shape[0]
    tok = lambda b, j, e: (b, j, 0)
    per_tok = lambda b, j, e: (b, 0, 0, j)
    per_i = lambda b, j, e: (b, 0, e, j)
    once = dict(pipeline_mode=pl.Buffered(1))
    return pl.pallas_call(
        functools.partial(_peer_expert_kernel, ctx_len=ctx_len),
        out_shape=jax.ShapeDtypeStruct((bsz, s, d), F32),
        grid=(bsz, s // tb, n_exp // ech),
        in_specs=[pl.BlockSpec((1, tb, d), tok, **once),
                  pl.BlockSpec((1, 2, 1, mod.shape[-1]), lambda b, j, e: (b, 0, 0, 0)),
                  pl.BlockSpec((1, tb, d), tok, **once),
                  pl.BlockSpec((ech, d), lambda b, j, e: (e, 0)),
                  pl.BlockSpec((d, ech), lambda b, j, e: (0, e)),
                  pl.BlockSpec((1, PEER_HEADS, PEER_I_PER_STEP, tb), per_i),
                  pl.BlockSpec((1, PEER_HEADS, PEER_I_PER_STEP, tb), per_i),
                  pl.BlockSpec((1, PEER_HEADS, N_KEYS, tb), per_tok, **once),
                  pl.BlockSpec((1, PEER_HEADS, N_KEYS, tb), per_tok, **once)],
        out_specs=pl.BlockSpec((1, tb, d), tok, **once),
        scratch_shapes=[pltpu.VMEM((d, tb), F32)],
        compiler_params=_params(("parallel", "parallel", "arbitrary")),
        name="peer_experts",
    )(xs, mod, h2, u_tab, vt_tab, cnt, e0, rank, e1)


def _rope_tables(ctx_len, n_lat):
    rows = n_lat // GRID_W
    row = jnp.repeat(jnp.arange(rows, dtype=F32), GRID_W)
    col = jnp.tile(jnp.arange(GRID_W, dtype=F32), rows)
    axis_dim = HEAD_DIM // 2
    inv_freq = ROPE_THETA ** (-jnp.arange(0, axis_dim, 2, dtype=F32) / axis_dim)
    ang_r = row[:, None] * inv_freq[None, :]
    ang_c = col[:, None] * inv_freq[None, :]
    ang = jnp.concatenate([ang_r, ang_r, ang_c, ang_c], axis=-1)
    cos, sin = jnp.cos(ang), jnp.sin(ang)
    first = (jnp.arange(HEAD_DIM) % (HEAD_DIM // 2)) < (HEAD_DIM // 4)
    sina = jnp.where(first, -sin, 0.0)
    sinb = jnp.where(first, 0.0, sin)
    pad = lambda t, v: jnp.concatenate([jnp.full((ctx_len, HEAD_DIM), v, F32), t], axis=0)
    return pad(cos, 1.0), pad(sina, 0.0), pad(sinb, 0.0)


def kernel(x, c, ctx, c_ctx, ada_w, ada_b, norm1_w, norm2_w, w_in, attn_qnorm_w, attn_knorm_w, dn_conv_w,
           dn_A_log, dn_dt_bias, dn_norm_w, w_out, peer_wq, peer_subkeys, peer_u, peer_v):
    bsz, n_lat, d = x.shape
    ctx_len = ctx.shape[1]
    depth = ada_w.shape[0]
    assert ctx_len == TOKEN_BLOCK and n_lat % TOKEN_BLOCK == 0 and bsz + 1 <= SUBLANES
    assert w_in.shape[-1] == IN_MAIN_W + N_GATE_COLS

    xs = jnp.concatenate([ctx, x], axis=1)
    cos, sina, sinb = _rope_tables(ctx_len, n_lat)
    q_lead_blocks = (-ctx_len % latent_query_block(ctx_len, n_lat)) // TOKEN_BLOCK

    cc = jnp.zeros((SUBLANES, d), F32).at[:bsz].set(c).at[bsz].set(c_ctx)
    mod_all = ada_modulation(cc, ada_w, ada_b)
    mod_ctx = jnp.broadcast_to(mod_all[:, bsz][:, None], (depth, bsz, 6 * d))
    mod = jnp.stack([mod_ctx, mod_all[:, :bsz]], axis=2)[:, :, :, None, :]

    for l in range(depth):
        w_main = w_in[l, :, :IN_MAIN_W].astype(BF16)
        w_gate = jnp.pad(w_in[l, :, IN_MAIN_W:], ((0, 0), (0, LANES - N_GATE_COLS))).astype(BF16)
        q, k, v, dnqkv, gate, ba = in_projection(xs, mod[l], norm1_w[l], w_main, w_gate, cos, sina, sinb,
                                                 attn_qnorm_w[l], attn_knorm_w[l], q_lead_blocks)
        attn_ctx, attn_lat = attention(q, k, v, ctx_len)
        conv_w = jnp.pad(dn_conv_w[l], ((0, SUBLANES - CONV_K), (0, 0)))
        gate_par = jnp.zeros((SUBLANES, LANES), F32)
        gate_par = gate_par.at[0, 2 * DN_HEADS:4 * DN_HEADS].set(dn_A_log[l].reshape(-1))
        gate_par = gate_par.at[1, 2 * DN_HEADS:4 * DN_HEADS].set(dn_dt_bias[l].reshape(-1))
        o_f, o_b = dn_scan(*dn_prepare(dnqkv, ba, conv_w, gate_par, ctx_len), ctx_len)
        xs, h2 = out_projection(xs, mod[l], attn_ctx, attn_lat, o_f, o_b, gate, dn_norm_w[l],
                                w_out[l].astype(BF16), norm2_w[l])
        sk = peer_subkeys[l].reshape(2 * PEER_HEADS, N_KEYS, PEER_HALF).astype(BF16)
        st = peer_scores(h2, peer_wq[l].astype(BF16), sk)
        cnt, e0, rank, e1 = peer_topk(st)
        xs = peer_experts(xs, mod[l], h2, (peer_u[l] * GELU_GATE_SCALE).astype(BF16), peer_v[l].T.astype(BF16), cnt, e0, rank, e1,
                          ctx_len)
    return xs[:, ctx_len:]
```

```python
import functools
import math

import jax
import jax.numpy as jnp
from jax import lax
from jax.experimental import pallas as pl
from jax.experimental.pallas import tpu as pltpu

F32 = jnp.float32
BF16 = jnp.bfloat16
HIGHEST = lax.Precision.HIGHEST

HEAD_DIM = 128
ATTN_HEADS = 4
ATTN_KV_HEADS = 2
ATTN_GROUP = ATTN_HEADS // ATTN_KV_HEADS
DN_HEADS = 4
ATTN_Q_W = ATTN_HEADS * HEAD_DIM
ATTN_KV_W = ATTN_KV_HEADS * HEAD_DIM
DN_W = DN_HEADS * HEAD_DIM
IN_MAIN_W = ATTN_Q_W + 2 * ATTN_KV_W + 3 * DN_W + DN_W
N_GATE_COLS = 4 * DN_HEADS
ROPE_THETA = 10000.0
GRID_W = 64
CONV_K = 5
CONV_PAD = CONV_K // 2
PEER_HEADS = 8
PEER_HALF = 128
N_KEYS = 128
PEER_TOPK = 16
EPS = 1e-6
NEG_BIG = -1e30
GELU_GATE_SCALE = 2.0 ** -0.5

LANES = 128
SUBLANES = 8
TOKEN_BLOCK = 256
ADA_COL_BLOCK = 1536
DN_CHUNK = 128
ATTN_KEY_BLOCK = 3328
ATTN_Q_BLOCK = 512
ATTN_CHAIN_ROWS = 256
PEER_TOKEN_BLOCK = 1280
PEER_I_PER_STEP = 8
VMEM_LIMIT = 56 * 1024 * 1024


def _params(sem):
    return pltpu.CompilerParams(dimension_semantics=sem, vmem_limit_bytes=VMEM_LIMIT)


def _bf16_dot(a, b):
    return jnp.dot(a.astype(BF16), b.astype(BF16), preferred_element_type=F32)


def _bf16_dot_nt(a, b):
    return lax.dot_general(a.astype(BF16), b.astype(BF16), (((1,), (1,)), ((), ())),
                           preferred_element_type=F32)


def _ada_kernel(c_ref, w_ref, b_ref, o_ref):
    c = c_ref[...]
    a = c * jax.nn.sigmoid(c)
    o_ref[0] = jnp.dot(a, w_ref[0], preferred_element_type=F32, precision=HIGHEST) + b_ref[0]


def ada_modulation(cc, ada_w, ada_b):
    depth, d, n = ada_w.shape
    tn = ADA_COL_BLOCK
    return pl.pallas_call(
        _ada_kernel,
        out_shape=jax.ShapeDtypeStruct((depth, SUBLANES, n), F32),
        grid=(depth, n // tn),
        in_specs=[pl.BlockSpec((SUBLANES, d), lambda l, j: (0, 0)),
                  pl.BlockSpec((1, d, tn), lambda l, j: (l, 0, j)),
                  pl.BlockSpec((1, 1, tn), lambda l, j: (l, 0, j))],
        out_specs=pl.BlockSpec((1, SUBLANES, tn), lambda l, j: (l, 0, j)),
        compiler_params=_params(("parallel", "parallel")),
        name="ada_modulation",
    )(cc, ada_w, ada_b.reshape(depth, 1, n))


def _mod_index(b, j):
    return (b, jnp.minimum(j, 1), 0, 0)


def _modulated_norm(x, nw, shift, scale):
    ms = jnp.mean(x * x, axis=-1, keepdims=True)
    y = x * lax.rsqrt(ms + EPS) * nw
    return y * (1.0 + scale) + shift


def _head_rmsnorm(x, w):
    return x * lax.rsqrt(jnp.mean(x * x, axis=-1, keepdims=True) + EPS) * w


def _inproj_kernel(x_ref, mod_ref, nw_ref, wm_ref, wg_ref, cos_ref, sina_ref, sinb_ref, qnw_ref, knw_ref,
                   q_out, k_out, v_out, dn_out, gate_out, ba_out):
    d = x_ref.shape[-1]
    x = x_ref[0]
    h = _modulated_norm(x, nw_ref[...], mod_ref[0, 0, :, 0:d], mod_ref[0, 0, :, d:2 * d]).astype(BF16)
    p = jnp.dot(h, wm_ref[...], preferred_element_type=F32)
    ba_out[0] = jnp.dot(h, wg_ref[...], preferred_element_type=F32)
    cos = cos_ref[...]
    sina = sina_ref[...]
    sinb = sinb_ref[...]

    def rope(t):
        return (t * cos + pltpu.roll(t, HEAD_DIM - HEAD_DIM // 4, 1) * sina
                + pltpu.roll(t, HEAD_DIM // 4, 1) * sinb)

    scale = HEAD_DIM ** -0.5 * math.log2(math.e)
    for hd in range(ATTN_HEADS):
        qh = _head_rmsnorm(p[:, hd * HEAD_DIM:(hd + 1) * HEAD_DIM], qnw_ref[...])
        q_out[0, hd] = (rope(qh) * scale).astype(BF16)
    off = ATTN_Q_W
    for hd in range(ATTN_KV_HEADS):
        kh = _head_rmsnorm(p[:, off + hd * HEAD_DIM:off + (hd + 1) * HEAD_DIM], knw_ref[...])
        k_out[0, hd] = rope(kh).astype(BF16)
    off += ATTN_KV_W
    for hd in range(ATTN_KV_HEADS):
        v_out[0, hd, :, 0:HEAD_DIM] = p[:, off + hd * HEAD_DIM:off + (hd + 1) * HEAD_DIM].astype(BF16)
        v_out[0, hd, :, HEAD_DIM:2 * HEAD_DIM] = jnp.ones((x.shape[0], HEAD_DIM), BF16)
    off += ATTN_KV_W
    dn_out[0] = p[:, off:off + 3 * DN_W]
    off += 3 * DN_W
    gate_out[0] = p[:, off:off + DN_W]


def in_projection(xs, mod, norm_w, w_main, w_gate, cos, sina, sinb, qn_w, kn_w, q_lead_blocks):
    bsz, s, d = xs.shape
    tm = TOKEN_BLOCK
    blk = lambda j: jnp.maximum(j - q_lead_blocks, 0)
    tok = lambda b, j: (b, blk(j), 0)
    head_tok = lambda b, j: (b, 0, blk(j), 0)
    full2 = lambda b, j: (0, 0)
    rope_spec = pl.BlockSpec((tm, HEAD_DIM), lambda b, j: (blk(j), 0))
    return pl.pallas_call(
        _inproj_kernel,
        out_shape=(jax.ShapeDtypeStruct((bsz, ATTN_HEADS, q_lead_blocks * tm + s, HEAD_DIM), BF16),
                   jax.ShapeDtypeStruct((bsz, ATTN_KV_HEADS, s, HEAD_DIM), BF16),
                   jax.ShapeDtypeStruct((bsz, ATTN_KV_HEADS, s, 2 * HEAD_DIM), BF16),
                   jax.ShapeDtypeStruct((bsz, s, 3 * DN_W), F32),
                   jax.ShapeDtypeStruct((bsz, s, DN_W), F32),
                   jax.ShapeDtypeStruct((bsz, s, LANES), F32)),
        grid=(bsz, q_lead_blocks + s // tm),
        in_specs=[pl.BlockSpec((1, tm, d), tok),
                  pl.BlockSpec((1, 1, 1, mod.shape[-1]), lambda b, j: _mod_index(b, blk(j))),
                  pl.BlockSpec((1, d), full2),
                  pl.BlockSpec((d, IN_MAIN_W), full2),
                  pl.BlockSpec((d, LANES), full2),
                  rope_spec, rope_spec, rope_spec,
                  pl.BlockSpec((1, HEAD_DIM), full2),
                  pl.BlockSpec((1, HEAD_DIM), full2)],
        out_specs=(pl.BlockSpec((1, ATTN_HEADS, tm, HEAD_DIM), lambda b, j: (b, 0, j, 0)),
                   pl.BlockSpec((1, ATTN_KV_HEADS, tm, HEAD_DIM), head_tok),
                   pl.BlockSpec((1, ATTN_KV_HEADS, tm, 2 * HEAD_DIM), head_tok),
                   pl.BlockSpec((1, tm, 3 * DN_W), tok),
                   pl.BlockSpec((1, tm, DN_W), tok),
                   pl.BlockSpec((1, tm, LANES), tok)),
        compiler_params=_params(("parallel", "arbitrary")),
        name="in_projection",
    )(xs, mod, norm_w.reshape(1, d), w_main, w_gate, cos, sina, sinb,
      qn_w.reshape(1, HEAD_DIM), kn_w.reshape(1, HEAD_DIM))


def _attn_kernel(q_ref, k_ref, v_ref, o_ref, m_sc, acc_sc):
    ki = pl.program_id(2)
    nk = pl.num_programs(2)

    @pl.when(ki == 0)
    def _():
        m_sc[...] = jnp.full(m_sc.shape, NEG_BIG, F32)
        acc_sc[...] = jnp.zeros(acc_sc.shape, F32)

    tq = q_ref.shape[2]
    chains = [(hd, pl.ds(r0, ATTN_CHAIN_ROWS)) for hd in range(ATTN_HEADS)
              for r0 in range(0, tq, ATTN_CHAIN_ROWS)]
    scores = [lax.dot_general(q_ref[0, hd, rows, :], k_ref[0, hd // ATTN_GROUP], (((1,), (1,)), ((), ())),
                              preferred_element_type=F32) for hd, rows in chains]
    for (hd, rows), s in zip(chains, scores):
        m_prev = m_sc[hd, rows, :]
        m_new = jnp.maximum(m_prev, jnp.max(s, axis=-1, keepdims=True))
        p = jnp.exp2(s - m_new)
        alpha = jnp.exp2(m_prev - m_new)
        acc_sc[hd, rows, :] = alpha * acc_sc[hd, rows, :] + jnp.dot(
            p.astype(BF16), v_ref[0, hd // ATTN_GROUP], preferred_element_type=F32)
        m_sc[hd, rows, :] = m_new

    @pl.when(ki == nk - 1)
    def _():
        for hd in range(ATTN_HEADS):
            o_ref[0, :, hd * HEAD_DIM:(hd + 1) * HEAD_DIM] = (
                acc_sc[hd, :, 0:HEAD_DIM] / acc_sc[hd, :, HEAD_DIM:2 * HEAD_DIM]).astype(o_ref.dtype)


def _attention_call(q, k, v, q_row0, n_q, tq, n_keys, tk, name):
    bsz = q.shape[0]
    n_q_blocks = n_q // tq
    q_block0 = q_row0 // tq
    assert q_block0 * tq == q_row0
    return pl.pallas_call(
        _attn_kernel,
        out_shape=jax.ShapeDtypeStruct((bsz, n_q, ATTN_Q_W), BF16),
        grid=(bsz, n_q_blocks, n_keys // tk),
        in_specs=[pl.BlockSpec((1, ATTN_HEADS, tq, HEAD_DIM), lambda b, i, j: (b, 0, i + q_block0, 0)),
                  pl.BlockSpec((1, ATTN_KV_HEADS, tk, HEAD_DIM), lambda b, i, j: (b, 0, j, 0)),
                  pl.BlockSpec((1, ATTN_KV_HEADS, tk, 2 * HEAD_DIM), lambda b, i, j: (b, 0, j, 0))],
        out_specs=pl.BlockSpec((1, tq, ATTN_Q_W), lambda b, i, j: (b, i, 0)),
        scratch_shapes=[pltpu.VMEM((ATTN_HEADS, tq, 1), F32),
                        pltpu.VMEM((ATTN_HEADS, tq, 2 * HEAD_DIM), F32)],
        compiler_params=_params(("parallel", "parallel", "arbitrary")),
        name=name,
    )(q, k, v)


def latent_query_block(ctx_len, n_lat):
    return ATTN_Q_BLOCK if n_lat % ATTN_Q_BLOCK == 0 and ATTN_Q_BLOCK % ctx_len == 0 else ATTN_CHAIN_ROWS


def attention(q, k, v, ctx_len):
    s = k.shape[2]
    n_lat = s - ctx_len
    lead = q.shape[2] - s
    assert ctx_len % ATTN_CHAIN_ROWS == 0 and n_lat % ATTN_CHAIN_ROWS == 0
    tk = ATTN_KEY_BLOCK if s % ATTN_KEY_BLOCK == 0 else ATTN_CHAIN_ROWS
    tq = latent_query_block(ctx_len, n_lat)
    attn_ctx = _attention_call(q, k, v, lead, ctx_len, ctx_len, ctx_len, ctx_len, "attention_ctx")
    attn_lat = _attention_call(q, k, v, lead + ctx_len, n_lat, tq, s, tk, "attention")
    return attn_ctx, attn_lat


def _dn_prep_kernel(main_ref, prev_ref, next_ref, ba_ref, cw_ref, gp_ref,
                    w_out, u_out, qg_out, kdt_out, qk_out, dl_out, ext_sc, *, ctx_chunks):
    j = pl.program_id(1)
    nj = pl.num_programs(1)
    c = DN_CHUNK
    has_prev = (j != 0) & (j != ctx_chunks)
    has_next = (j != ctx_chunks - 1) & (j != nj - 1)
    ext_sc[0:SUBLANES] = jnp.where(has_prev, prev_ref[0], 0.0)
    ext_sc[SUBLANES:SUBLANES + c] = main_ref[0]
    ext_sc[SUBLANES + c:2 * SUBLANES + c] = jnp.where(has_next, next_ref[0], 0.0)
    y = ext_sc[SUBLANES - CONV_PAD:SUBLANES - CONV_PAD + c] * cw_ref[0:1]
    for t in range(1, CONV_K):
        y = y + ext_sc[SUBLANES - CONV_PAD + t:SUBLANES - CONV_PAD + t + c] * cw_ref[t:t + 1]
    y = y * jax.nn.sigmoid(y)

    ba = ba_ref[0]
    beta_all = jax.nn.sigmoid(ba)
    g_all = -jnp.exp(gp_ref[0:1]) * jax.nn.softplus(ba + gp_ref[1:2])
    row = lax.broadcasted_iota(jnp.int32, (c, c), 0)
    col = lax.broadcasted_iota(jnp.int32, (c, c), 1)
    lower = (row >= col).astype(BF16)
    g_hi = g_all.astype(BF16)
    g_r1 = g_all - g_hi.astype(F32)
    g_mid = g_r1.astype(BF16)
    g_lo = (g_r1 - g_mid.astype(F32)).astype(BF16)
    dotf = functools.partial(jnp.dot, preferred_element_type=F32)
    prefix = dotf(lower, g_hi) + (dotf(lower, g_mid) + dotf(lower, g_lo))
    total = prefix[c - 1:c]
    gc = (prefix, total - prefix + g_all)
    gct = (gc[0].T, gc[1].T)
    row2 = lax.broadcasted_iota(jnp.int32, (2 * c, 2 * c), 0)
    col2 = lax.broadcasted_iota(jnp.int32, (2 * c, 2 * c), 1)
    eye2 = (row2 == col2).astype(F32)
    zero = jnp.zeros((c, c), F32)

    heads = range(DN_HEADS)
    a2, rhs2 = [], []
    for hd in heads:
        q = y[:, hd * HEAD_DIM:(hd + 1) * HEAD_DIM]
        k = y[:, DN_W + hd * HEAD_DIM:DN_W + (hd + 1) * HEAD_DIM]
        v = y[:, 2 * DN_W + hd * HEAD_DIM:2 * DN_W + (hd + 1) * HEAD_DIM]
        q = q * lax.rsqrt(jnp.sum(q * q, axis=-1, keepdims=True) + EPS) * (HEAD_DIM ** -0.5)
        k = k * lax.rsqrt(jnp.sum(k * k, axis=-1, keepdims=True) + EPS)
        kk = _bf16_dot_nt(k, k)
        qk = _bf16_dot_nt(q, k)
        a_dir, rhs_dir = [], []
        for dr in range(2):
            cb = dr * DN_HEADS + hd
            cg = 2 * DN_HEADS + cb
            beta = beta_all[:, cb:cb + 1]
            gcol = gc[dr][:, cg:cg + 1]
            grow = gct[dr][cg:cg + 1, :]
            incl = (row >= col) if dr == 0 else (row <= col)
            strict = (row > col) if dr == 0 else (row < col)
            decay = jnp.exp(jnp.where(incl, gcol - grow, NEG_BIG))
            a_dir.append(jnp.where(strict, kk * beta * decay, 0.0))
            eg = jnp.exp(gcol)
            rhs_dir.append(jnp.concatenate([v * beta, k * (beta * eg)], axis=1))
            qg_out[0, dr, hd] = (q * eg).astype(BF16)
            tot = total[:, cg:cg + 1]
            kdt_out[0, dr, hd] = (k * jnp.exp(tot - gcol)).T.astype(BF16)
            qk_out[0, dr, hd] = (qk * decay).astype(BF16)
            dl_out[0, dr, hd, 0] = jnp.broadcast_to(jnp.exp(tot), (1, LANES))
        a2.append(jnp.concatenate([jnp.concatenate([a_dir[0], zero], axis=1),
                                   jnp.concatenate([zero, a_dir[1]], axis=1)], axis=0))
        rhs2.append(jnp.concatenate(rhs_dir, axis=0))
    blk = 2
    base_mask = (row2 // blk) == (col2 // blk)
    x = [eye2 - jnp.where(base_mask, a2[hd], 0.0) for hd in heads]
    while blk < c:
        sibling = ((row2 // (2 * blk)) == (col2 // (2 * blk))) & ((row2 // blk) != (col2 // blk))
        fold = [_bf16_dot(x[hd], jnp.where(sibling, a2[hd], 0.0)) for hd in heads]
        x = [x[hd] - _bf16_dot(fold[hd], x[hd]) for hd in heads]
        blk *= 2
    for hd in heads:
        uw = _bf16_dot(x[hd], rhs2[hd])
        for dr in range(2):
            u_out[0, dr, hd] = uw[dr * c:(dr + 1) * c, 0:HEAD_DIM]
            w_out[0, dr, hd] = uw[dr * c:(dr + 1) * c, HEAD_DIM:2 * HEAD_DIM].astype(BF16)


def dn_prepare(dnqkv, ba, conv_w, gate_par, ctx_len):
    bsz, s, wdt = dnqkv.shape
    c = DN_CHUNK
    nc = s // c
    rows8 = s // SUBLANES
    per = c // SUBLANES
    chain = lambda b, j: (b, 0, 0, j, 0)
    return pl.pallas_call(
        functools.partial(_dn_prep_kernel, ctx_chunks=ctx_len // c),
        out_shape=(jax.ShapeDtypeStruct((bsz, 2, DN_HEADS, s, HEAD_DIM), BF16),
                   jax.ShapeDtypeStruct((bsz, 2, DN_HEADS, s, HEAD_DIM), F32),
                   jax.ShapeDtypeStruct((bsz, 2, DN_HEADS, s, HEAD_DIM), BF16),
                   jax.ShapeDtypeStruct((bsz, 2, DN_HEADS, HEAD_DIM, s), BF16),
                   jax.ShapeDtypeStruct((bsz, 2, DN_HEADS, s, c), BF16),
                   jax.ShapeDtypeStruct((bsz, 2, DN_HEADS, nc, 1, LANES), F32)),
        grid=(bsz, nc),
        in_specs=[pl.BlockSpec((1, c, wdt), lambda b, j: (b, j, 0)),
                  pl.BlockSpec((1, SUBLANES, wdt), lambda b, j: (b, jnp.maximum(j * per - 1, 0), 0)),
                  pl.BlockSpec((1, SUBLANES, wdt), lambda b, j: (b, jnp.minimum((j + 1) * per, rows8 - 1), 0)),
                  pl.BlockSpec((1, c, LANES), lambda b, j: (b, j, 0)),
                  pl.BlockSpec((SUBLANES, wdt), lambda b, j: (0, 0)),
                  pl.BlockSpec((SUBLANES, LANES), lambda b, j: (0, 0))],
        out_specs=(pl.BlockSpec((1, 2, DN_HEADS, c, HEAD_DIM), chain),
                   pl.BlockSpec((1, 2, DN_HEADS, c, HEAD_DIM), chain),
                   pl.BlockSpec((1, 2, DN_HEADS, c, HEAD_DIM), chain),
                   pl.BlockSpec((1, 2, DN_HEADS, HEAD_DIM, c), lambda b, j: (b, 0, 0, 0, j)),
                   pl.BlockSpec((1, 2, DN_HEADS, c, c), chain),
                   pl.BlockSpec((1, 2, DN_HEADS, 1, 1, LANES), lambda b, j: (b, 0, 0, j, 0, 0))),
        scratch_shapes=[pltpu.VMEM((c + 2 * SUBLANES, wdt), F32)],
        compiler_params=_params(("parallel", "parallel")),
        name="dn_prepare",
    )(dnqkv, dnqkv, dnqkv, ba, conv_w, gate_par)


def _dn_scan_kernel(*refs, bsz):
    ins = refs[:12]
    of_ref, ob_ref, s_sc = refs[12:]
    n = pl.program_id(0)

    @pl.when(n == 0)
    def _():
        s_sc[...] = jnp.zeros(s_sc.shape, F32)

    chains = [(dr, b, hd) for dr in range(2) for b in range(bsz) for hd in range(DN_HEADS)]
    dotf = functools.partial(jnp.dot, preferred_element_type=F32)

    def inp(dr, k):
        return ins[dr * 6 + k]

    state = [s_sc[ci] for ci in range(len(chains))]
    r = [dotf(jnp.concatenate([inp(dr, 0)[b, 0, hd], inp(dr, 2)[b, 0, hd]], axis=0), state[ci].astype(BF16))
         for ci, (dr, b, hd) in enumerate(chains)]
    v_new = [(inp(dr, 1)[b, 0, hd] - r[ci][0:DN_CHUNK]).astype(BF16) for ci, (dr, b, hd) in enumerate(chains)]
    intra = [dotf(inp(dr, 4)[b, 0, hd], v_new[ci]) for ci, (dr, b, hd) in enumerate(chains)]
    upd = [dotf(inp(dr, 3)[b, 0, hd], v_new[ci]) for ci, (dr, b, hd) in enumerate(chains)]
    for ci, (dr, b, hd) in enumerate(chains):
        o_ref = of_ref if dr == 0 else ob_ref
        o_ref[b, :, hd * HEAD_DIM:(hd + 1) * HEAD_DIM] = r[ci][DN_CHUNK:] + intra[ci]
        s_sc[ci] = state[ci] * inp(dr, 5)[b, 0, hd, 0] + upd[ci]


def dn_scan(w, u, qg, kdt, qk, dl, ctx_len):
    bsz, _, _, s, _ = w.shape
    c = DN_CHUNK
    nc = s // c
    cc = ctx_len // c

    def bwd_chunk(n):
        return jnp.where(n < cc, cc - 1 - n, nc - 1 - (n - cc))

    in_specs, args = [], []
    for dr in range(2):
        pos = (lambda n: n) if dr == 0 else bwd_chunk
        tokm = lambda n, dr=dr, pos=pos: (0, dr, 0, pos(n), 0)
        for arr in (w, u, qg):
            in_specs.append(pl.BlockSpec((bsz, 1, DN_HEADS, c, HEAD_DIM), tokm))
            args.append(arr)
        in_specs.append(pl.BlockSpec((bsz, 1, DN_HEADS, HEAD_DIM, c), lambda n, dr=dr, pos=pos: (0, dr, 0, 0, pos(n))))
        args.append(kdt)
        in_specs.append(pl.BlockSpec((bsz, 1, DN_HEADS, c, c), tokm))
        args.append(qk)
        in_specs.append(pl.BlockSpec((bsz, 1, DN_HEADS, 1, 1, LANES), lambda n, dr=dr, pos=pos: (0, dr, 0, pos(n), 0, 0)))
        args.append(dl)
    return pl.pallas_call(
        functools.partial(_dn_scan_kernel, bsz=bsz),
        out_shape=(jax.ShapeDtypeStruct((bsz, s, DN_W), F32), jax.ShapeDtypeStruct((bsz, s, DN_W), F32)),
        grid=(nc,),
        in_specs=in_specs,
        out_specs=(pl.BlockSpec((bsz, c, DN_W), lambda n: (0, n, 0)),
                   pl.BlockSpec((bsz, c, DN_W), lambda n: (0, bwd_chunk(n), 0))),
        scratch_shapes=[pltpu.VMEM((2 * bsz * DN_HEADS, HEAD_DIM, HEAD_DIM), F32)],
        compiler_params=_params(("arbitrary",)),
        name="dn_scan",
    )(*args)


def _outproj_kernel(x_ref, mod_ref, attn_ctx_ref, attn_lat_ref, of_ref, ob_ref, gate_ref, dnw_ref, wo_ref,
                    n2w_ref, x_out, h2_out):
    d = x_ref.shape[-1]
    o = of_ref[0] + ob_ref[0]
    gate = gate_ref[0]
    parts = [jnp.where(pl.program_id(1) == 0, attn_ctx_ref[0], attn_lat_ref[0])]
    for hd in range(DN_HEADS):
        sl = slice(hd * HEAD_DIM, (hd + 1) * HEAD_DIM)
        g = gate[:, sl]
        parts.append((_head_rmsnorm(o[:, sl], dnw_ref[...]) * (g * jax.nn.sigmoid(g))).astype(BF16))
    mix = jnp.concatenate(parts, axis=1)
    y = jnp.dot(mix, wo_ref[...], preferred_element_type=F32)
    x = x_ref[0] + mod_ref[0, 0, :, 2 * d:3 * d] * y
    x_out[0] = x
    h2_out[0] = _modulated_norm(x, n2w_ref[...], mod_ref[0, 0, :, 3 * d:4 * d],
                                mod_ref[0, 0, :, 4 * d:5 * d]).astype(BF16)


def out_projection(xs, mod, attn_ctx, attn_lat, o_f, o_b, gate, dn_norm_w, w_out, norm2_w):
    bsz, s, d = xs.shape
    tm = TOKEN_BLOCK
    assert attn_ctx.shape[1] == tm
    tok = lambda b, j: (b, j, 0)
    full2 = lambda b, j: (0, 0)
    return pl.pallas_call(
        _outproj_kernel,
        out_shape=(jax.ShapeDtypeStruct((bsz, s, d), F32), jax.ShapeDtypeStruct((bsz, s, d), BF16)),
        grid=(bsz, s // tm),
        in_specs=[pl.BlockSpec((1, tm, d), tok),
                  pl.BlockSpec((1, 1, 1, mod.shape[-1]), _mod_index),
                  pl.BlockSpec((1, tm, ATTN_Q_W), lambda b, j: (b, 0, 0)),
                  pl.BlockSpec((1, tm, ATTN_Q_W), lambda b, j: (b, jnp.maximum(j - 1, 0), 0)),
                  pl.BlockSpec((1, tm, DN_W), tok),
                  pl.BlockSpec((1, tm, DN_W), tok),
                  pl.BlockSpec((1, tm, DN_W), tok),
                  pl.BlockSpec((1, HEAD_DIM), full2),
                  pl.BlockSpec(w_out.shape, full2),
                  pl.BlockSpec((1, d), full2)],
        out_specs=(pl.BlockSpec((1, tm, d), tok), pl.BlockSpec((1, tm, d), tok)),
        compiler_params=_params(("parallel", "parallel")),
        name="out_projection",
    )(xs, mod, attn_ctx, attn_lat, o_f, o_b, gate, dn_norm_w.reshape(1, HEAD_DIM), w_out,
      norm2_w.reshape(1, d))


def _peer_score_kernel(h2_ref, wq_ref, sk_ref, st_out):
    q = jnp.dot(h2_ref[0], wq_ref[...], preferred_element_type=F32).astype(BF16)
    for hp in range(2 * PEER_HEADS):
        st_out[0, hp] = lax.dot_general(sk_ref[hp], q[:, hp * PEER_HALF:(hp + 1) * PEER_HALF],
                                        (((1,), (1,)), ((), ())), preferred_element_type=F32)


def peer_scores(h2, wq, subkeys):
    bsz, s, d = h2.shape
    tm = TOKEN_BLOCK
    nhp = 2 * PEER_HEADS
    return pl.pallas_call(
        _peer_score_kernel,
        out_shape=jax.ShapeDtypeStruct((bsz, nhp, N_KEYS, s), F32),
        grid=(bsz, s // tm),
        in_specs=[pl.BlockSpec((1, tm, d), lambda b, j: (b, j, 0)),
                  pl.BlockSpec(wq.shape, lambda b, j: (0, 0)),
                  pl.BlockSpec(subkeys.shape, lambda b, j: (0, 0, 0))],
        out_specs=pl.BlockSpec((1, nhp, N_KEYS, tm), lambda b, j: (b, 0, 0, j)),
        compiler_params=_params(("parallel", "parallel")),
        name="peer_scores",
    )(h2, wq, subkeys)


def _sorting_network(n):
    pairs = []
    p = 1
    while p < n:
        k = p
        while k >= 1:
            for j in range(k % p, n - k, 2 * k):
                for i in range(min(k, n - j - k)):
                    if (i + j) // (2 * p) == (i + j + k) // (2 * p):
                        pairs.append((i + j, i + j + k))
            k //= 2
        p *= 2
    return pairs


def _sorted_top(s, k, with_rank):
    n_tiles = s.shape[0] // SUBLANES
    v = [s[i * SUBLANES:(i + 1) * SUBLANES] for i in range(n_tiles)]
    for lo, hi in _sorting_network(n_tiles):
        v[lo], v[hi] = jnp.maximum(v[lo], v[hi]), jnp.minimum(v[lo], v[hi])
    out = []
    for r in range(k):
        m = jnp.max(v[0], axis=0, keepdims=True)
        out.append(m)
        hit = v[0] == m
        for i in range(min(n_tiles, k - 1 - r)):
            v[i] = jnp.where(hit, v[i + 1] if i + 1 < n_tiles else NEG_BIG, v[i])
    rank = None
    if with_rank:
        rank = jnp.full(s.shape, float(k), F32)
        for r in reversed(range(k)):
            rank = jnp.where(s >= out[r], float(r), rank)
    return out, rank


def _peer_topk_kernel(st_ref, cnt_out, e0_out, rank_out, e1_out):
    nt = PEER_TOPK + 1
    tops = ([], [])
    for hd in range(PEER_HEADS):
        a, _ = _sorted_top(st_ref[0, 2 * hd], nt, False)
        b, rank = _sorted_top(st_ref[0, 2 * hd + 1], nt, True)
        rank_out[0, hd] = rank.astype(BF16)
        tops[0].append(a)
        tops[1].append(b)
    a8 = [jnp.concatenate([tops[0][hd][r] for hd in range(PEER_HEADS)], axis=0) for r in range(nt)]
    b8 = [jnp.concatenate([tops[1][hd][r] for hd in range(PEER_HEADS)], axis=0) for r in range(nt)]
    cand = [a8[i] + b8[j] for i in range(nt) for j in range(nt) if (i + 1) * (j + 1) <= nt]
    top = []
    for _ in range(nt):
        m = functools.reduce(jnp.maximum, cand)
        top.append(m)
        cand = [jnp.where(t == m, NEG_BIG, t) for t in cand]
    tau8 = 0.5 * (top[PEER_TOPK - 1] + top[PEER_TOPK])
    smax = a8[0] + b8[0]
    z8 = jnp.exp(top[0] - smax)
    for t in top[1:PEER_TOPK]:
        z8 = z8 + jnp.exp(t - smax)
    rz8 = 1.0 / z8
    for hd in range(PEER_HEADS):
        s0 = st_ref[0, 2 * hd]
        s1 = st_ref[0, 2 * hd + 1]
        thr = tau8[hd:hd + 1] - s0
        cnt = jnp.zeros(s0.shape, F32)
        for r in range(nt):
            cnt = jnp.where(tops[1][hd][r] > thr, float(r + 1), cnt)
        cnt_out[0, hd] = cnt
        e0_out[0, hd] = jnp.exp(s0 - tops[0][hd][0])
        e1_out[0, hd] = (jnp.exp(s1 - tops[1][hd][0]) * (rz8[hd:hd + 1] * GELU_GATE_SCALE)).astype(BF16)


def peer_topk(st):
    bsz, nhp, nk, s = st.shape
    tl = LANES
    spec = pl.BlockSpec((1, PEER_HEADS, nk, tl), lambda b, j: (b, 0, 0, j))
    words = jax.ShapeDtypeStruct((bsz, PEER_HEADS, nk, s), F32)
    halfs = jax.ShapeDtypeStruct((bsz, PEER_HEADS, nk, s), BF16)
    return pl.pallas_call(
        _peer_topk_kernel,
        out_shape=(words, words, halfs, halfs),
        grid=(bsz, s // tl),
        in_specs=[pl.BlockSpec((1, nhp, nk, tl), lambda b, j: (b, 0, 0, j))],
        out_specs=(spec, spec, spec, spec),
        compiler_params=_params(("parallel", "parallel")),
        name="peer_topk",
    )(st)


def _peer_expert_kernel(x_ref, mod_ref, h2_ref, u_ref, vt_ref, cnt_ref, e0_ref, rank_ref, e1_ref,
                        x_out, acc_sc, *, ctx_len):
    tok_block = pl.program_id(1)
    ec = pl.program_id(2)
    n_ec = pl.num_programs(2)
    d = x_ref.shape[-1]
    tb = x_ref.shape[1]
    pk = 2 * SUBLANES

    @pl.when(ec == 0)
    def _():
        acc_sc[...] = jnp.zeros(acc_sc.shape, F32)

    def row_tile(ref, hd, ii):
        return jnp.broadcast_to(ref[0, hd, ii:ii + 1, :], (pk, tb)).astype(BF16)

    h2 = h2_ref[0]
    pair = 2 * N_KEYS
    n_pairs = PEER_I_PER_STEP // 2

    def activations(p):
        return lax.dot_general(u_ref[p * pair:(p + 1) * pair, :], h2, (((1,), (1,)), ((), ())),
                               preferred_element_type=F32)

    def gate_weights(p):
        tiles = []
        for ii in (2 * p, 2 * p + 1):
            wt = [jnp.zeros((pk, tb), BF16) for _ in range(N_KEYS // pk)]
            for hd in range(PEER_HEADS):
                cnt = row_tile(cnt_ref, hd, ii)
                e0 = row_tile(e0_ref, hd, ii)
                for rt in range(N_KEYS // pk):
                    rows = slice(rt * pk, (rt + 1) * pk)
                    sel = jnp.where(rank_ref[0, hd, rows, :] < cnt, e1_ref[0, hd, rows, :],
                                    jnp.zeros((), BF16))
                    wt[rt] = wt[rt] + sel * e0
            tiles += wt
        return tiles

    def gated(at2, tiles):
        g = []
        for rt, wt in enumerate(tiles):
            a = at2[rt * pk:(rt + 1) * pk]
            act = a * (1.0 + lax.erf(a))
            g.append(act.astype(BF16) * wt)
        return jnp.concatenate(g, axis=0)

    g = [gated(activations(p), gate_weights(p)) for p in range(n_pairs)]
    acc_sc[...] += jnp.dot(vt_ref[...], jnp.concatenate(g, axis=0), preferred_element_type=F32)

    @pl.when(ec == n_ec - 1)
    def _():
        tok = tok_block * tb + lax.broadcasted_iota(jnp.int32, (tb, 1), 0)
        g2 = jnp.where(tok < ctx_len, mod_ref[0, 0, :, 5 * d:6 * d], mod_ref[0, 1, :, 5 * d:6 * d])
        x_out[0] = x_ref[0] + g2 * acc_sc[...].T


def peer_experts(xs, mod, h2, u_tab, vt_tab, cnt, e0, rank, e1, ctx_len):
    bsz, s, d = xs.shape
    tb = PEER_TOKEN_BLOCK if s % PEER_TOKEN_BLOCK == 0 else TOKEN_BLOCK
    ech = PEER_I_PER_STEP * N_KEYS
    n_exp = u_tab.shape[0]
    tok = lambda b, j, e: (b, j, 0)
    per_tok = lambda b, j, e: (b, 0, 0, j)
    per_i = lambda b, j, e: (b, 0, e, j)
    once = dict(pipeline_mode=pl.Buffered(1))
    return pl.pallas_call(
        functools.partial(_peer_expert_kernel, ctx_len=ctx_len),
        out_shape=jax.ShapeDtypeStruct((bsz, s, d), F32),
        grid=(bsz, s // tb, n_exp // ech),
        in_specs=[pl.BlockSpec((1, tb, d), tok, **once),
                  pl.BlockSpec((1, 2, 1, mod.shape[-1]), lambda b, j, e: (b, 0, 0, 0)),
                  pl.BlockSpec((1, tb, d), tok, **once),
                  pl.BlockSpec((ech, d), lambda b, j, e: (e, 0)),
                  pl.BlockSpec((d, ech), lambda b, j, e: (0, e)),
                  pl.BlockSpec((1, PEER_HEADS, PEER_I_PER_STEP, tb), per_i),
                  pl.BlockSpec((1, PEER_HEADS, PEER_I_PER_STEP, tb), per_i),
                  pl.BlockSpec((1, PEER_HEADS, N_KEYS, tb), per_tok, **once),
                  pl.BlockSpec((1, PEER_HEADS, N_KEYS, tb), per_tok, **once)],
        out_specs=pl.BlockSpec((1, tb, d), tok, **once),
        scratch_shapes=[pltpu.VMEM((d, tb), F32)],
        compiler_params=_params(("parallel", "parallel", "arbitrary")),
        name="peer_experts",
    )(xs, mod, h2, u_tab, vt_tab, cnt, e0, rank, e1)


def _rope_tables(ctx_len, n_lat):
    rows = n_lat // GRID_W
    row = jnp.repeat(jnp.arange(rows, dtype=F32), GRID_W)
    col = jnp.tile(jnp.arange(GRID_W, dtype=F32), rows)
    axis_dim = HEAD_DIM // 2
    inv_freq = ROPE_THETA ** (-jnp.arange(0, axis_dim, 2, dtype=F32) / axis_dim)
    ang_r = row[:, None] * inv_freq[None, :]
    ang_c = col[:, None] * inv_freq[None, :]
    ang = jnp.concatenate([ang_r, ang_r, ang_c, ang_c], axis=-1)
    cos, sin = jnp.cos(ang), jnp.sin(ang)
    first = (jnp.arange(HEAD_DIM) % (HEAD_DIM // 2)) < (HEAD_DIM // 4)
    sina = jnp.where(first, -sin, 0.0)
    sinb = jnp.where(first, 0.0, sin)
    pad = lambda t, v: jnp.concatenate([jnp.full((ctx_len, HEAD_DIM), v, F32), t], axis=0)
    return pad(cos, 1.0), pad(sina, 0.0), pad(sinb, 0.0)


def kernel(x, c, ctx, c_ctx, ada_w, ada_b, norm1_w, norm2_w, w_in, attn_qnorm_w, attn_knorm_w, dn_conv_w,
           dn_A_log, dn_dt_bias, dn_norm_w, w_out, peer_wq, peer_subkeys, peer_u, peer_v):
    bsz, n_lat, d = x.shape
    ctx_len = ctx.shape[1]
    depth = ada_w.shape[0]
    assert ctx_len == TOKEN_BLOCK and n_lat % TOKEN_BLOCK == 0 and bsz + 1 <= SUBLANES
    assert w_in.shape[-1] == IN_MAIN_W + N_GATE_COLS

    xs = jnp.concatenate([ctx, x], axis=1)
    cos, sina, sinb = _rope_tables(ctx_len, n_lat)
    q_lead_blocks = (-ctx_len % latent_query_block(ctx_len, n_lat)) // TOKEN_BLOCK

    cc = jnp.zeros((SUBLANES, d), F32).at[:bsz].set(c).at[bsz].set(c_ctx)
    mod_all = ada_modulation(cc, ada_w, ada_b)
    mod_ctx = jnp.broadcast_to(mod_all[:, bsz][:, None], (depth, bsz, 6 * d))
    mod = jnp.stack([mod_ctx, mod_all[:, :bsz]], axis=2)[:, :, :, None, :]

    for l in range(depth):
        w_main = w_in[l, :, :IN_MAIN_W].astype(BF16)
        w_gate = jnp.pad(w_in[l, :, IN_MAIN_W:], ((0, 0), (0, LANES - N_GATE_COLS))).astype(BF16)
        q, k, v, dnqkv, gate, ba = in_projection(xs, mod[l], norm1_w[l], w_main, w_gate, cos, sina, sinb,
                                                 attn_qnorm_w[l], attn_knorm_w[l], q_lead_blocks)
        attn_ctx, attn_lat = attention(q, k, v, ctx_len)
        conv_w = jnp.pad(dn_conv_w[l], ((0, SUBLANES - CONV_K), (0, 0)))
        gate_par = jnp.zeros((SUBLANES, LANES), F32)
        gate_par = gate_par.at[0, 2 * DN_HEADS:4 * DN_HEADS].set(dn_A_log[l].reshape(-1))
        gate_par = gate_par.at[1, 2 * DN_HEADS:4 * DN_HEADS].set(dn_dt_bias[l].reshape(-1))
        o_f, o_b = dn_scan(*dn_prepare(dnqkv, ba, conv_w, gate_par, ctx_len), ctx_len)
        xs, h2 = out_projection(xs, mod[l], attn_ctx, attn_lat, o_f, o_b, gate, dn_norm_w[l],
                                w_out[l].astype(BF16), norm2_w[l])
        sk = peer_subkeys[l].reshape(2 * PEER_HEADS, N_KEYS, PEER_HALF).astype(BF16)
        st = peer_scores(h2, peer_wq[l].astype(BF16), sk)
        cnt, e0, rank, e1 = peer_topk(st)
        xs = peer_experts(xs, mod[l], h2, (peer_u[l] * GELU_GATE_SCALE).astype(BF16), peer_v[l].T.astype(BF16), cnt, e0, rank, e1,
                          ctx_len)
    return xs[:, ctx_len:]
```

```python
import functools
import math

import jax
import jax.numpy as jnp
from jax import lax
from jax.experimental import pallas as pl
from jax.experimental.pallas import tpu as pltpu

F32 = jnp.float32
BF16 = jnp.bfloat16
HIGHEST = lax.Precision.HIGHEST

HEAD_DIM = 128
ATTN_HEADS = 4
ATTN_KV_HEADS = 2
ATTN_GROUP = ATTN_HEADS // ATTN_KV_HEADS
DN_HEADS = 4
ATTN_Q_W = ATTN_HEADS * HEAD_DIM
ATTN_KV_W = ATTN_KV_HEADS * HEAD_DIM
DN_W = DN_HEADS * HEAD_DIM
IN_MAIN_W = ATTN_Q_W + 2 * ATTN_KV_W + 3 * DN_W + DN_W
N_GATE_COLS = 4 * DN_HEADS
ROPE_THETA = 10000.0
GRID_W = 64
CONV_K = 5
CONV_PAD = CONV_K // 2
PEER_HEADS = 8
PEER_HALF = 128
N_KEYS = 128
PEER_TOPK = 16
EPS = 1e-6
NEG_BIG = -1e30
GELU_GATE_SCALE = 2.0 ** -0.5

LANES = 128
SUBLANES = 8
TOKEN_BLOCK = 256
ADA_COL_BLOCK = 1536
DN_CHUNK = 128
ATTN_KEY_BLOCK = 3328
ATTN_Q_BLOCK = 512
ATTN_CHAIN_ROWS = 256
PEER_TOKEN_BLOCK = 1280
PEER_I_PER_STEP = 8
VMEM_LIMIT = 56 * 1024 * 1024


def _params(sem):
    return pltpu.CompilerParams(dimension_semantics=sem, vmem_limit_bytes=VMEM_LIMIT)


def _bf16_dot(a, b):
    return jnp.dot(a.astype(BF16), b.astype(BF16), preferred_element_type=F32)


def _bf16_dot_nt(a, b):
    return lax.dot_general(a.astype(BF16), b.astype(BF16), (((1,), (1,)), ((), ())),
                           preferred_element_type=F32)


def _ada_kernel(c_ref, w_ref, b_ref, o_ref):
    c = c_ref[...]
    a = c * jax.nn.sigmoid(c)
    o_ref[0] = jnp.dot(a, w_ref[0], preferred_element_type=F32, precision=HIGHEST) + b_ref[0]


def ada_modulation(cc, ada_w, ada_b):
    depth, d, n = ada_w.shape
    tn = ADA_COL_BLOCK
    return pl.pallas_call(
        _ada_kernel,
        out_shape=jax.ShapeDtypeStruct((depth, SUBLANES, n), F32),
        grid=(depth, n // tn),
        in_specs=[pl.BlockSpec((SUBLANES, d), lambda l, j: (0, 0)),
                  pl.BlockSpec((1, d, tn), lambda l, j: (l, 0, j)),
                  pl.BlockSpec((1, 1, tn), lambda l, j: (l, 0, j))],
        out_specs=pl.BlockSpec((1, SUBLANES, tn), lambda l, j: (l, 0, j)),
        compiler_params=_params(("parallel", "parallel")),
        name="ada_modulation",
    )(cc, ada_w, ada_b.reshape(depth, 1, n))


def _mod_index(b, j):
    return (b, jnp.minimum(j, 1), 0, 0)


def _modulated_norm(x, nw, shift, scale):
    ms = jnp.mean(x * x, axis=-1, keepdims=True)
    y = x * lax.rsqrt(ms + EPS) * nw
    return y * (1.0 + scale) + shift


def _head_rmsnorm(x, w):
    return x * lax.rsqrt(jnp.mean(x * x, axis=-1, keepdims=True) + EPS) * w


def _inproj_kernel(x_ref, mod_ref, nw_ref, wm_ref, wg_ref, cos_ref, sina_ref, sinb_ref, qnw_ref, knw_ref,
                   q_out, k_out, v_out, dn_out, gate_out, ba_out):
    d = x_ref.shape[-1]
    x = x_ref[0]
    h = _modulated_norm(x, nw_ref[...], mod_ref[0, 0, :, 0:d], mod_ref[0, 0, :, d:2 * d]).astype(BF16)
    p = jnp.dot(h, wm_ref[...], preferred_element_type=F32)
    ba_out[0] = jnp.dot(h, wg_ref[...], preferred_element_type=F32)
    cos = cos_ref[...]
    sina = sina_ref[...]
    sinb = sinb_ref[...]

    def rope(t):
        return (t * cos + pltpu.roll(t, HEAD_DIM - HEAD_DIM // 4, 1) * sina
                + pltpu.roll(t, HEAD_DIM // 4, 1) * sinb)

    scale = HEAD_DIM ** -0.5 * math.log2(math.e)
    for hd in range(ATTN_HEADS):
        qh = _head_rmsnorm(p[:, hd * HEAD_DIM:(hd + 1) * HEAD_DIM], qnw_ref[...])
        q_out[0, hd] = (rope(qh) * scale).astype(BF16)
    off = ATTN_Q_W
    for hd in range(ATTN_KV_HEADS):
        kh = _head_rmsnorm(p[:, off + hd * HEAD_DIM:off + (hd + 1) * HEAD_DIM], knw_ref[...])
        k_out[0, hd] = rope(kh).astype(BF16)
    off += ATTN_KV_W
    for hd in range(ATTN_KV_HEADS):
        v_out[0, hd, :, 0:HEAD_DIM] = p[:, off + hd * HEAD_DIM:off + (hd + 1) * HEAD_DIM].astype(BF16)
        v_out[0, hd, :, HEAD_DIM:2 * HEAD_DIM] = jnp.ones((x.shape[0], HEAD_DIM), BF16)
    off += ATTN_KV_W
    dn_out[0] = p[:, off:off + 3 * DN_W]
    off += 3 * DN_W
    gate_out[0] = p[:, off:off + DN_W]


def in_projection(xs, mod, norm_w, w_main, w_gate, cos, sina, sinb, qn_w, kn_w, q_lead_blocks):
    bsz, s, d = xs.shape
    tm = TOKEN_BLOCK
    blk = lambda j: jnp.maximum(j - q_lead_blocks, 0)
    tok = lambda b, j: (b, blk(j), 0)
    head_tok = lambda b, j: (b, 0, blk(j), 0)
    full2 = lambda b, j: (0, 0)
    rope_spec = pl.BlockSpec((tm, HEAD_DIM), lambda b, j: (blk(j), 0))
    return pl.pallas_call(
        _inproj_kernel,
        out_shape=(jax.ShapeDtypeStruct((bsz, ATTN_HEADS, q_lead_blocks * tm + s, HEAD_DIM), BF16),
                   jax.ShapeDtypeStruct((bsz, ATTN_KV_HEADS, s, HEAD_DIM), BF16),
                   jax.ShapeDtypeStruct((bsz, ATTN_KV_HEADS, s, 2 * HEAD_DIM), BF16),
                   jax.ShapeDtypeStruct((bsz, s, 3 * DN_W), F32),
                   jax.ShapeDtypeStruct((bsz, s, DN_W), F32),
                   jax.ShapeDtypeStruct((bsz, s, LANES), F32)),
        grid=(bsz, q_lead_blocks + s // tm),
        in_specs=[pl.BlockSpec((1, tm, d), tok),
                  pl.BlockSpec((1, 1, 1, mod.shape[-1]), lambda b, j: _mod_index(b, blk(j))),
                  pl.BlockSpec((1, d), full2),
                  pl.BlockSpec((d, IN_MAIN_W), full2),
                  pl.BlockSpec((d, LANES), full2),
                  rope_spec, rope_spec, rope_spec,
                  pl.BlockSpec((1, HEAD_DIM), full2),
                  pl.BlockSpec((1, HEAD_DIM), full2)],
        out_specs=(pl.BlockSpec((1, ATTN_HEADS, tm, HEAD_DIM), lambda b, j: (b, 0, j, 0)),
                   pl.BlockSpec((1, ATTN_KV_HEADS, tm, HEAD_DIM), head_tok),
                   pl.BlockSpec((1, ATTN_KV_HEADS, tm, 2 * HEAD_DIM), head_tok),
                   pl.BlockSpec((1, tm, 3 * DN_W), tok),
                   pl.BlockSpec((1, tm, DN_W), tok),
                   pl.BlockSpec((1, tm, LANES), tok)),
        compiler_params=_params(("parallel", "arbitrary")),
        name="in_projection",
    )(xs, mod, norm_w.reshape(1, d), w_main, w_gate, cos, sina, sinb,
      qn_w.reshape(1, HEAD_DIM), kn_w.reshape(1, HEAD_DIM))


def _attn_kernel(q_ref, k_ref, v_ref, o_ref, m_sc, acc_sc):
    ki = pl.program_id(2)
    nk = pl.num_programs(2)

    @pl.when(ki == 0)
    def _():
        m_sc[...] = jnp.full(m_sc.shape, NEG_BIG, F32)
        acc_sc[...] = jnp.zeros(acc_sc.shape, F32)

    tq = q_ref.shape[2]
    chains = [(hd, pl.ds(r0, ATTN_CHAIN_ROWS)) for hd in range(ATTN_HEADS)
              for r0 in range(0, tq, ATTN_CHAIN_ROWS)]
    scores = [lax.dot_general(q_ref[0, hd, rows, :], k_ref[0, hd // ATTN_GROUP], (((1,), (1,)), ((), ())),
                              preferred_element_type=F32) for hd, rows in chains]
    for (hd, rows), s in zip(chains, scores):
        m_prev = m_sc[hd, rows, :]
        m_new = jnp.maximum(m_prev, jnp.max(s, axis=-1, keepdims=True))
        p = jnp.exp2(s - m_new)
        alpha = jnp.exp2(m_prev - m_new)
        acc_sc[hd, rows, :] = alpha * acc_sc[hd, rows, :] + jnp.dot(
            p.astype(BF16), v_ref[0, hd // ATTN_GROUP], preferred_element_type=F32)
        m_sc[hd, rows, :] = m_new

    @pl.when(ki == nk - 1)
    def _():
        for hd in range(ATTN_HEADS):
            o_ref[0, :, hd * HEAD_DIM:(hd + 1) * HEAD_DIM] = (
                acc_sc[hd, :, 0:HEAD_DIM] / acc_sc[hd, :, HEAD_DIM:2 * HEAD_DIM]).astype(o_ref.dtype)


def _attention_call(q, k, v, q_row0, n_q, tq, n_keys, tk, name):
    bsz = q.shape[0]
    n_q_blocks = n_q // tq
    q_block0 = q_row0 // tq
    assert q_block0 * tq == q_row0
    return pl.pallas_call(
        _attn_kernel,
        out_shape=jax.ShapeDtypeStruct((bsz, n_q, ATTN_Q_W), BF16),
        grid=(bsz, n_q_blocks, n_keys // tk),
        in_specs=[pl.BlockSpec((1, ATTN_HEADS, tq, HEAD_DIM), lambda b, i, j: (b, 0, i + q_block0, 0)),
                  pl.BlockSpec((1, ATTN_KV_HEADS, tk, HEAD_DIM), lambda b, i, j: (b, 0, j, 0)),
                  pl.BlockSpec((1, ATTN_KV_HEADS, tk, 2 * HEAD_DIM), lambda b, i, j: (b, 0, j, 0))],
        out_specs=pl.BlockSpec((1, tq, ATTN_Q_W), lambda b, i, j: (b, i, 0)),
        scratch_shapes=[pltpu.VMEM((ATTN_HEADS, tq, 1), F32),
                        pltpu.VMEM((ATTN_HEADS, tq, 2 * HEAD_DIM), F32)],
        compiler_params=_params(("parallel", "parallel", "arbitrary")),
        name=name,
    )(q, k, v)


def latent_query_block(ctx_len, n_lat):
    return ATTN_Q_BLOCK if n_lat % ATTN_Q_BLOCK == 0 and ATTN_Q_BLOCK % ctx_len == 0 else ATTN_CHAIN_ROWS


def attention(q, k, v, ctx_len):
    s = k.shape[2]
    n_lat = s - ctx_len
    lead = q.shape[2] - s
    assert ctx_len % ATTN_CHAIN_ROWS == 0 and n_lat % ATTN_CHAIN_ROWS == 0
    tk = ATTN_KEY_BLOCK if s % ATTN_KEY_BLOCK == 0 else ATTN_CHAIN_ROWS
    tq = latent_query_block(ctx_len, n_lat)
    attn_ctx = _attention_call(q, k, v, lead, ctx_len, ctx_len, ctx_len, ctx_len, "attention_ctx")
    attn_lat = _attention_call(q, k, v, lead + ctx_len, n_lat, tq, s, tk, "attention")
    return attn_ctx, attn_lat


def _dn_prep_kernel(main_ref, prev_ref, next_ref, ba_ref, cw_ref, gp_ref,
                    w_out, u_out, qg_out, kdt_out, qk_out, dl_out, ext_sc, *, ctx_chunks):
    j = pl.program_id(1)
    nj = pl.num_programs(1)
    c = DN_CHUNK
    has_prev = (j != 0) & (j != ctx_chunks)
    has_next = (j != ctx_chunks - 1) & (j != nj - 1)
    ext_sc[0:SUBLANES] = jnp.where(has_prev, prev_ref[0], 0.0)
    ext_sc[SUBLANES:SUBLANES + c] = main_ref[0]
    ext_sc[SUBLANES + c:2 * SUBLANES + c] = jnp.where(has_next, next_ref[0], 0.0)
    y = ext_sc[SUBLANES - CONV_PAD:SUBLANES - CONV_PAD + c] * cw_ref[0:1]
    for t in range(1, CONV_K):
        y = y + ext_sc[SUBLANES - CONV_PAD + t:SUBLANES - CONV_PAD + t + c] * cw_ref[t:t + 1]
    y = y * jax.nn.sigmoid(y)

    ba = ba_ref[0]
    beta_all = jax.nn.sigmoid(ba)
    g_all = -jnp.exp(gp_ref[0:1]) * jax.nn.softplus(ba + gp_ref[1:2])
    row = lax.broadcasted_iota(jnp.int32, (c, c), 0)
    col = lax.broadcasted_iota(jnp.int32, (c, c), 1)
    lower = (row >= col).astype(BF16)
    g_hi = g_all.astype(BF16)
    g_r1 = g_all - g_hi.astype(F32)
    g_mid = g_r1.astype(BF16)
    g_lo = (g_r1 - g_mid.astype(F32)).astype(BF16)
    dotf = functools.partial(jnp.dot, preferred_element_type=F32)
    prefix = dotf(lower, g_hi) + (dotf(lower, g_mid) + dotf(lower, g_lo))
    total = prefix[c - 1:c]
    gc = (prefix, total - prefix + g_all)
    gct = (gc[0].T, gc[1].T)
    row2 = lax.broadcasted_iota(jnp.int32, (2 * c, 2 * c), 0)
    col2 = lax.broadcasted_iota(jnp.int32, (2 * c, 2 * c), 1)
    eye2 = (row2 == col2).astype(F32)
    zero = jnp.zeros((c, c), F32)

    heads = range(DN_HEADS)
    a2, rhs2 = [], []
    for hd in heads:
        q = y[:, hd * HEAD_DIM:(hd + 1) * HEAD_DIM]
        k = y[:, DN_W + hd * HEAD_DIM:DN_W + (hd + 1) * HEAD_DIM]
        v = y[:, 2 * DN_W + hd * HEAD_DIM:2 * DN_W + (hd + 1) * HEAD_DIM]
        q = q * lax.rsqrt(jnp.sum(q * q, axis=-1, keepdims=True) + EPS) * (HEAD_DIM ** -0.5)
        k = k * lax.rsqrt(jnp.sum(k * k, axis=-1, keepdims=True) + EPS)
        kk = _bf16_dot_nt(k, k)
        qk = _bf16_dot_nt(q, k)
        a_dir, rhs_dir = [], []
        for dr in range(2):
            cb = dr * DN_HEADS + hd
            cg = 2 * DN_HEADS + cb
            beta = beta_all[:, cb:cb + 1]
            gcol = gc[dr][:, cg:cg + 1]
            grow = gct[dr][cg:cg + 1, :]
            incl = (row >= col) if dr == 0 else (row <= col)
            strict = (row > col) if dr == 0 else (row < col)
            decay = jnp.exp(jnp.where(incl, gcol - grow, NEG_BIG))
            a_dir.append(jnp.where(strict, kk * beta * decay, 0.0))
            eg = jnp.exp(gcol)
            rhs_dir.append(jnp.concatenate([v * beta, k * (beta * eg)], axis=1))
            qg_out[0, dr, hd] = (q * eg).astype(BF16)
            tot = total[:, cg:cg + 1]
            kdt_out[0, dr, hd] = (k * jnp.exp(tot - gcol)).T.astype(BF16)
            qk_out[0, dr, hd] = (qk * decay).astype(BF16)
            dl_out[0, dr, hd, 0] = jnp.broadcast_to(jnp.exp(tot), (1, LANES))
        a2.append(jnp.concatenate([jnp.concatenate([a_dir[0], zero], axis=1),
                                   jnp.concatenate([zero, a_dir[1]], axis=1)], axis=0))
        rhs2.append(jnp.concatenate(rhs_dir, axis=0))
    blk = 2
    base_mask = (row2 // blk) == (col2 // blk)
    x = [eye2 - jnp.where(base_mask, a2[hd], 0.0) for hd in heads]
    while blk < c:
        sibling = ((row2 // (2 * blk)) == (col2 // (2 * blk))) & ((row2 // blk) != (col2 // blk))
        fold = [_bf16_dot(x[hd], jnp.where(sibling, a2[hd], 0.0)) for hd in heads]
        x = [x[hd] - _bf16_dot(fold[hd], x[hd]) for hd in heads]
        blk *= 2
    for hd in heads:
        uw = _bf16_dot(x[hd], rhs2[hd])
        for dr in range(2):
            u_out[0, dr, hd] = uw[dr * c:(dr + 1) * c, 0:HEAD_DIM]
            w_out[0, dr, hd] = uw[dr * c:(dr + 1) * c, HEAD_DIM:2 * HEAD_DIM].astype(BF16)


def dn_prepare(dnqkv, ba, conv_w, gate_par, ctx_len):
    bsz, s, wdt = dnqkv.shape
    c = DN_CHUNK
    nc = s // c
    rows8 = s // SUBLANES
    per = c // SUBLANES
    chain = lambda b, j: (b, 0, 0, j, 0)
    return pl.pallas_call(
        functools.partial(_dn_prep_kernel, ctx_chunks=ctx_len // c),
        out_shape=(jax.ShapeDtypeStruct((bsz, 2, DN_HEADS, s, HEAD_DIM), BF16),
                   jax.ShapeDtypeStruct((bsz, 2, DN_HEADS, s, HEAD_DIM), F32),
                   jax.ShapeDtypeStruct((bsz, 2, DN_HEADS, s, HEAD_DIM), BF16),
                   jax.ShapeDtypeStruct((bsz, 2, DN_HEADS, HEAD_DIM, s), BF16),
                   jax.ShapeDtypeStruct((bsz, 2, DN_HEADS, s, c), BF16),
                   jax.ShapeDtypeStruct((bsz, 2, DN_HEADS, nc, 1, LANES), F32)),
        grid=(bsz, nc),
        in_specs=[pl.BlockSpec((1, c, wdt), lambda b, j: (b, j, 0)),
                  pl.BlockSpec((1, SUBLANES, wdt), lambda b, j: (b, jnp.maximum(j * per - 1, 0), 0)),
                  pl.BlockSpec((1, SUBLANES, wdt), lambda b, j: (b, jnp.minimum((j + 1) * per, rows8 - 1), 0)),
                  pl.BlockSpec((1, c, LANES), lambda b, j: (b, j, 0)),
                  pl.BlockSpec((SUBLANES, wdt), lambda b, j: (0, 0)),
                  pl.BlockSpec((SUBLANES, LANES), lambda b, j: (0, 0))],
        out_specs=(pl.BlockSpec((1, 2, DN_HEADS, c, HEAD_DIM), chain),
                   pl.BlockSpec((1, 2, DN_HEADS, c, HEAD_DIM), chain),
                   pl.BlockSpec((1, 2, DN_HEADS, c, HEAD_DIM), chain),
                   pl.BlockSpec((1, 2, DN_HEADS, HEAD_DIM, c), lambda b, j: (b, 0, 0, 0, j)),
                   pl.BlockSpec((1, 2, DN_HEADS, c, c), chain),
                   pl.BlockSpec((1, 2, DN_HEADS, 1, 1, LANES), lambda b, j: (b, 0, 0, j, 0, 0))),
        scratch_shapes=[pltpu.VMEM((c + 2 * SUBLANES, wdt), F32)],
        compiler_params=_params(("parallel", "parallel")),
        name="dn_prepare",
    )(dnqkv, dnqkv, dnqkv, ba, conv_w, gate_par)


def _dn_scan_kernel(*refs, bsz):
    ins = refs[:12]
    of_ref, ob_ref, s_sc = refs[12:]
    n = pl.program_id(0)

    @pl.when(n == 0)
    def _():
        s_sc[...] = jnp.zeros(s_sc.shape, F32)

    chains = [(dr, b, hd) for dr in range(2) for b in range(bsz) for hd in range(DN_HEADS)]
    dotf = functools.partial(jnp.dot, preferred_element_type=F32)

    def inp(dr, k):
        return ins[dr * 6 + k]

    state = [s_sc[ci] for ci in range(len(chains))]
    r = [dotf(jnp.concatenate([inp(dr, 0)[b, 0, hd], inp(dr, 2)[b, 0, hd]], axis=0), state[ci].astype(BF16))
         for ci, (dr, b, hd) in enumerate(chains)]
    v_new = [(inp(dr, 1)[b, 0, hd] - r[ci][0:DN_CHUNK]).astype(BF16) for ci, (dr, b, hd) in enumerate(chains)]
    intra = [dotf(inp(dr, 4)[b, 0, hd], v_new[ci]) for ci, (dr, b, hd) in enumerate(chains)]
    upd = [dotf(inp(dr, 3)[b, 0, hd], v_new[ci]) for ci, (dr, b, hd) in enumerate(chains)]
    for ci, (dr, b, hd) in enumerate(chains):
        o_ref = of_ref if dr == 0 else ob_ref
        o_ref[b, :, hd * HEAD_DIM:(hd + 1) * HEAD_DIM] = r[ci][DN_CHUNK:] + intra[ci]
        s_sc[ci] = state[ci] * inp(dr, 5)[b, 0, hd, 0] + upd[ci]


def dn_scan(w, u, qg, kdt, qk, dl, ctx_len):
    bsz, _, _, s, _ = w.shape
    c = DN_CHUNK
    nc = s // c
    cc = ctx_len // c

    def bwd_chunk(n):
        return jnp.where(n < cc, cc - 1 - n, nc - 1 - (n - cc))

    in_specs, args = [], []
    for dr in range(2):
        pos = (lambda n: n) if dr == 0 else bwd_chunk
        tokm = lambda n, dr=dr, pos=pos: (0, dr, 0, pos(n), 0)
        for arr in (w, u, qg):
            in_specs.append(pl.BlockSpec((bsz, 1, DN_HEADS, c, HEAD_DIM), tokm))
            args.append(arr)
        in_specs.append(pl.BlockSpec((bsz, 1, DN_HEADS, HEAD_DIM, c), lambda n, dr=dr, pos=pos: (0, dr, 0, 0, pos(n))))
        args.append(kdt)
        in_specs.append(pl.BlockSpec((bsz, 1, DN_HEADS, c, c), tokm))
        args.append(qk)
        in_specs.append(pl.BlockSpec((bsz, 1, DN_HEADS, 1, 1, LANES), lambda n, dr=dr, pos=pos: (0, dr, 0, pos(n), 0, 0)))
        args.append(dl)
    return pl.pallas_call(
        functools.partial(_dn_scan_kernel, bsz=bsz),
        out_shape=(jax.ShapeDtypeStruct((bsz, s, DN_W), F32), jax.ShapeDtypeStruct((bsz, s, DN_W), F32)),
        grid=(nc,),
        in_specs=in_specs,
        out_specs=(pl.BlockSpec((bsz, c, DN_W), lambda n: (0, n, 0)),
                   pl.BlockSpec((bsz, c, DN_W), lambda n: (0, bwd_chunk(n), 0))),
        scratch_shapes=[pltpu.VMEM((2 * bsz * DN_HEADS, HEAD_DIM, HEAD_DIM), F32)],
        compiler_params=_params(("arbitrary",)),
        name="dn_scan",
    )(*args)


def _outproj_kernel(x_ref, mod_ref, attn_ctx_ref, attn_lat_ref, of_ref, ob_ref, gate_ref, dnw_ref, wo_ref,
                    n2w_ref, wq_ref, sk_ref, x_out, h2_out, st_out):
    d = x_ref.shape[-1]
    o = of_ref[0] + ob_ref[0]
    gate = gate_ref[0]
    parts = [jnp.where(pl.program_id(1) == 0, attn_ctx_ref[0], attn_lat_ref[0])]
    for hd in range(DN_HEADS):
        sl = slice(hd * HEAD_DIM, (hd + 1) * HEAD_DIM)
        g = gate[:, sl]
        parts.append((_head_rmsnorm(o[:, sl], dnw_ref[...]) * (g * jax.nn.sigmoid(g))).astype(BF16))
    mix = jnp.concatenate(parts, axis=1)
    y = jnp.dot(mix, wo_ref[...], preferred_element_type=F32)
    x = x_ref[0] + mod_ref[0, 0, :, 2 * d:3 * d] * y
    x_out[0] = x
    h2 = _modulated_norm(x, n2w_ref[...], mod_ref[0, 0, :, 3 * d:4 * d],
                         mod_ref[0, 0, :, 4 * d:5 * d]).astype(BF16)
    h2_out[0] = h2
    q = jnp.dot(h2, wq_ref[...], preferred_element_type=F32).astype(BF16)
    for hp in range(2 * PEER_HEADS):
        st_out[0, hp] = lax.dot_general(sk_ref[hp], q[:, hp * PEER_HALF:(hp + 1) * PEER_HALF],
                                        (((1,), (1,)), ((), ())), preferred_element_type=F32)


def out_projection(xs, mod, attn_ctx, attn_lat, o_f, o_b, gate, dn_norm_w, w_out, norm2_w, wq, subkeys):
    bsz, s, d = xs.shape
    tm = TOKEN_BLOCK
    nhp = 2 * PEER_HEADS
    assert attn_ctx.shape[1] == tm
    tok = lambda b, j: (b, j, 0)
    full2 = lambda b, j: (0, 0)
    return pl.pallas_call(
        _outproj_kernel,
        out_shape=(jax.ShapeDtypeStruct((bsz, s, d), F32), jax.ShapeDtypeStruct((bsz, s, d), BF16),
                   jax.ShapeDtypeStruct((bsz, nhp, N_KEYS, s), F32)),
        grid=(bsz, s // tm),
        in_specs=[pl.BlockSpec((1, tm, d), tok),
                  pl.BlockSpec((1, 1, 1, mod.shape[-1]), _mod_index),
                  pl.BlockSpec((1, tm, ATTN_Q_W), lambda b, j: (b, 0, 0)),
                  pl.BlockSpec((1, tm, ATTN_Q_W), lambda b, j: (b, jnp.maximum(j - 1, 0), 0)),
                  pl.BlockSpec((1, tm, DN_W), tok),
                  pl.BlockSpec((1, tm, DN_W), tok),
                  pl.BlockSpec((1, tm, DN_W), tok),
                  pl.BlockSpec((1, HEAD_DIM), full2),
                  pl.BlockSpec(w_out.shape, full2),
                  pl.BlockSpec((1, d), full2),
                  pl.BlockSpec(wq.shape, full2),
                  pl.BlockSpec(subkeys.shape, lambda b, j: (0, 0, 0))],
        out_specs=(pl.BlockSpec((1, tm, d), tok), pl.BlockSpec((1, tm, d), tok),
                   pl.BlockSpec((1, nhp, N_KEYS, tm), lambda b, j: (b, 0, 0, j))),
        compiler_params=_params(("parallel", "parallel")),
        name="out_projection",
    )(xs, mod, attn_ctx, attn_lat, o_f, o_b, gate, dn_norm_w.reshape(1, HEAD_DIM), w_out,
      norm2_w.reshape(1, d), wq, subkeys)


def _sorting_network(n):
    pairs = []
    p = 1
    while p < n:
        k = p
        while k >= 1:
            for j in range(k % p, n - k, 2 * k):
                for i in range(min(k, n - j - k)):
                    if (i + j) // (2 * p) == (i + j + k) // (2 * p):
                        pairs.append((i + j, i + j + k))
            k //= 2
        p *= 2
    return pairs


def _sorted_top(s, k, with_rank):
    n_tiles = s.shape[0] // SUBLANES
    v = [s[i * SUBLANES:(i + 1) * SUBLANES] for i in range(n_tiles)]
    for lo, hi in _sorting_network(n_tiles):
        v[lo], v[hi] = jnp.maximum(v[lo], v[hi]), jnp.minimum(v[lo], v[hi])
    out = []
    for r in range(k):
        m = jnp.max(v[0], axis=0, keepdims=True)
        out.append(m)
        hit = v[0] == m
        for i in range(min(n_tiles, k - 1 - r)):
            v[i] = jnp.where(hit, v[i + 1] if i + 1 < n_tiles else NEG_BIG, v[i])
    rank = None
    if with_rank:
        rank = jnp.full(s.shape, float(k), F32)
        for r in reversed(range(k)):
            rank = jnp.where(s >= out[r], float(r), rank)
    return out, rank


def _peer_topk_kernel(st_ref, cnt_out, e0_out, rank_out, e1_out):
    nt = PEER_TOPK + 1
    tops = ([], [])
    for hd in range(PEER_HEADS):
        a, _ = _sorted_top(st_ref[0, 2 * hd], nt, False)
        b, rank = _sorted_top(st_ref[0, 2 * hd + 1], nt, True)
        rank_out[0, hd] = rank.astype(BF16)
        tops[0].append(a)
        tops[1].append(b)
    a8 = [jnp.concatenate([tops[0][hd][r] for hd in range(PEER_HEADS)], axis=0) for r in range(nt)]
    b8 = [jnp.concatenate([tops[1][hd][r] for hd in range(PEER_HEADS)], axis=0) for r in range(nt)]
    cand = [a8[i] + b8[j] for i in range(nt) for j in range(nt) if (i + 1) * (j + 1) <= nt]
    top = []
    for _ in range(nt):
        m = functools.reduce(jnp.maximum, cand)
        top.append(m)
        cand = [jnp.where(t == m, NEG_BIG, t) for t in cand]
    tau8 = 0.5 * (top[PEER_TOPK - 1] + top[PEER_TOPK])
    smax = a8[0] + b8[0]
    z8 = jnp.exp(top[0] - smax)
    for t in top[1:PEER_TOPK]:
        z8 = z8 + jnp.exp(t - smax)
    rz8 = 1.0 / z8
    for hd in range(PEER_HEADS):
        s0 = st_ref[0, 2 * hd]
        s1 = st_ref[0, 2 * hd + 1]
        thr = tau8[hd:hd + 1] - s0
        cnt = jnp.zeros(s0.shape, F32)
        for r in range(nt):
            cnt = jnp.where(tops[1][hd][r] > thr, float(r + 1), cnt)
        cnt_out[0, hd] = cnt
        e0_out[0, hd] = jnp.exp(s0 - tops[0][hd][0])
        e1_out[0, hd] = (jnp.exp(s1 - tops[1][hd][0]) * (rz8[hd:hd + 1] * GELU_GATE_SCALE)).astype(BF16)


def peer_topk(st):
    bsz, nhp, nk, s = st.shape
    tl = LANES
    spec = pl.BlockSpec((1, PEER_HEADS, nk, tl), lambda b, j: (b, 0, 0, j))
    words = jax.ShapeDtypeStruct((bsz, PEER_HEADS, nk, s), F32)
    halfs = jax.ShapeDtypeStruct((bsz, PEER_HEADS, nk, s), BF16)
    return pl.pallas_call(
        _peer_topk_kernel,
        out_shape=(words, words, halfs, halfs),
        grid=(bsz, s // tl),
        in_specs=[pl.BlockSpec((1, nhp, nk, tl), lambda b, j: (b, 0, 0, j))],
        out_specs=(spec, spec, spec, spec),
        compiler_params=_params(("parallel", "parallel")),
        name="peer_topk",
    )(st)


def _peer_expert_kernel(x_ref, mod_ref, h2_ref, u_ref, vt_ref, cnt_ref, e0_ref, rank_ref, e1_ref,
                        x_out, acc_sc, *, ctx_len):
    tok_block = pl.program_id(1)
    ec = pl.program_id(2)
    n_ec = pl.num_programs(2)
    d = x_ref.shape[-1]
    tb = x_ref.shape[1]
    pk = 2 * SUBLANES

    @pl.when(ec == 0)
    def _():
        acc_sc[...] = jnp.zeros(acc_sc.shape, F32)

    def row_tile(ref, hd, ii):
        return jnp.broadcast_to(ref[0, hd, ii:ii + 1, :], (pk, tb)).astype(BF16)

    h2 = h2_ref[0]
    pair = 2 * N_KEYS
    n_pairs = PEER_I_PER_STEP // 2

    def activations(p):
        return lax.dot_general(u_ref[p * pair:(p + 1) * pair, :], h2, (((1,), (1,)), ((), ())),
                               preferred_element_type=F32)

    def gate_weights(p):
        tiles = []
        for ii in (2 * p, 2 * p + 1):
            wt = [jnp.zeros((pk, tb), BF16) for _ in range(N_KEYS // pk)]
            for hd in range(PEER_HEADS):
                cnt = row_tile(cnt_ref, hd, ii)
                e0 = row_tile(e0_ref, hd, ii)
                for rt in range(N_KEYS // pk):
                    rows = slice(rt * pk, (rt + 1) * pk)
                    sel = jnp.where(rank_ref[0, hd, rows, :] < cnt, e1_ref[0, hd, rows, :],
                                    jnp.zeros((), BF16))
                    wt[rt] = wt[rt] + sel * e0
            tiles += wt
        return tiles

    def gated(at2, tiles):
        g = []
        for rt, wt in enumerate(tiles):
            a = at2[rt * pk:(rt + 1) * pk]
            act = a * (1.0 + lax.erf(a))
            g.append(act.astype(BF16) * wt)
        return jnp.concatenate(g, axis=0)

    g = [gated(activations(p), gate_weights(p)) for p in range(n_pairs)]
    acc_sc[...] += jnp.dot(vt_ref[...], jnp.concatenate(g, axis=0), preferred_element_type=F32)

    @pl.when(ec == n_ec - 1)
    def _():
        tok = tok_block * tb + lax.broadcasted_iota(jnp.int32, (tb, 1), 0)
        g2 = jnp.where(tok < ctx_len, mod_ref[0, 0, :, 5 * d:6 * d], mod_ref[0, 1, :, 5 * d:6 * d])
        x_out[0] = x_ref[0] + g2 * acc_sc[...].T


def peer_experts(xs, mod, h2, u_tab, vt_tab, cnt, e0, rank, e1, ctx_len):
    bsz, s, d = xs.shape
    tb = PEER_TOKEN_BLOCK if s % PEER_TOKEN_BLOCK == 0 else TOKEN_BLOCK
    ech = PEER_I_PER_STEP * N_KEYS
    n_exp = u_tab.shape[0]
    tok = lambda b, j, e: (b, j, 0)
    per_tok = lambda b, j, e: (b, 0, 0, j)
    per_i = lambda b, j, e: (b, 0, e, j)
    once = dict(pipeline_mode=pl.Buffered(1))
    return pl.pallas_call(
        functools.partial(_peer_expert_kernel, ctx_len=ctx_len),
        out_shape=jax.ShapeDtypeStruct((bsz, s, d), F32),
        grid=(bsz, s // tb, n_exp // ech),
        in_specs=[pl.BlockSpec((1, tb, d), tok, **once),
                  pl.BlockSpec((1, 2, 1, mod.shape[-1]), lambda b, j, e: (b, 0, 0, 0)),
                  pl.BlockSpec((1, tb, d), tok, **once),
                  pl.BlockSpec((ech, d), lambda b, j, e: (e, 0)),
                  pl.BlockSpec((d, ech), lambda b, j, e: (0, e)),
                  pl.BlockSpec((1, PEER_HEADS, PEER_I_PER_STEP, tb), per_i),
                  pl.BlockSpec((1, PEER_HEADS, PEER_I_PER_STEP, tb), per_i),
                  pl.BlockSpec((1, PEER_HEADS, N_KEYS, tb), per_tok, **once),
                  pl.BlockSpec((1, PEER_HEADS, N_KEYS, tb), per_tok, **once)],
        out_specs=pl.BlockSpec((1, tb, d), tok, **once),
        scratch_shapes=[pltpu.VMEM((d, tb), F32)],
        compiler_params=_params(("parallel", "parallel", "arbitrary")),
        name="peer_experts",
    )(xs, mod, h2, u_tab, vt_tab, cnt, e0, rank, e1)


def _rope_tables(ctx_len, n_lat):
    rows = n_lat // GRID_W
    row = jnp.repeat(jnp.arange(rows, dtype=F32), GRID_W)
    col = jnp.tile(jnp.arange(GRID_W, dtype=F32), rows)
    axis_dim = HEAD_DIM // 2
    inv_freq = ROPE_THETA ** (-jnp.arange(0, axis_dim, 2, dtype=F32) / axis_dim)
    ang_r = row[:, None] * inv_freq[None, :]
    ang_c = col[:, None] * inv_freq[None, :]
    ang = jnp.concatenate([ang_r, ang_r, ang_c, ang_c], axis=-1)
    cos, sin = jnp.cos(ang), jnp.sin(ang)
    first = (jnp.arange(HEAD_DIM) % (HEAD_DIM // 2)) < (HEAD_DIM // 4)
    sina = jnp.where(first, -sin, 0.0)
    sinb = jnp.where(first, 0.0, sin)
    pad = lambda t, v: jnp.concatenate([jnp.full((ctx_len, HEAD_DIM), v, F32), t], axis=0)
    return pad(cos, 1.0), pad(sina, 0.0), pad(sinb, 0.0)


def kernel(x, c, ctx, c_ctx, ada_w, ada_b, norm1_w, norm2_w, w_in, attn_qnorm_w, attn_knorm_w, dn_conv_w,
           dn_A_log, dn_dt_bias, dn_norm_w, w_out, peer_wq, peer_subkeys, peer_u, peer_v):
    bsz, n_lat, d = x.shape
    ctx_len = ctx.shape[1]
    depth = ada_w.shape[0]
    assert ctx_len == TOKEN_BLOCK and n_lat % TOKEN_BLOCK == 0 and bsz + 1 <= SUBLANES
    assert w_in.shape[-1] == IN_MAIN_W + N_GATE_COLS

    xs = jnp.concatenate([ctx, x], axis=1)
    cos, sina, sinb = _rope_tables(ctx_len, n_lat)
    q_lead_blocks = (-ctx_len % latent_query_block(ctx_len, n_lat)) // TOKEN_BLOCK

    cc = jnp.zeros((SUBLANES, d), F32).at[:bsz].set(c).at[bsz].set(c_ctx)
    mod_all = ada_modulation(cc, ada_w, ada_b)
    mod_ctx = jnp.broadcast_to(mod_all[:, bsz][:, None], (depth, bsz, 6 * d))
    mod = jnp.stack([mod_ctx, mod_all[:, :bsz]], axis=2)[:, :, :, None, :]

    for l in range(depth):
        w_main = w_in[l, :, :IN_MAIN_W].astype(BF16)
        w_gate = jnp.pad(w_in[l, :, IN_MAIN_W:], ((0, 0), (0, LANES - N_GATE_COLS))).astype(BF16)
        q, k, v, dnqkv, gate, ba = in_projection(xs, mod[l], norm1_w[l], w_main, w_gate, cos, sina, sinb,
                                                 attn_qnorm_w[l], attn_knorm_w[l], q_lead_blocks)
        attn_ctx, attn_lat = attention(q, k, v, ctx_len)
        conv_w = jnp.pad(dn_conv_w[l], ((0, SUBLANES - CONV_K), (0, 0)))
        gate_par = jnp.zeros((SUBLANES, LANES), F32)
        gate_par = gate_par.at[0, 2 * DN_HEADS:4 * DN_HEADS].set(dn_A_log[l].reshape(-1))
        gate_par = gate_par.at[1, 2 * DN_HEADS:4 * DN_HEADS].set(dn_dt_bias[l].reshape(-1))
        o_f, o_b = dn_scan(*dn_prepare(dnqkv, ba, conv_w, gate_par, ctx_len), ctx_len)
        sk = peer_subkeys[l].reshape(2 * PEER_HEADS, N_KEYS, PEER_HALF).astype(BF16)
        xs, h2, st = out_projection(xs, mod[l], attn_ctx, attn_lat, o_f, o_b, gate, dn_norm_w[l],
                                    w_out[l].astype(BF16), norm2_w[l], peer_wq[l].astype(BF16), sk)
        cnt, e0, rank, e1 = peer_topk(st)
        xs = peer_experts(xs, mod[l], h2, (peer_u[l] * GELU_GATE_SCALE).astype(BF16), peer_v[l].T.astype(BF16), cnt, e0, rank, e1,
                          ctx_len)
    return xs[:, ctx_len:]
```

```python
import functools
import math

import jax
import jax.numpy as jnp
from jax import lax
from jax.experimental import pallas as pl
from jax.experimental.pallas import tpu as pltpu

F32 = jnp.float32
BF16 = jnp.bfloat16
HIGHEST = lax.Precision.HIGHEST

HEAD_DIM = 128
ATTN_HEADS = 4
ATTN_KV_HEADS = 2
ATTN_GROUP = ATTN_HEADS // ATTN_KV_HEADS
DN_HEADS = 4
ATTN_Q_W = ATTN_HEADS * HEAD_DIM
ATTN_KV_W = ATTN_KV_HEADS * HEAD_DIM
DN_W = DN_HEADS * HEAD_DIM
IN_MAIN_W = ATTN_Q_W + 2 * ATTN_KV_W + 3 * DN_W + DN_W
N_GATE_COLS = 4 * DN_HEADS
ROPE_THETA = 10000.0
GRID_W = 64
CONV_K = 5
CONV_PAD = CONV_K // 2
PEER_HEADS = 8
PEER_HALF = 128
N_KEYS = 128
PEER_TOPK = 16
EPS = 1e-6
NEG_BIG = -1e30
GELU_GATE_SCALE = 2.0 ** -0.5

LANES = 128
SUBLANES = 8
TOKEN_BLOCK = 256
ADA_COL_BLOCK = 1536
DN_CHUNK = 128
DN_SCAN_CHUNKS = 2
ATTN_KEY_BLOCK = 3328
ATTN_Q_BLOCK = 512
ATTN_CHAIN_ROWS = 256
PEER_TOKEN_BLOCK = 1280
PEER_I_PER_STEP = 8
VMEM_LIMIT = 56 * 1024 * 1024


def _params(sem):
    return pltpu.CompilerParams(dimension_semantics=sem, vmem_limit_bytes=VMEM_LIMIT)


def _bf16_dot(a, b):
    return jnp.dot(a.astype(BF16), b.astype(BF16), preferred_element_type=F32)


def _bf16_dot_nt(a, b):
    return lax.dot_general(a.astype(BF16), b.astype(BF16), (((1,), (1,)), ((), ())),
                           preferred_element_type=F32)


def _ada_kernel(c_ref, w_ref, b_ref, o_ref):
    c = c_ref[...]
    a = c * jax.nn.sigmoid(c)
    o_ref[0] = jnp.dot(a, w_ref[0], preferred_element_type=F32, precision=HIGHEST) + b_ref[0]


def ada_modulation(cc, ada_w, ada_b):
    depth, d, n = ada_w.shape
    tn = ADA_COL_BLOCK
    return pl.pallas_call(
        _ada_kernel,
        out_shape=jax.ShapeDtypeStruct((depth, SUBLANES, n), F32),
        grid=(depth, n // tn),
        in_specs=[pl.BlockSpec((SUBLANES, d), lambda l, j: (0, 0)),
                  pl.BlockSpec((1, d, tn), lambda l, j: (l, 0, j)),
                  pl.BlockSpec((1, 1, tn), lambda l, j: (l, 0, j))],
        out_specs=pl.BlockSpec((1, SUBLANES, tn), lambda l, j: (l, 0, j)),
        compiler_params=_params(("parallel", "parallel")),
        name="ada_modulation",
    )(cc, ada_w, ada_b.reshape(depth, 1, n))


def _mod_index(b, j):
    return (b, jnp.minimum(j, 1), 0, 0)


def _modulated_norm(x, nw, shift, scale):
    ms = jnp.mean(x * x, axis=-1, keepdims=True)
    y = x * lax.rsqrt(ms + EPS) * nw
    return y * (1.0 + scale) + shift


def _head_rmsnorm(x, w):
    return x * lax.rsqrt(jnp.mean(x * x, axis=-1, keepdims=True) + EPS) * w


def _inproj_kernel(x_ref, mod_ref, nw_ref, wm_ref, wg_ref, cos_ref, sina_ref, sinb_ref, qnw_ref, knw_ref,
                   q_out, k_out, v_out, dn_out, gate_out, ba_out):
    d = x_ref.shape[-1]
    x = x_ref[0]
    h = _modulated_norm(x, nw_ref[...], mod_ref[0, 0, :, 0:d], mod_ref[0, 0, :, d:2 * d]).astype(BF16)
    p = jnp.dot(h, wm_ref[...], preferred_element_type=F32)
    ba_out[0] = jnp.dot(h, wg_ref[...], preferred_element_type=F32)
    cos = cos_ref[...]
    sina = sina_ref[...]
    sinb = sinb_ref[...]

    def rope(t):
        return (t * cos + pltpu.roll(t, HEAD_DIM - HEAD_DIM // 4, 1) * sina
                + pltpu.roll(t, HEAD_DIM // 4, 1) * sinb)

    scale = HEAD_DIM ** -0.5 * math.log2(math.e)
    for hd in range(ATTN_HEADS):
        qh = _head_rmsnorm(p[:, hd * HEAD_DIM:(hd + 1) * HEAD_DIM], qnw_ref[...])
        q_out[0, hd] = (rope(qh) * scale).astype(BF16)
    off = ATTN_Q_W
    for hd in range(ATTN_KV_HEADS):
        kh = _head_rmsnorm(p[:, off + hd * HEAD_DIM:off + (hd + 1) * HEAD_DIM], knw_ref[...])
        k_out[0, hd] = rope(kh).astype(BF16)
    off += ATTN_KV_W
    for hd in range(ATTN_KV_HEADS):
        v_out[0, hd, :, 0:HEAD_DIM] = p[:, off + hd * HEAD_DIM:off + (hd + 1) * HEAD_DIM].astype(BF16)
        v_out[0, hd, :, HEAD_DIM:2 * HEAD_DIM] = jnp.ones((x.shape[0], HEAD_DIM), BF16)
    off += ATTN_KV_W
    dn_out[0] = p[:, off:off + 3 * DN_W]
    off += 3 * DN_W
    gate_out[0] = p[:, off:off + DN_W]


def in_projection(xs, mod, norm_w, w_main, w_gate, cos, sina, sinb, qn_w, kn_w, q_lead_blocks):
    bsz, s, d = xs.shape
    tm = TOKEN_BLOCK
    blk = lambda j: jnp.maximum(j - q_lead_blocks, 0)
    tok = lambda b, j: (b, blk(j), 0)
    head_tok = lambda b, j: (b, 0, blk(j), 0)
    full2 = lambda b, j: (0, 0)
    rope_spec = pl.BlockSpec((tm, HEAD_DIM), lambda b, j: (blk(j), 0))
    return pl.pallas_call(
        _inproj_kernel,
        out_shape=(jax.ShapeDtypeStruct((bsz, ATTN_HEADS, q_lead_blocks * tm + s, HEAD_DIM), BF16),
                   jax.ShapeDtypeStruct((bsz, ATTN_KV_HEADS, s, HEAD_DIM), BF16),
                   jax.ShapeDtypeStruct((bsz, ATTN_KV_HEADS, s, 2 * HEAD_DIM), BF16),
                   jax.ShapeDtypeStruct((bsz, s, 3 * DN_W), F32),
                   jax.ShapeDtypeStruct((bsz, s, DN_W), F32),
                   jax.ShapeDtypeStruct((bsz, s, LANES), F32)),
        grid=(bsz, q_lead_blocks + s // tm),
        in_specs=[pl.BlockSpec((1, tm, d), tok),
                  pl.BlockSpec((1, 1, 1, mod.shape[-1]), lambda b, j: _mod_index(b, blk(j))),
                  pl.BlockSpec((1, d), full2),
                  pl.BlockSpec((d, IN_MAIN_W), full2),
                  pl.BlockSpec((d, LANES), full2),
                  rope_spec, rope_spec, rope_spec,
                  pl.BlockSpec((1, HEAD_DIM), full2),
                  pl.BlockSpec((1, HEAD_DIM), full2)],
        out_specs=(pl.BlockSpec((1, ATTN_HEADS, tm, HEAD_DIM), lambda b, j: (b, 0, j, 0)),
                   pl.BlockSpec((1, ATTN_KV_HEADS, tm, HEAD_DIM), head_tok),
                   pl.BlockSpec((1, ATTN_KV_HEADS, tm, 2 * HEAD_DIM), head_tok),
                   pl.BlockSpec((1, tm, 3 * DN_W), tok),
                   pl.BlockSpec((1, tm, DN_W), tok),
                   pl.BlockSpec((1, tm, LANES), tok)),
        compiler_params=_params(("parallel", "arbitrary")),
        name="in_projection",
    )(xs, mod, norm_w.reshape(1, d), w_main, w_gate, cos, sina, sinb,
      qn_w.reshape(1, HEAD_DIM), kn_w.reshape(1, HEAD_DIM))


def _attn_kernel(q_ref, k_ref, v_ref, o_ref, m_sc, acc_sc):
    ki = pl.program_id(2)
    nk = pl.num_programs(2)

    @pl.when(ki == 0)
    def _():
        m_sc[...] = jnp.full(m_sc.shape, NEG_BIG, F32)
        acc_sc[...] = jnp.zeros(acc_sc.shape, F32)

    tq = q_ref.shape[2]
    chains = [(hd, pl.ds(r0, ATTN_CHAIN_ROWS)) for hd in range(ATTN_HEADS)
              for r0 in range(0, tq, ATTN_CHAIN_ROWS)]
    scores = [lax.dot_general(q_ref[0, hd, rows, :], k_ref[0, hd // ATTN_GROUP], (((1,), (1,)), ((), ())),
                              preferred_element_type=F32) for hd, rows in chains]
    for (hd, rows), s in zip(chains, scores):
        m_prev = m_sc[hd, rows, :]
        m_new = jnp.maximum(m_prev, jnp.max(s, axis=-1, keepdims=True))
        p = jnp.exp2(s - m_new)
        alpha = jnp.exp2(m_prev - m_new)
        acc_sc[hd, rows, :] = alpha * acc_sc[hd, rows, :] + jnp.dot(
            p.astype(BF16), v_ref[0, hd // ATTN_GROUP], preferred_element_type=F32)
        m_sc[hd, rows, :] = m_new

    @pl.when(ki == nk - 1)
    def _():
        for hd in range(ATTN_HEADS):
            o_ref[0, :, hd * HEAD_DIM:(hd + 1) * HEAD_DIM] = (
                acc_sc[hd, :, 0:HEAD_DIM] / acc_sc[hd, :, HEAD_DIM:2 * HEAD_DIM]).astype(o_ref.dtype)


def _attention_call(q, k, v, q_row0, n_q, tq, n_keys, tk, name):
    bsz = q.shape[0]
    n_q_blocks = n_q // tq
    q_block0 = q_row0 // tq
    assert q_block0 * tq == q_row0
    return pl.pallas_call(
        _attn_kernel,
        out_shape=jax.ShapeDtypeStruct((bsz, n_q, ATTN_Q_W), BF16),
        grid=(bsz, n_q_blocks, n_keys // tk),
        in_specs=[pl.BlockSpec((1, ATTN_HEADS, tq, HEAD_DIM), lambda b, i, j: (b, 0, i + q_block0, 0)),
                  pl.BlockSpec((1, ATTN_KV_HEADS, tk, HEAD_DIM), lambda b, i, j: (b, 0, j, 0)),
                  pl.BlockSpec((1, ATTN_KV_HEADS, tk, 2 * HEAD_DIM), lambda b, i, j: (b, 0, j, 0))],
        out_specs=pl.BlockSpec((1, tq, ATTN_Q_W), lambda b, i, j: (b, i, 0)),
        scratch_shapes=[pltpu.VMEM((ATTN_HEADS, tq, 1), F32),
                        pltpu.VMEM((ATTN_HEADS, tq, 2 * HEAD_DIM), F32)],
        compiler_params=_params(("parallel", "parallel", "arbitrary")),
        name=name,
    )(q, k, v)


def latent_query_block(ctx_len, n_lat):
    return ATTN_Q_BLOCK if n_lat % ATTN_Q_BLOCK == 0 and ATTN_Q_BLOCK % ctx_len == 0 else ATTN_CHAIN_ROWS


def attention(q, k, v, ctx_len):
    s = k.shape[2]
    n_lat = s - ctx_len
    lead = q.shape[2] - s
    assert ctx_len % ATTN_CHAIN_ROWS == 0 and n_lat % ATTN_CHAIN_ROWS == 0
    tk = ATTN_KEY_BLOCK if s % ATTN_KEY_BLOCK == 0 else ATTN_CHAIN_ROWS
    tq = latent_query_block(ctx_len, n_lat)
    attn_ctx = _attention_call(q, k, v, lead, ctx_len, ctx_len, ctx_len, ctx_len, "attention_ctx")
    attn_lat = _attention_call(q, k, v, lead + ctx_len, n_lat, tq, s, tk, "attention")
    return attn_ctx, attn_lat


def _dn_prep_kernel(main_ref, prev_ref, next_ref, ba_ref, cw_ref, gp_ref,
                    w_out, u_out, qg_out, kdt_out, qk_out, dl_out, ext_sc, *, ctx_chunks):
    j = pl.program_id(1)
    nj = pl.num_programs(1)
    c = DN_CHUNK
    has_prev = (j != 0) & (j != ctx_chunks)
    has_next = (j != ctx_chunks - 1) & (j != nj - 1)
    ext_sc[0:SUBLANES] = jnp.where(has_prev, prev_ref[0], 0.0)
    ext_sc[SUBLANES:SUBLANES + c] = main_ref[0]
    ext_sc[SUBLANES + c:2 * SUBLANES + c] = jnp.where(has_next, next_ref[0], 0.0)
    y = ext_sc[SUBLANES - CONV_PAD:SUBLANES - CONV_PAD + c] * cw_ref[0:1]
    for t in range(1, CONV_K):
        y = y + ext_sc[SUBLANES - CONV_PAD + t:SUBLANES - CONV_PAD + t + c] * cw_ref[t:t + 1]
    y = y * jax.nn.sigmoid(y)

    ba = ba_ref[0]
    beta_all = jax.nn.sigmoid(ba)
    g_all = -jnp.exp(gp_ref[0:1]) * jax.nn.softplus(ba + gp_ref[1:2])
    row = lax.broadcasted_iota(jnp.int32, (c, c), 0)
    col = lax.broadcasted_iota(jnp.int32, (c, c), 1)
    lower = (row >= col).astype(BF16)
    g_hi = g_all.astype(BF16)
    g_r1 = g_all - g_hi.astype(F32)
    g_mid = g_r1.astype(BF16)
    g_lo = (g_r1 - g_mid.astype(F32)).astype(BF16)
    dotf = functools.partial(jnp.dot, preferred_element_type=F32)
    prefix = dotf(lower, g_hi) + (dotf(lower, g_mid) + dotf(lower, g_lo))
    total = prefix[c - 1:c]
    gc = (prefix, total - prefix + g_all)
    gct = (gc[0].T, gc[1].T)
    row2 = lax.broadcasted_iota(jnp.int32, (2 * c, 2 * c), 0)
    col2 = lax.broadcasted_iota(jnp.int32, (2 * c, 2 * c), 1)
    eye2 = (row2 == col2).astype(F32)
    zero = jnp.zeros((c, c), F32)

    heads = range(DN_HEADS)
    a2, rhs2 = [], []
    for hd in heads:
        q = y[:, hd * HEAD_DIM:(hd + 1) * HEAD_DIM]
        k = y[:, DN_W + hd * HEAD_DIM:DN_W + (hd + 1) * HEAD_DIM]
        v = y[:, 2 * DN_W + hd * HEAD_DIM:2 * DN_W + (hd + 1) * HEAD_DIM]
        q = q * lax.rsqrt(jnp.sum(q * q, axis=-1, keepdims=True) + EPS) * (HEAD_DIM ** -0.5)
        k = k * lax.rsqrt(jnp.sum(k * k, axis=-1, keepdims=True) + EPS)
        kk = _bf16_dot_nt(k, k)
        qk = _bf16_dot_nt(q, k)
        a_dir, rhs_dir = [], []
        for dr in range(2):
            cb = dr * DN_HEADS + hd
            cg = 2 * DN_HEADS + cb
            beta = beta_all[:, cb:cb + 1]
            gcol = gc[dr][:, cg:cg + 1]
            grow = gct[dr][cg:cg + 1, :]
            incl = (row >= col) if dr == 0 else (row <= col)
            strict = (row > col) if dr == 0 else (row < col)
            decay = jnp.exp(jnp.where(incl, gcol - grow, NEG_BIG))
            a_dir.append(jnp.where(strict, kk * beta * decay, 0.0))
            eg = jnp.exp(gcol)
            rhs_dir.append(jnp.concatenate([v * beta, k * (beta * eg)], axis=1))
            qg_out[0, dr, hd] = (q * eg).astype(BF16)
            tot = total[:, cg:cg + 1]
            kdt_out[0, dr, hd] = (k * jnp.exp(tot - gcol)).T.astype(BF16)
            qk_out[0, dr, hd] = (qk * decay).astype(BF16)
            dl_out[0, dr, hd, 0] = jnp.broadcast_to(jnp.exp(tot), (1, LANES))
        a2.append(jnp.concatenate([jnp.concatenate([a_dir[0], zero], axis=1),
                                   jnp.concatenate([zero, a_dir[1]], axis=1)], axis=0))
        rhs2.append(jnp.concatenate(rhs_dir, axis=0))
    blk = 2
    base_mask = (row2 // blk) == (col2 // blk)
    x = [eye2 - jnp.where(base_mask, a2[hd], 0.0) for hd in heads]
    while blk < c:
        sibling = ((row2 // (2 * blk)) == (col2 // (2 * blk))) & ((row2 // blk) != (col2 // blk))
        fold = [_bf16_dot(x[hd], jnp.where(sibling, a2[hd], 0.0)) for hd in heads]
        x = [x[hd] - _bf16_dot(fold[hd], x[hd]) for hd in heads]
        blk *= 2
    for hd in heads:
        uw = _bf16_dot(x[hd], rhs2[hd])
        for dr in range(2):
            u_out[0, dr, hd] = uw[dr * c:(dr + 1) * c, 0:HEAD_DIM]
            w_out[0, dr, hd] = uw[dr * c:(dr + 1) * c, HEAD_DIM:2 * HEAD_DIM].astype(BF16)


def dn_prepare(dnqkv, ba, conv_w, gate_par, ctx_len):
    bsz, s, wdt = dnqkv.shape
    c = DN_CHUNK
    nc = s // c
    rows8 = s // SUBLANES
    per = c // SUBLANES
    chain = lambda b, j: (b, 0, 0, j, 0)
    return pl.pallas_call(
        functools.partial(_dn_prep_kernel, ctx_chunks=ctx_len // c),
        out_shape=(jax.ShapeDtypeStruct((bsz, 2, DN_HEADS, s, HEAD_DIM), BF16),
                   jax.ShapeDtypeStruct((bsz, 2, DN_HEADS, s, HEAD_DIM), F32),
                   jax.ShapeDtypeStruct((bsz, 2, DN_HEADS, s, HEAD_DIM), BF16),
                   jax.ShapeDtypeStruct((bsz, 2, DN_HEADS, HEAD_DIM, s), BF16),
                   jax.ShapeDtypeStruct((bsz, 2, DN_HEADS, s, c), BF16),
                   jax.ShapeDtypeStruct((bsz, 2, DN_HEADS, nc, 1, LANES), F32)),
        grid=(bsz, nc),
        in_specs=[pl.BlockSpec((1, c, wdt), lambda b, j: (b, j, 0)),
                  pl.BlockSpec((1, SUBLANES, wdt), lambda b, j: (b, jnp.maximum(j * per - 1, 0), 0)),
                  pl.BlockSpec((1, SUBLANES, wdt), lambda b, j: (b, jnp.minimum((j + 1) * per, rows8 - 1), 0)),
                  pl.BlockSpec((1, c, LANES), lambda b, j: (b, j, 0)),
                  pl.BlockSpec((SUBLANES, wdt), lambda b, j: (0, 0)),
                  pl.BlockSpec((SUBLANES, LANES), lambda b, j: (0, 0))],
        out_specs=(pl.BlockSpec((1, 2, DN_HEADS, c, HEAD_DIM), chain),
                   pl.BlockSpec((1, 2, DN_HEADS, c, HEAD_DIM), chain),
                   pl.BlockSpec((1, 2, DN_HEADS, c, HEAD_DIM), chain),
                   pl.BlockSpec((1, 2, DN_HEADS, HEAD_DIM, c), lambda b, j: (b, 0, 0, 0, j)),
                   pl.BlockSpec((1, 2, DN_HEADS, c, c), chain),
                   pl.BlockSpec((1, 2, DN_HEADS, 1, 1, LANES), lambda b, j: (b, 0, 0, j, 0, 0))),
        scratch_shapes=[pltpu.VMEM((c + 2 * SUBLANES, wdt), F32)],
        compiler_params=_params(("parallel", "parallel")),
        name="dn_prepare",
    )(dnqkv, dnqkv, dnqkv, ba, conv_w, gate_par)


def _dn_scan_kernel(*refs, bsz):
    ins = refs[:12]
    of_ref, ob_ref, s_sc = refs[12:]
    n = pl.program_id(0)

    @pl.when(n == 0)
    def _():
        s_sc[...] = jnp.zeros(s_sc.shape, F32)

    chains = [(dr, b, hd) for dr in range(2) for b in range(bsz) for hd in range(DN_HEADS)]
    dotf = functools.partial(jnp.dot, preferred_element_type=F32)

    def inp(dr, k):
        return ins[dr * 6 + k]

    state = [s_sc[ci] for ci in range(len(chains))]
    per_step = inp(0, 0).shape[3] // DN_CHUNK
    for k in range(per_step):
        rows = [pl.ds((k if dr == 0 else per_step - 1 - k) * DN_CHUNK, DN_CHUNK) for dr in range(2)]
        sub = [(k if dr == 0 else per_step - 1 - k) for dr in range(2)]
        r = [dotf(jnp.concatenate([inp(dr, 0)[b, 0, hd, rows[dr], :], inp(dr, 2)[b, 0, hd, rows[dr], :]], axis=0),
                  state[ci].astype(BF16)) for ci, (dr, b, hd) in enumerate(chains)]
        v_new = [(inp(dr, 1)[b, 0, hd, rows[dr], :] - r[ci][0:DN_CHUNK]).astype(BF16)
                 for ci, (dr, b, hd) in enumerate(chains)]
        intra = [dotf(inp(dr, 4)[b, 0, hd, rows[dr], :], v_new[ci]) for ci, (dr, b, hd) in enumerate(chains)]
        upd = [dotf(inp(dr, 3)[b, 0, hd, :, rows[dr]], v_new[ci]) for ci, (dr, b, hd) in enumerate(chains)]
        for ci, (dr, b, hd) in enumerate(chains):
            o_ref = of_ref if dr == 0 else ob_ref
            o_ref[b, rows[dr], hd * HEAD_DIM:(hd + 1) * HEAD_DIM] = r[ci][DN_CHUNK:] + intra[ci]
            state[ci] = state[ci] * inp(dr, 5)[b, 0, hd, sub[dr], 0] + upd[ci]
    for ci in range(len(chains)):
        s_sc[ci] = state[ci]


def dn_scan(w, u, qg, kdt, qk, dl, ctx_len):
    bsz, _, _, s, _ = w.shape
    per = DN_SCAN_CHUNKS if (s // DN_CHUNK) % DN_SCAN_CHUNKS == 0 and (ctx_len // DN_CHUNK) % DN_SCAN_CHUNKS == 0 else 1
    c = per * DN_CHUNK
    nc = s // c
    cc = ctx_len // c

    def bwd_chunk(n):
        return jnp.where(n < cc, cc - 1 - n, nc - 1 - (n - cc))

    in_specs, args = [], []
    for dr in range(2):
        pos = (lambda n: n) if dr == 0 else bwd_chunk
        tokm = lambda n, dr=dr, pos=pos: (0, dr, 0, pos(n), 0)
        for arr in (w, u, qg):
            in_specs.append(pl.BlockSpec((bsz, 1, DN_HEADS, c, HEAD_DIM), tokm))
            args.append(arr)
        in_specs.append(pl.BlockSpec((bsz, 1, DN_HEADS, HEAD_DIM, c), lambda n, dr=dr, pos=pos: (0, dr, 0, 0, pos(n))))
        args.append(kdt)
        in_specs.append(pl.BlockSpec((bsz, 1, DN_HEADS, c, DN_CHUNK), tokm))
        args.append(qk)
        in_specs.append(pl.BlockSpec((bsz, 1, DN_HEADS, per, 1, LANES), lambda n, dr=dr, pos=pos: (0, dr, 0, pos(n), 0, 0)))
        args.append(dl)
    return pl.pallas_call(
        functools.partial(_dn_scan_kernel, bsz=bsz),
        out_shape=(jax.ShapeDtypeStruct((bsz, s, DN_W), F32), jax.ShapeDtypeStruct((bsz, s, DN_W), F32)),
        grid=(nc,),
        in_specs=in_specs,
        out_specs=(pl.BlockSpec((bsz, c, DN_W), lambda n: (0, n, 0)),
                   pl.BlockSpec((bsz, c, DN_W), lambda n: (0, bwd_chunk(n), 0))),
        scratch_shapes=[pltpu.VMEM((2 * bsz * DN_HEADS, HEAD_DIM, HEAD_DIM), F32)],
        compiler_params=_params(("arbitrary",)),
        name="dn_scan",
    )(*args)


def _outproj_kernel(x_ref, mod_ref, attn_ctx_ref, attn_lat_ref, of_ref, ob_ref, gate_ref, dnw_ref, wo_ref,
                    n2w_ref, wq_ref, sk_ref, x_out, h2_out, st_out):
    d = x_ref.shape[-1]
    o = of_ref[0] + ob_ref[0]
    gate = gate_ref[0]
    parts = [jnp.where(pl.program_id(1) == 0, attn_ctx_ref[0], attn_lat_ref[0])]
    for hd in range(DN_HEADS):
        sl = slice(hd * HEAD_DIM, (hd + 1) * HEAD_DIM)
        g = gate[:, sl]
        parts.append((_head_rmsnorm(o[:, sl], dnw_ref[...]) * (g * jax.nn.sigmoid(g))).astype(BF16))
    mix = jnp.concatenate(parts, axis=1)
    y = jnp.dot(mix, wo_ref[...], preferred_element_type=F32)
    x = x_ref[0] + mod_ref[0, 0, :, 2 * d:3 * d] * y
    x_out[0] = x
    h2 = _modulated_norm(x, n2w_ref[...], mod_ref[0, 0, :, 3 * d:4 * d],
                         mod_ref[0, 0, :, 4 * d:5 * d]).astype(BF16)
    h2_out[0] = h2
    q = jnp.dot(h2, wq_ref[...], preferred_element_type=F32).astype(BF16)
    for hp in range(2 * PEER_HEADS):
        st_out[0, hp] = lax.dot_general(sk_ref[hp], q[:, hp * PEER_HALF:(hp + 1) * PEER_HALF],
                                        (((1,), (1,)), ((), ())), preferred_element_type=F32)


def out_projection(xs, mod, attn_ctx, attn_lat, o_f, o_b, gate, dn_norm_w, w_out, norm2_w, wq, subkeys):
    bsz, s, d = xs.shape
    tm = TOKEN_BLOCK
    nhp = 2 * PEER_HEADS
    assert attn_ctx.shape[1] == tm
    tok = lambda b, j: (b, j, 0)
    full2 = lambda b, j: (0, 0)
    return pl.pallas_call(
        _outproj_kernel,
        out_shape=(jax.ShapeDtypeStruct((bsz, s, d), F32), jax.ShapeDtypeStruct((bsz, s, d), BF16),
                   jax.ShapeDtypeStruct((bsz, nhp, N_KEYS, s), F32)),
        grid=(bsz, s // tm),
        in_specs=[pl.BlockSpec((1, tm, d), tok),
                  pl.BlockSpec((1, 1, 1, mod.shape[-1]), _mod_index),
                  pl.BlockSpec((1, tm, ATTN_Q_W), lambda b, j: (b, 0, 0)),
                  pl.BlockSpec((1, tm, ATTN_Q_W), lambda b, j: (b, jnp.maximum(j - 1, 0), 0)),
                  pl.BlockSpec((1, tm, DN_W), tok),
                  pl.BlockSpec((1, tm, DN_W), tok),
                  pl.BlockSpec((1, tm, DN_W), tok),
                  pl.BlockSpec((1, HEAD_DIM), full2),
                  pl.BlockSpec(w_out.shape, full2),
                  pl.BlockSpec((1, d), full2),
                  pl.BlockSpec(wq.shape, full2),
                  pl.BlockSpec(subkeys.shape, lambda b, j: (0, 0, 0))],
        out_specs=(pl.BlockSpec((1, tm, d), tok), pl.BlockSpec((1, tm, d), tok),
                   pl.BlockSpec((1, nhp, N_KEYS, tm), lambda b, j: (b, 0, 0, j))),
        compiler_params=_params(("parallel", "parallel")),
        name="out_projection",
    )(xs, mod, attn_ctx, attn_lat, o_f, o_b, gate, dn_norm_w.reshape(1, HEAD_DIM), w_out,
      norm2_w.reshape(1, d), wq, subkeys)


def _sorting_network(n):
    pairs = []
    p = 1
    while p < n:
        k = p
        while k >= 1:
            for j in range(k % p, n - k, 2 * k):
                for i in range(min(k, n - j - k)):
                    if (i + j) // (2 * p) == (i + j + k) // (2 * p):
                        pairs.append((i + j, i + j + k))
            k //= 2
        p *= 2
    return pairs


def _sorted_top(s, k, with_rank):
    n_tiles = s.shape[0] // SUBLANES
    v = [s[i * SUBLANES:(i + 1) * SUBLANES] for i in range(n_tiles)]
    for lo, hi in _sorting_network(n_tiles):
        v[lo], v[hi] = jnp.maximum(v[lo], v[hi]), jnp.minimum(v[lo], v[hi])
    out = []
    for r in range(k):
        m = jnp.max(v[0], axis=0, keepdims=True)
        out.append(m)
        hit = v[0] == m
        for i in range(min(n_tiles, k - 1 - r)):
            v[i] = jnp.where(hit, v[i + 1] if i + 1 < n_tiles else NEG_BIG, v[i])
    rank = None
    if with_rank:
        rank = jnp.full(s.shape, float(k), F32)
        for r in reversed(range(k)):
            rank = jnp.where(s >= out[r], float(r), rank)
    return out, rank


def _peer_topk_kernel(st_ref, cnt_out, e0_out, rank_out, e1_out):
    nt = PEER_TOPK + 1
    tops = ([], [])
    for hd in range(PEER_HEADS):
        a, _ = _sorted_top(st_ref[0, 2 * hd], nt, False)
        b, rank = _sorted_top(st_ref[0, 2 * hd + 1], nt, True)
        rank_out[0, hd] = rank.astype(BF16)
        tops[0].append(a)
        tops[1].append(b)
    a8 = [jnp.concatenate([tops[0][hd][r] for hd in range(PEER_HEADS)], axis=0) for r in range(nt)]
    b8 = [jnp.concatenate([tops[1][hd][r] for hd in range(PEER_HEADS)], axis=0) for r in range(nt)]
    cand = [a8[i] + b8[j] for i in range(nt) for j in range(nt) if (i + 1) * (j + 1) <= nt]
    top = []
    for _ in range(nt):
        m = functools.reduce(jnp.maximum, cand)
        top.append(m)
        cand = [jnp.where(t == m, NEG_BIG, t) for t in cand]
    tau8 = 0.5 * (top[PEER_TOPK - 1] + top[PEER_TOPK])
    smax = a8[0] + b8[0]
    z8 = jnp.exp(top[0] - smax)
    for t in top[1:PEER_TOPK]:
        z8 = z8 + jnp.exp(t - smax)
    rz8 = 1.0 / z8
    for hd in range(PEER_HEADS):
        s0 = st_ref[0, 2 * hd]
        s1 = st_ref[0, 2 * hd + 1]
        thr = tau8[hd:hd + 1] - s0
        cnt = jnp.zeros(s0.shape, F32)
        for r in range(nt):
            cnt = jnp.where(tops[1][hd][r] > thr, float(r + 1), cnt)
        cnt_out[0, hd] = cnt
        e0_out[0, hd] = jnp.exp(s0 - tops[0][hd][0])
        e1_out[0, hd] = (jnp.exp(s1 - tops[1][hd][0]) * (rz8[hd:hd + 1] * GELU_GATE_SCALE)).astype(BF16)


def peer_topk(st):
    bsz, nhp, nk, s = st.shape
    tl = LANES
    spec = pl.BlockSpec((1, PEER_HEADS, nk, tl), lambda b, j: (b, 0, 0, j))
    words = jax.ShapeDtypeStruct((bsz, PEER_HEADS, nk, s), F32)
    halfs = jax.ShapeDtypeStruct((bsz, PEER_HEADS, nk, s), BF16)
    return pl.pallas_call(
        _peer_topk_kernel,
        out_shape=(words, words, halfs, halfs),
        grid=(bsz, s // tl),
        in_specs=[pl.BlockSpec((1, nhp, nk, tl), lambda b, j: (b, 0, 0, j))],
        out_specs=(spec, spec, spec, spec),
        compiler_params=_params(("parallel", "parallel")),
        name="peer_topk",
    )(st)


def _peer_expert_kernel(x_ref, mod_ref, h2_ref, u_ref, vt_ref, cnt_ref, e0_ref, rank_ref, e1_ref,
                        x_out, acc_sc, *, ctx_len):
    tok_block = pl.program_id(1)
    ec = pl.program_id(2)
    n_ec = pl.num_programs(2)
    d = x_ref.shape[-1]
    tb = x_ref.shape[1]
    pk = 2 * SUBLANES

    @pl.when(ec == 0)
    def _():
        acc_sc[...] = jnp.zeros(acc_sc.shape, F32)

    def row_tile(ref, hd, ii):
        return jnp.broadcast_to(ref[0, hd, ii:ii + 1, :], (pk, tb)).astype(BF16)

    h2 = h2_ref[0]
    pair = 2 * N_KEYS
    n_pairs = PEER_I_PER_STEP // 2

    def activations(p):
        return lax.dot_general(u_ref[p * pair:(p + 1) * pair, :], h2, (((1,), (1,)), ((), ())),
                               preferred_element_type=F32)

    def gate_weights(p):
        tiles = []
        for ii in (2 * p, 2 * p + 1):
            wt = [jnp.zeros((pk, tb), BF16) for _ in range(N_KEYS // pk)]
            for hd in range(PEER_HEADS):
                cnt = row_tile(cnt_ref, hd, ii)
                e0 = row_tile(e0_ref, hd, ii)
                for rt in range(N_KEYS // pk):
                    rows = slice(rt * pk, (rt + 1) * pk)
                    sel = jnp.where(rank_ref[0, hd, rows, :] < cnt, e1_ref[0, hd, rows, :],
                                    jnp.zeros((), BF16))
                    wt[rt] = wt[rt] + sel * e0
            tiles += wt
        return tiles

    def gated(at2, tiles):
        g = []
        for rt, wt in enumerate(tiles):
            a = at2[rt * pk:(rt + 1) * pk]
            act = a * (1.0 + lax.erf(a))
            g.append(act.astype(BF16) * wt)
        return jnp.concatenate(g, axis=0)

    g = [gated(activations(p), gate_weights(p)) for p in range(n_pairs)]
    acc_sc[...] += jnp.dot(vt_ref[...], jnp.concatenate(g, axis=0), preferred_element_type=F32)

    @pl.when(ec == n_ec - 1)
    def _():
        tok = tok_block * tb + lax.broadcasted_iota(jnp.int32, (tb, 1), 0)
        g2 = jnp.where(tok < ctx_len, mod_ref[0, 0, :, 5 * d:6 * d], mod_ref[0, 1, :, 5 * d:6 * d])
        x_out[0] = x_ref[0] + g2 * acc_sc[...].T


def peer_experts(xs, mod, h2, u_tab, vt_tab, cnt, e0, rank, e1, ctx_len):
    bsz, s, d = xs.shape
    tb = PEER_TOKEN_BLOCK if s % PEER_TOKEN_BLOCK == 0 else TOKEN_BLOCK
    ech = PEER_I_PER_STEP * N_KEYS
    n_exp = u_tab.shape[0]
    tok = lambda b, j, e: (b, j, 0)
    per_tok = lambda b, j, e: (b, 0, 0, j)
    per_i = lambda b, j, e: (b, 0, e, j)
    once = dict(pipeline_mode=pl.Buffered(1))
    return pl.pallas_call(
        functools.partial(_peer_expert_kernel, ctx_len=ctx_len),
        out_shape=jax.ShapeDtypeStruct((bsz, s, d), F32),
        grid=(bsz, s // tb, n_exp // ech),
        in_specs=[pl.BlockSpec((1, tb, d), tok, **once),
                  pl.BlockSpec((1, 2, 1, mod.shape[-1]), lambda b, j, e: (b, 0, 0, 0)),
                  pl.BlockSpec((1, tb, d), tok, **once),
                  pl.BlockSpec((ech, d), lambda b, j, e: (e, 0)),
                  pl.BlockSpec((d, ech), lambda b, j, e: (0, e)),
                  pl.BlockSpec((1, PEER_HEADS, PEER_I_PER_STEP, tb), per_i),
                  pl.BlockSpec((1, PEER_HEADS, PEER_I_PER_STEP, tb), per_i),
                  pl.BlockSpec((1, PEER_HEADS, N_KEYS, tb), per_tok, **once),
                  pl.BlockSpec((1, PEER_HEADS, N_KEYS, tb), per_tok, **once)],
        out_specs=pl.BlockSpec((1, tb, d), tok, **once),
        scratch_shapes=[pltpu.VMEM((d, tb), F32)],
        compiler_params=_params(("parallel", "parallel", "arbitrary")),
        name="peer_experts",
    )(xs, mod, h2, u_tab, vt_tab, cnt, e0, rank, e1)


def _rope_tables(ctx_len, n_lat):
    rows = n_lat // GRID_W
    row = jnp.repeat(jnp.arange(rows, dtype=F32), GRID_W)
    col = jnp.tile(jnp.arange(GRID_W, dtype=F32), rows)
    axis_dim = HEAD_DIM // 2
    inv_freq = ROPE_THETA ** (-jnp.arange(0, axis_dim, 2, dtype=F32) / axis_dim)
    ang_r = row[:, None] * inv_freq[None, :]
    ang_c = col[:, None] * inv_freq[None, :]
    ang = jnp.concatenate([ang_r, ang_r, ang_c, ang_c], axis=-1)
    cos, sin = jnp.cos(ang), jnp.sin(ang)
    first = (jnp.arange(HEAD_DIM) % (HEAD_DIM // 2)) < (HEAD_DIM // 4)
    sina = jnp.where(first, -sin, 0.0)
    sinb = jnp.where(first, 0.0, sin)
    pad = lambda t, v: jnp.concatenate([jnp.full((ctx_len, HEAD_DIM), v, F32), t], axis=0)
    return pad(cos, 1.0), pad(sina, 0.0), pad(sinb, 0.0)


def kernel(x, c, ctx, c_ctx, ada_w, ada_b, norm1_w, norm2_w, w_in, attn_qnorm_w, attn_knorm_w, dn_conv_w,
           dn_A_log, dn_dt_bias, dn_norm_w, w_out, peer_wq, peer_subkeys, peer_u, peer_v):
    bsz, n_lat, d = x.shape
    ctx_len = ctx.shape[1]
    depth = ada_w.shape[0]
    assert ctx_len == TOKEN_BLOCK and n_lat % TOKEN_BLOCK == 0 and bsz + 1 <= SUBLANES
    assert w_in.shape[-1] == IN_MAIN_W + N_GATE_COLS

    xs = jnp.concatenate([ctx, x], axis=1)
    cos, sina, sinb = _rope_tables(ctx_len, n_lat)
    q_lead_blocks = (-ctx_len % latent_query_block(ctx_len, n_lat)) // TOKEN_BLOCK

    cc = jnp.zeros((SUBLANES, d), F32).at[:bsz].set(c).at[bsz].set(c_ctx)
    mod_all = ada_modulation(cc, ada_w, ada_b)
    mod_ctx = jnp.broadcast_to(mod_all[:, bsz][:, None], (depth, bsz, 6 * d))
    mod = jnp.stack([mod_ctx, mod_all[:, :bsz]], axis=2)[:, :, :, None, :]

    for l in range(depth):
        w_main = w_in[l, :, :IN_MAIN_W].astype(BF16)
        w_gate = jnp.pad(w_in[l, :, IN_MAIN_W:], ((0, 0), (0, LANES - N_GATE_COLS))).astype(BF16)
        q, k, v, dnqkv, gate, ba = in_projection(xs, mod[l], norm1_w[l], w_main, w_gate, cos, sina, sinb,
                                                 attn_qnorm_w[l], attn_knorm_w[l], q_lead_blocks)
        attn_ctx, attn_lat = attention(q, k, v, ctx_len)
        conv_w = jnp.pad(dn_conv_w[l], ((0, SUBLANES - CONV_K), (0, 0)))
        gate_par = jnp.zeros((SUBLANES, LANES), F32)
        gate_par = gate_par.at[0, 2 * DN_HEADS:4 * DN_HEADS].set(dn_A_log[l].reshape(-1))
        gate_par = gate_par.at[1, 2 * DN_HEADS:4 * DN_HEADS].set(dn_dt_bias[l].reshape(-1))
        o_f, o_b = dn_scan(*dn_prepare(dnqkv, ba, conv_w, gate_par, ctx_len), ctx_len)
        sk = peer_subkeys[l].reshape(2 * PEER_HEADS, N_KEYS, PEER_HALF).astype(BF16)
        xs, h2, st = out_projection(xs, mod[l], attn_ctx, attn_lat, o_f, o_b, gate, dn_norm_w[l],
                                    w_out[l].astype(BF16), norm2_w[l], peer_wq[l].astype(BF16), sk)
        cnt, e0, rank, e1 = peer_topk(st)
        xs = peer_experts(xs, mod[l], h2, (peer_u[l] * GELU_GATE_SCALE).astype(BF16), peer_v[l].T.astype(BF16), cnt, e0, rank, e1,
                          ctx_len)
    return xs[:, ctx_len:]
```
